```python
import math
import jax, jax.numpy as jnp
from jax import lax
import numpy as np

D_MODEL = 4096
BATCH = 4
SEQ = 4096
DEPTH = 1

MIX_WIDTH = D_MODEL
ATTN_WIDTH = MIX_WIDTH // 2
SGU_WIDTH = MIX_WIDTH - ATTN_WIDTH
DIFF_HEAD_DIM = 64
DIFF_V_DIM = 2 * DIFF_HEAD_DIM
N_DIFF_HEADS = ATTN_WIDTH // DIFF_V_DIM
N_SOFTMAX_MAPS = 2 * N_DIFF_HEADS
Q_COLS = N_DIFF_HEADS * 2 * DIFF_HEAD_DIM
CHUNK = 128
SGU_HEAD_DIM = 128
N_SGU_HEADS = SGU_WIDTH // SGU_HEAD_DIM
IN_COLS = 2 * Q_COLS + ATTN_WIDTH + 2 * SGU_WIDTH
N_BUCKETS = 32
MAX_DISTANCE = 128
Q_BLOCK = 128
N_EXPERTS = 32
TOP_K = 4
EXPERT_FF = D_MODEL // 2
SWIGLU_LIMIT = 7.0
SWIGLU_ALPHA = 1.702
MOE_BLOCK = 128
NORM_EPS = 1e-5

kernel_name = "hymba_diffattn_sgu_moe_layer"


def _rmsnorm(x, g):
    x32 = x.astype(jnp.float32)
    y = x32 * lax.rsqrt(jnp.mean(x32 * x32, axis=-1, keepdims=True) + NORM_EPS)
    return (y * g.astype(jnp.float32)).astype(x.dtype)


def _layernorm(x, g, b):
    x32 = x.astype(jnp.float32)
    mu = jnp.mean(x32, axis=-1, keepdims=True)
    xc = x32 - mu
    y = xc * lax.rsqrt(jnp.mean(xc * xc, axis=-1, keepdims=True) + NORM_EPS)
    return (y * g.astype(jnp.float32) + b.astype(jnp.float32)).astype(x.dtype)


def _t5_bucket(dist):
    max_exact = N_BUCKETS // 2
    d = jnp.maximum(dist, 1).astype(jnp.float32)
    large = max_exact + (jnp.log(d / max_exact) / math.log(MAX_DISTANCE / max_exact)
                         * (N_BUCKETS - max_exact)).astype(jnp.int32)
    large = jnp.minimum(large, N_BUCKETS - 1)
    return jnp.where(dist < max_exact, dist, large)


def _diff_attention(q, k, v, rel_bias, lam):
    b_, n_h, _, s_len, d = q.shape
    scale = d ** -0.5
    n_blocks = s_len // Q_BLOCK
    kpos = jnp.arange(s_len, dtype=jnp.int32)
    table = rel_bias.astype(jnp.float32)

    def one_block(i):
        start = i * Q_BLOCK
        qb = lax.dynamic_slice_in_dim(q, start, Q_BLOCK, axis=3)
        s = jnp.einsum('bhmqd,bhmkd->bhmqk', qb, k).astype(jnp.float32) * scale
        dist = (start + jnp.arange(Q_BLOCK, dtype=jnp.int32))[:, None] - kpos[None, :]
        bias = table[_t5_bucket(jnp.maximum(dist, 0))]
        bias = jnp.transpose(bias, (2, 0, 1)).reshape(n_h, 2, Q_BLOCK, s_len)
        s = jnp.where(dist >= 0, s + bias, -jnp.inf)
        p = jax.nn.softmax(s, axis=-1)
        a = p[:, :, 0] - lam * p[:, :, 1]
        return jnp.einsum('bhqk,bhkc->bhqc', a.astype(v.dtype), v)

    o = lax.map(one_block, jnp.arange(n_blocks, dtype=jnp.int32))
    return jnp.transpose(o, (1, 0, 3, 2, 4)).reshape(b_, s_len, n_h, v.shape[-1])


def _spatial_gating(z, ln_g, ln_b, w_s, b_s):
    u, v = jnp.split(z, 2, axis=-1)
    v = _layernorm(v, ln_g, ln_b)
    b_, s_len, w = v.shape
    v = v.reshape(b_, s_len // CHUNK, CHUNK, N_SGU_HEADS, SGU_HEAD_DIM)
    w_causal = w_s * jnp.tril(jnp.ones((CHUNK, CHUNK), w_s.dtype))
    y = jnp.einsum('hts,bnshc->bnthc', w_causal, v) + b_s.T[None, None, :, :, None]
    return u * y.reshape(b_, s_len, w)


def _moe(x, router_w, router_b, w_gu, b_gu, w_dn, b_dn):
    b_, s_len, d = x.shape
    n_tok = b_ * s_len
    xt = x.reshape(n_tok, d)
    logits = (xt @ router_w + router_b).astype(jnp.float32)
    top_vals, top_idx = lax.top_k(logits, TOP_K)
    gates = jax.nn.softmax(top_vals, axis=-1)

    n_pairs = n_tok * TOP_K
    n_blocks = -(-n_pairs // MOE_BLOCK) + N_EXPERTS
    n_rows = n_blocks * MOE_BLOCK
    flat_e = top_idx.reshape(n_pairs).astype(jnp.int32)
    flat_tok = jnp.repeat(jnp.arange(n_tok, dtype=jnp.int32), TOP_K)
    flat_g = gates.reshape(n_pairs)
    order = jnp.argsort(flat_e)
    se, stok, sg = flat_e[order], flat_tok[order], flat_g[order]
    counts = jnp.bincount(flat_e, length=N_EXPERTS).astype(jnp.int32)
    starts = jnp.cumsum(counts) - counts
    pcounts = (counts + MOE_BLOCK - 1) // MOE_BLOCK * MOE_BLOCK
    pends = jnp.cumsum(pcounts)
    pstarts = pends - pcounts
    dest = pstarts[se] + jnp.arange(n_pairs, dtype=jnp.int32) - starts[se]
    row_tok = jnp.zeros((n_rows,), jnp.int32).at[dest].set(stok)
    row_gate = jnp.zeros((n_rows,), jnp.float32).at[dest].set(sg)
    block_e = jnp.minimum(
        jnp.searchsorted(pends, jnp.arange(n_blocks, dtype=jnp.int32) * MOE_BLOCK, side='right'),
        N_EXPERTS - 1).astype(jnp.int32)

    def body(acc, blk):
        e, toks, g = blk
        xb = xt[toks]
        gu = (xb @ w_gu[e] + b_gu[e]).astype(jnp.float32)
        gate = jnp.minimum(gu[:, :EXPERT_FF], SWIGLU_LIMIT)
        up = jnp.clip(gu[:, EXPERT_FF:], -SWIGLU_LIMIT, SWIGLU_LIMIT)
        h = ((up + 1.0) * gate * jax.nn.sigmoid(SWIGLU_ALPHA * gate)).astype(xt.dtype)
        y = (h @ w_dn[e] + b_dn[e]) * g[:, None].astype(xt.dtype)
        return acc.at[toks].add(y), None

    out, _ = lax.scan(body, jnp.zeros_like(xt),
                      (block_e, row_tok.reshape(n_blocks, MOE_BLOCK),
                       row_gate.reshape(n_blocks, MOE_BLOCK)))
    return out.reshape(b_, s_len, d)


def setup_inputs(seed: int = 0) -> dict:
    key = jax.random.key(seed)
    ks = jax.random.split(key, 24)
    f32 = jnp.float32
    nrm = lambda k, shape, s: jax.random.normal(k, shape, f32) * s
    return {
        "x": nrm(ks[0], (BATCH, SEQ, D_MODEL), 1.0),
        "attn_norm_g": 1.0 + nrm(ks[1], (DEPTH, D_MODEL), 0.02),
        "w_in": nrm(ks[2], (DEPTH, D_MODEL, IN_COLS), D_MODEL ** -0.5),
        "lambda_q1": nrm(ks[3], (DEPTH, DIFF_HEAD_DIM), 0.1),
        "lambda_k1": nrm(ks[4], (DEPTH, DIFF_HEAD_DIM), 0.1),
        "lambda_q2": nrm(ks[5], (DEPTH, DIFF_HEAD_DIM), 0.1),
        "lambda_k2": nrm(ks[6], (DEPTH, DIFF_HEAD_DIM), 0.1),
        "diff_subln_g": 1.0 + nrm(ks[7], (DEPTH, DIFF_V_DIM), 0.02),
        "sgu_ln_g": 1.0 + nrm(ks[8], (DEPTH, SGU_WIDTH), 0.02),
        "sgu_ln_b": nrm(ks[9], (DEPTH, SGU_WIDTH), 0.02),
        "sgu_w": nrm(ks[10], (DEPTH, N_SGU_HEADS, CHUNK, CHUNK), CHUNK ** -0.5),
        "sgu_b": 1.0 + nrm(ks[11], (DEPTH, N_SGU_HEADS, CHUNK), 0.1),
        "rel_bias": nrm(ks[12], (N_BUCKETS, N_SOFTMAX_MAPS), 0.5),
        "w_out": nrm(ks[13], (DEPTH, MIX_WIDTH, D_MODEL), MIX_WIDTH ** -0.5),
        "ffn_norm_g": 1.0 + nrm(ks[14], (DEPTH, D_MODEL), 0.02),
        "router_w": nrm(ks[15], (DEPTH, D_MODEL, N_EXPERTS), D_MODEL ** -0.5),
        "router_b": nrm(ks[16], (DEPTH, N_EXPERTS), 0.01),
        "w_gate_up": nrm(ks[17], (DEPTH, N_EXPERTS, D_MODEL, 2 * EXPERT_FF), D_MODEL ** -0.5),
        "b_gate_up": nrm(ks[18], (DEPTH, N_EXPERTS, 2 * EXPERT_FF), 0.01),
        "w_down": nrm(ks[19], (DEPTH, N_EXPERTS, EXPERT_FF, D_MODEL), EXPERT_FF ** -0.5),
        "b_down": nrm(ks[20], (DEPTH, N_EXPERTS, D_MODEL), 0.01),
        "final_norm_g": 1.0 + nrm(ks[21], (D_MODEL,), 0.02),
    }


def reference(x, attn_norm_g, w_in, lambda_q1, lambda_k1, lambda_q2, lambda_k2,
              diff_subln_g, sgu_ln_g, sgu_ln_b, sgu_w, sgu_b, rel_bias, w_out,
              ffn_norm_g, router_w, router_b, w_gate_up, b_gate_up, w_down, b_down,
              final_norm_g):
    b_, s_len, _ = x.shape
    for l in range(DEPTH):
        lambda_init = 0.8 - 0.6 * math.exp(-0.3 * l)
        h = _rmsnorm(x, attn_norm_g[l])
        proj = h @ w_in[l]
        q = proj[..., :Q_COLS].reshape(b_, s_len, N_DIFF_HEADS, 2, DIFF_HEAD_DIM)
        k = proj[..., Q_COLS:2 * Q_COLS].reshape(b_, s_len, N_DIFF_HEADS, 2, DIFF_HEAD_DIM)
        v = proj[..., 2 * Q_COLS:2 * Q_COLS + ATTN_WIDTH].reshape(b_, s_len, N_DIFF_HEADS, DIFF_V_DIM)
        q = jnp.transpose(q, (0, 2, 3, 1, 4))
        k = jnp.transpose(k, (0, 2, 3, 1, 4))
        v = jnp.transpose(v, (0, 2, 1, 3))
        lam = (jnp.exp(jnp.sum(lambda_q1[l].astype(jnp.float32) * lambda_k1[l].astype(jnp.float32)))
               - jnp.exp(jnp.sum(lambda_q2[l].astype(jnp.float32) * lambda_k2[l].astype(jnp.float32)))
               + lambda_init)
        att = _diff_attention(q, k, v, rel_bias, lam)
        att = (_rmsnorm(att, diff_subln_g[l]) * (1.0 - lambda_init)).reshape(b_, s_len, ATTN_WIDTH)
        z = jax.nn.gelu(proj[..., 2 * Q_COLS + ATTN_WIDTH:], approximate=False)
        sgu = _spatial_gating(z, sgu_ln_g[l], sgu_ln_b[l], sgu_w[l], sgu_b[l])
        x = x + jnp.concatenate([att, sgu], axis=-1) @ w_out[l]
        h = _rmsnorm(x, ffn_norm_g[l])
        x = x + _moe(h, router_w[l], router_b[l], w_gate_up[l], b_gate_up[l],
                     w_down[l], b_down[l])
    return _rmsnorm(x, final_norm_g)
```

```python
import functools
import math

import jax
import jax.numpy as jnp
from jax import lax
from jax.experimental import pallas as pl
from jax.experimental.pallas import tpu as pltpu

F32 = jnp.float32
BF16 = jnp.bfloat16

TOP_K = 4
MAX_DISTANCE = 128
SWIGLU_LIMIT = 7.0
SWIGLU_ALPHA = 1.702
NORM_EPS = 1e-5
LAMBDA_INIT = 0.8 - 0.6 * math.exp(0.0)
MASK_VALUE = -1e30

LANES = 128
VMEM_LIMIT_BYTES = 56 * 1024 * 1024


def _tile(dim, want):
    t = min(dim, want)
    while dim % t:
        t -= LANES
    assert t > 0, (dim, want)
    return t


def _params(*semantics):
    return pltpu.CompilerParams(dimension_semantics=semantics,
                                vmem_limit_bytes=VMEM_LIMIT_BYTES)


def _rmsnorm_kernel(x_ref, g_ref, o_ref):
    x = x_ref[...]
    ms = jnp.mean(x * x, axis=-1, keepdims=True)
    o_ref[...] = (x * lax.rsqrt(ms + NORM_EPS) * g_ref[...]).astype(o_ref.dtype)


def _rmsnorm_rows(x, g, out_dtype):
    n, d = x.shape
    tr = _tile(n, 256)
    return pl.pallas_call(
        _rmsnorm_kernel,
        grid=(n // tr,),
        in_specs=[pl.BlockSpec((tr, d), lambda i: (i, 0)),
                  pl.BlockSpec((1, d), lambda i: (0, 0))],
        out_specs=pl.BlockSpec((tr, d), lambda i: (i, 0)),
        out_shape=jax.ShapeDtypeStruct((n, d), out_dtype),
        compiler_params=_params("parallel"),
        name="rmsnorm_rows",
    )(x, g.reshape(1, d))


def _matmul_kernel(a_ref, b_ref, o_ref):
    o_ref[...] = jnp.dot(a_ref[...], b_ref[...],
                         preferred_element_type=F32).astype(o_ref.dtype)


def _in_projection(h, w):
    m, k = h.shape
    n = w.shape[1]
    tm, tn = _tile(m, 1024), _tile(n, 1024)
    return pl.pallas_call(
        _matmul_kernel,
        grid=(m // tm, n // tn),
        in_specs=[pl.BlockSpec((tm, k), lambda i, j: (i, 0)),
                  pl.BlockSpec((k, tn), lambda i, j: (0, j))],
        out_specs=pl.BlockSpec((tm, tn), lambda i, j: (i, j)),
        out_shape=jax.ShapeDtypeStruct((m, n), BF16),
        compiler_params=_params("parallel", "parallel"),
        name="in_projection",
    )(h, w)


def _out_projection_kernel(a_ref, s_ref, wa_ref, ws_ref, x_ref, o_ref):
    acc = jnp.dot(a_ref[...], wa_ref[...], preferred_element_type=F32)
    acc += jnp.dot(s_ref[...], ws_ref[...], preferred_element_type=F32)
    o_ref[...] = x_ref[...] + acc


def _out_projection(att, sgu, w, x):
    m, ka = att.shape
    ks = sgu.shape[1]
    n = w.shape[1]
    assert ka == ks and w.shape[0] == ka + ks
    tm, tn = _tile(m, 1024), _tile(n, 512)
    return pl.pallas_call(
        _out_projection_kernel,
        grid=(m // tm, n // tn),
        in_specs=[pl.BlockSpec((tm, ka), lambda i, j: (i, 0)),
                  pl.BlockSpec((tm, ks), lambda i, j: (i, 0)),
                  pl.BlockSpec((ka, tn), lambda i, j: (0, j)),
                  pl.BlockSpec((ks, tn), lambda i, j: (1, j)),
                  pl.BlockSpec((tm, tn), lambda i, j: (i, j))],
        out_specs=pl.BlockSpec((tm, tn), lambda i, j: (i, j)),
        out_shape=jax.ShapeDtypeStruct((m, n), F32),
        compiler_params=_params("parallel", "parallel"),
        name="out_projection",
    )(att, sgu, w, w, x)


def _bias_tiles_kernel(tbl_ref, o_ref, *, tq, n_buckets):
    h = pl.program_id(0)
    max_exact = n_buckets // 2
    r = lax.broadcasted_iota(jnp.int32, (tq, 2 * tq), 0)
    c = lax.broadcasted_iota(jnp.int32, (tq, 2 * tq), 1)
    dist = r - c + tq
    d = jnp.maximum(dist, 1).astype(F32)
    large = max_exact + (jnp.log(d / max_exact) / math.log(MAX_DISTANCE / max_exact)
                         * (n_buckets - max_exact)).astype(jnp.int32)
    large = jnp.minimum(large, n_buckets - 1)
    bucket = jnp.where(dist < max_exact, dist, large)
    for m in range(2):
        val = jnp.zeros((tq, 2 * tq), F32)
        for b in range(n_buckets):
            val = jnp.where(bucket == b, tbl_ref[b, 2 * h + m], val)
        val = jnp.where(dist >= 0, val, MASK_VALUE)
        o_ref[0, m * tq:(m + 1) * tq, :] = val


def _bias_tiles(rel_bias, tq):
    n_buckets, n_maps = rel_bias.shape
    n_heads = n_maps // 2
    assert tq >= MAX_DISTANCE
    return pl.pallas_call(
        functools.partial(_bias_tiles_kernel, tq=tq, n_buckets=n_buckets),
        grid=(n_heads,),
        in_specs=[pl.BlockSpec(memory_space=pltpu.SMEM)],
        out_specs=pl.BlockSpec((1, 2 * tq, 2 * tq), lambda h: (h, 0, 0)),
        out_shape=jax.ShapeDtypeStruct((n_heads, 2 * tq, 2 * tq), F32),
        compiler_params=_params("parallel"),
        name="bias_tiles",
    )(rel_bias.astype(F32))


def _attention_kernel(far_ref, lam_ref, q_ref, k_ref, v_ref, bias_ref, g_ref, o_ref,
                      m_ref, l_ref, acc_ref, *, tq, hd):
    h = pl.program_id(1)
    i = pl.program_id(2)
    scale = hd ** -0.5

    q = q_ref[...].astype(F32) * scale
    lane = lax.broadcasted_iota(jnp.int32, q.shape, 1)
    q0 = jnp.where(lane < hd, q, 0.0).astype(BF16)
    q1 = jnp.where(lane >= hd, q, 0.0).astype(BF16)
    qq = jnp.concatenate([q0, q1], axis=0)
    row = lax.broadcasted_iota(jnp.int32, (2 * tq, 1), 0)
    far = jnp.where(row < tq, far_ref[2 * h], far_ref[2 * h + 1])

    def scores(kb):
        kblk = k_ref[pl.ds(pl.multiple_of(kb * tq, tq), tq), :]
        return lax.dot_general(qq, kblk, (((1,), (1,)), ((), ())),
                               preferred_element_type=F32)

    def weighted_values(p, kb):
        vblk = v_ref[pl.ds(pl.multiple_of(kb * tq, tq), tq), :]
        return jnp.dot(p.astype(BF16), vblk, preferred_element_type=F32)

    def online_update(s, kb):
        m_prev = m_ref[...]
        m_new = jnp.maximum(m_prev, jnp.max(s, axis=-1, keepdims=True))
        alpha = jnp.exp(m_prev - m_new)
        p = jnp.exp(s - m_new)
        l_ref[...] = alpha * l_ref[...] + jnp.sum(p, axis=-1, keepdims=True)
        acc_ref[...] = alpha * acc_ref[...] + weighted_values(p, kb)
        m_ref[...] = m_new

    s = scores(i) + bias_ref[0, :, tq:]
    m0 = jnp.max(s, axis=-1, keepdims=True)
    p = jnp.exp(s - m0)
    m_ref[...] = m0
    l_ref[...] = jnp.sum(p, axis=-1, keepdims=True)
    acc_ref[...] = weighted_values(p, i)

    kb_prev = jnp.maximum(i - 1, 0)
    no_prev = jnp.where(i >= 1, 0.0, MASK_VALUE)
    online_update(scores(kb_prev) + (bias_ref[0, :, :tq] + no_prev), kb_prev)

    def far_step(kb, carry):
        online_update(scores(kb) + far, kb)
        return carry

    lax.fori_loop(0, jnp.maximum(i - 1, 0), far_step, 0)

    o = acc_ref[...] / l_ref[...]
    a = o[:tq] - lam_ref[0] * o[tq:]
    ms = jnp.mean(a * a, axis=-1, keepdims=True)
    y = a * lax.rsqrt(ms + NORM_EPS) * g_ref[...] * (1.0 - LAMBDA_INIT)
    o_ref[...] = y.astype(o_ref.dtype)


def _diff_attention(proj, bias, far_bias, lam, subln_g, *, batch, seq, n_heads, hd, tq):
    vd = subln_g.shape[-1]
    assert 2 * hd == LANES and vd == LANES
    nq = seq // tq
    k_col0 = n_heads
    v_col0 = 2 * n_heads
    grid_spec = pltpu.PrefetchScalarGridSpec(
        num_scalar_prefetch=2,
        grid=(batch, n_heads, nq),
        in_specs=[
            pl.BlockSpec((tq, LANES), lambda b, h, i, *_: (b * nq + i, h)),
            pl.BlockSpec((seq, LANES), lambda b, h, i, *_: (b, k_col0 + h)),
            pl.BlockSpec((seq, LANES), lambda b, h, i, *_: (b, v_col0 + h)),
            pl.BlockSpec((1, 2 * tq, 2 * tq), lambda b, h, i, *_: (h, 0, 0)),
            pl.BlockSpec((1, vd), lambda b, h, i, *_: (0, 0)),
        ],
        out_specs=pl.BlockSpec((tq, vd), lambda b, h, i, *_: (b * nq + i, h)),
        scratch_shapes=[pltpu.VMEM((2 * tq, 1), F32),
                        pltpu.VMEM((2 * tq, 1), F32),
                        pltpu.VMEM((2 * tq, vd), F32)],
    )
    return pl.pallas_call(
        functools.partial(_attention_kernel, tq=tq, hd=hd),
        grid_spec=grid_spec,
        out_shape=jax.ShapeDtypeStruct((batch * seq, n_heads * vd), BF16),
        compiler_params=_params("parallel", "parallel", "arbitrary"),
        name="diff_attention",
    )(far_bias, lam, proj, proj, proj, bias, subln_g.reshape(1, vd))


def _gelu(x):
    return 0.5 * x * (1.0 + lax.erf(x * math.sqrt(0.5)))


def _sgu_kernel(u_ref, v_ref, lng_ref, lnb_ref, w_ref, bt_ref, o_ref, *, n_heads, hdim):
    u = _gelu(u_ref[...].astype(F32))
    v = _gelu(v_ref[...].astype(F32))
    mu = jnp.mean(v, axis=-1, keepdims=True)
    vc = v - mu
    var = jnp.mean(vc * vc, axis=-1, keepdims=True)
    vn = (vc * lax.rsqrt(var + NORM_EPS) * lng_ref[...] + lnb_ref[...]).astype(BF16)
    chunk = w_ref.shape[-1]
    r = lax.broadcasted_iota(jnp.int32, (chunk, chunk), 0)
    c = lax.broadcasted_iota(jnp.int32, (chunk, chunk), 1)
    causal = r >= c
    for hh in range(n_heads):
        cols = slice(hh * hdim, (hh + 1) * hdim)
        w = jnp.where(causal, w_ref[hh], 0.0).astype(BF16)
        y = jnp.dot(w, vn[:, cols], preferred_element_type=F32) + bt_ref[:, hh:hh + 1]
        o_ref[:, cols] = (u[:, cols] * y).astype(o_ref.dtype)


def _spatial_gating(proj, ln_g, ln_b, w_s, b_s, *, u_col0):
    n_tok = proj.shape[0]
    n_heads, chunk, _ = w_s.shape
    width = ln_g.shape[-1]
    hdim = width // n_heads
    assert u_col0 % width == 0
    ub = u_col0 // width
    return pl.pallas_call(
        functools.partial(_sgu_kernel, n_heads=n_heads, hdim=hdim),
        grid=(n_tok // chunk,),
        in_specs=[pl.BlockSpec((chunk, width), lambda i: (i, ub)),
                  pl.BlockSpec((chunk, width), lambda i: (i, ub + 1)),
                  pl.BlockSpec((1, width), lambda i: (0, 0)),
                  pl.BlockSpec((1, width), lambda i: (0, 0)),
                  pl.BlockSpec((n_heads, chunk, chunk), lambda i: (0, 0, 0)),
                  pl.BlockSpec((chunk, n_heads), lambda i: (0, 0))],
        out_specs=pl.BlockSpec((chunk, width), lambda i: (i, 0)),
        out_shape=jax.ShapeDtypeStruct((n_tok, width), BF16),
        compiler_params=_params("parallel"),
        name="spatial_gating",
    )(proj, proj, ln_g.reshape(1, width), ln_b.reshape(1, width), w_s, b_s.T)


def _router_kernel(x_ref, g_ref, rw_ref, rb_ref, hp_ref, idx_ref, gate_ref):
    x = x_ref[...]
    ms = jnp.mean(x * x, axis=-1, keepdims=True)
    h = x * lax.rsqrt(ms + NORM_EPS) * g_ref[...]
    half = h.shape[-1] // 2
    hi = pltpu.bitcast(h[:, :half].astype(BF16).astype(F32), jnp.uint32)
    lo = pltpu.bitcast(h[:, half:].astype(BF16).astype(F32), jnp.uint32)
    hp_ref[...] = (hi & jnp.uint32(0xFFFF0000)) | (lo >> 16)

    logits = jnp.dot(h, rw_ref[...], preferred_element_type=F32,
                     precision=lax.Precision.HIGHEST) + rb_ref[...]
    n_exp = logits.shape[-1]
    lane = lax.broadcasted_iota(jnp.int32, logits.shape, 1)
    vals, idxs = [], []
    for _ in range(TOP_K):
        top = jnp.max(logits, axis=-1, keepdims=True)
        idx = jnp.min(jnp.where(logits == top, lane, n_exp), axis=-1, keepdims=True)
        vals.append(top)
        idxs.append(idx)
        logits = jnp.where(lane == idx, -jnp.inf, logits)
    exps = [jnp.exp(v - vals[0]) for v in vals]
    denom = exps[0]
    for e in exps[1:]:
        denom = denom + e
    for k in range(TOP_K):
        idx_ref[:, k:k + 1] = idxs[k]
        gate_ref[:, k:k + 1] = exps[k] / denom


def _router(x, g, rw, rb):
    n, d = x.shape
    n_exp = rw.shape[1]
    tr = _tile(n, 256)
    return pl.pallas_call(
        _router_kernel,
        grid=(n // tr,),
        in_specs=[pl.BlockSpec((tr, d), lambda i: (i, 0)),
                  pl.BlockSpec((1, d), lambda i: (0, 0)),
                  pl.BlockSpec((d, n_exp), lambda i: (0, 0)),
                  pl.BlockSpec((1, n_exp), lambda i: (0, 0))],
        out_specs=[pl.BlockSpec((tr, d // 2), lambda i: (i, 0)),
                   pl.BlockSpec((tr, TOP_K), lambda i: (i, 0)),
                   pl.BlockSpec((tr, TOP_K), lambda i: (i, 0))],
        out_shape=[jax.ShapeDtypeStruct((n, d // 2), jnp.uint32),
                   jax.ShapeDtypeStruct((n, TOP_K), jnp.int32),
                   jax.ShapeDtypeStruct((n, TOP_K), F32)],
        compiler_params=_params("parallel"),
        name="router",
    )(x, g.reshape(1, d), rw, rb.reshape(1, n_exp))


def _dispatch_kernel(dest_ref, pends_ref, pcnt_ref, h_hbm, xs_hbm, zbuf, zsem, sem,
                     *, tb, tm, n_exp):
    s = pl.program_id(0)

    n_rows = xs_hbm.shape[0]

    def zero_block(start):
        return pltpu.make_async_copy(
            zbuf, xs_hbm.at[pl.ds(pl.multiple_of(start, tm), tm), :], zsem)

    def zero_fill(action):
        for e in range(n_exp):
            @pl.when(pcnt_ref[e] > 0)
            def _():
                action(zero_block(pends_ref[e] - tm))
        for b in range(n_exp):
            start = pends_ref[n_exp - 1] + b * tm

            @pl.when(start < n_rows)
            def _():
                action(zero_block(start))

    @pl.when(s == 0)
    def _():
        zbuf[...] = jnp.zeros_like(zbuf)
        zero_fill(lambda copy: copy.start())
        zero_fill(lambda copy: copy.wait())

    def row_copy(tok, slot):
        return pltpu.make_async_copy(h_hbm.at[pl.ds(tok, 1), :],
                                     xs_hbm.at[pl.ds(slot, 1), :], sem)

    def issue(t, carry):
        tok = s * tb + t
        for k in range(TOP_K):
            row_copy(tok, dest_ref[tok * TOP_K + k]).start()
        return carry

    lax.fori_loop(0, tb, issue, 0)

    def drain(t, carry):
        for k in range(TOP_K):
            row_copy(0, 0).wait()
        return carry

    lax.fori_loop(0, tb, drain, 0)


def _dispatch(hp, dest, pends, pcounts, *, n_rows, tm):
    n_tok, dw = hp.shape
    n_exp = pends.shape[0]
    tb = _tile(n_tok, 128)
    grid_spec = pltpu.PrefetchScalarGridSpec(
        num_scalar_prefetch=3,
        grid=(n_tok // tb,),
        in_specs=[pl.BlockSpec(memory_space=pl.ANY)],
        out_specs=pl.BlockSpec(memory_space=pl.ANY),
        scratch_shapes=[pltpu.VMEM((tm, dw), hp.dtype),
                        pltpu.SemaphoreType.DMA(()),
                        pltpu.SemaphoreType.DMA(())],
    )
    return pl.pallas_call(
        functools.partial(_dispatch_kernel, tb=tb, tm=tm, n_exp=n_exp),
        grid_spec=grid_spec,
        out_shape=jax.ShapeDtypeStruct((n_rows, dw), hp.dtype),
        compiler_params=_params("arbitrary"),
        name="dispatch",
    )(dest, pends, pcounts, hp)


def _weights_changed(be_ref, i):
    prev = be_ref[jnp.maximum(i - 1, 0)]
    return jnp.logical_or(i == 0, be_ref[i] != prev)


def _unpack_bf16_pairs(p):
    hi = pltpu.bitcast(p & jnp.uint32(0xFFFF0000), F32).astype(BF16)
    lo = pltpu.bitcast(p << 16, F32).astype(BF16)
    return hi, lo


def _expert_up_kernel(be_ref, nv_ref, xs_ref, wg_ref, wu_ref, bg_ref, bu_ref, o_ref,
                      wg_bf, wu_bf):
    i = pl.program_id(1)

    @pl.when(_weights_changed(be_ref, i))
    def _():
        wg_bf[...] = wg_ref[0].astype(BF16)
        wu_bf[...] = wu_ref[0].astype(BF16)

    @pl.when(i < nv_ref[0])
    def _():
        x_hi, x_lo = _unpack_bf16_pairs(xs_ref[...])
        half = x_hi.shape[-1]
        gate = (jnp.dot(x_hi, wg_bf[:half, :], preferred_element_type=F32)
                + jnp.dot(x_lo, wg_bf[half:, :], preferred_element_type=F32) + bg_ref[0])
        up = (jnp.dot(x_hi, wu_bf[:half, :], preferred_element_type=F32)
              + jnp.dot(x_lo, wu_bf[half:, :], preferred_element_type=F32) + bu_ref[0])
        gate = jnp.minimum(gate, SWIGLU_LIMIT)
        up = jnp.clip(up, -SWIGLU_LIMIT, SWIGLU_LIMIT)
        act = (up + 1.0) * gate * (1.0 / (1.0 + jnp.exp(-SWIGLU_ALPHA * gate)))
        o_ref[...] = act.astype(o_ref.dtype)

    @pl.when(i >= nv_ref[0])
    def _():
        o_ref[...] = jnp.zeros_like(o_ref)


def _expert_up(xs, block_e, n_valid, w_gu, b_gu, *, tm):
    n_rows, dw = xs.shape
    n_exp, d, ff2 = w_gu.shape
    ff = ff2 // 2
    assert d == 2 * dw
    tn = _tile(ff, 512)
    nj = ff // tn
    n_blocks = n_rows // tm
    b3 = b_gu.reshape(n_exp, 1, ff2)

    def row_block(j, i, be, nv):
        return (jnp.minimum(i, nv[0] - 1), 0)

    grid_spec = pltpu.PrefetchScalarGridSpec(
        num_scalar_prefetch=2,
        grid=(nj, n_blocks),
        in_specs=[
            pl.BlockSpec((tm, dw), row_block),
            pl.BlockSpec((1, d, tn), lambda j, i, be, nv: (be[i], 0, j)),
            pl.BlockSpec((1, d, tn), lambda j, i, be, nv: (be[i], 0, nj + j)),
            pl.BlockSpec((1, 1, tn), lambda j, i, be, nv: (be[i], 0, j)),
            pl.BlockSpec((1, 1, tn), lambda j, i, be, nv: (be[i], 0, nj + j)),
        ],
        out_specs=pl.BlockSpec((tm, tn), lambda j, i, be, nv: (i, j)),
        scratch_shapes=[pltpu.VMEM((d, tn), BF16), pltpu.VMEM((d, tn), BF16)],
    )
    return pl.pallas_call(
        _expert_up_kernel,
        grid_spec=grid_spec,
        out_shape=jax.ShapeDtypeStruct((n_rows, ff), BF16),
        compiler_params=_params("arbitrary", "arbitrary"),
        name="expert_up",
    )(block_e, n_valid, xs, w_gu, w_gu, b3, b3)


def _expert_down_kernel(be_ref, nv_ref, h_ref, w_ref, b_ref, o_ref, w_bf):
    i = pl.program_id(1)

    @pl.when(_weights_changed(be_ref, i))
    def _():
        w_bf[...] = w_ref[0].astype(BF16)

    @pl.when(i < nv_ref[0])
    def _():
        o_ref[...] = jnp.dot(h_ref[...], w_bf[...], preferred_element_type=F32) + b_ref[0]

    @pl.when(i >= nv_ref[0])
    def _():
        o_ref[...] = jnp.zeros_like(o_ref)


def _expert_down(hs, block_e, n_valid, w_dn, b_dn, *, tm):
    n_rows, ff = hs.shape
    n_exp, _, d = w_dn.shape
    tn = _tile(d, 1024)
    nj = d // tn
    n_blocks = n_rows // tm
    b3 = b_dn.reshape(n_exp, 1, d)
    grid_spec = pltpu.PrefetchScalarGridSpec(
        num_scalar_prefetch=2,
        grid=(nj, n_blocks),
        in_specs=[
            pl.BlockSpec((tm, ff), lambda j, i, be, nv: (jnp.minimum(i, nv[0] - 1), 0)),
            pl.BlockSpec((1, ff, tn), lambda j, i, be, nv: (be[i], 0, j)),
            pl.BlockSpec((1, 1, tn), lambda j, i, be, nv: (be[i], 0, j)),
        ],
        out_specs=pl.BlockSpec((tm, tn), lambda j, i, be, nv: (i, j)),
        scratch_shapes=[pltpu.VMEM((ff, tn), BF16)],
    )
    return pl.pallas_call(
        _expert_down_kernel,
        grid_spec=grid_spec,
        out_shape=jax.ShapeDtypeStruct((n_rows, d), F32),
        compiler_params=_params("arbitrary", "arbitrary"),
        name="expert_down",
    )(block_e, n_valid, hs, w_dn, b3)


def _combine_kernel(dest_ref, x_ref, gate_ref, g_ref, ys_hbm, o_ref, buf, sem, *, tb):
    s = pl.program_id(0)

    def row_copy(slot, k, t):
        return pltpu.make_async_copy(ys_hbm.at[pl.ds(slot, 1), :],
                                     buf.at[k, pl.ds(t, 1), :], sem)

    def issue(t, carry):
        tok = s * tb + t
        for k in range(TOP_K):
            row_copy(dest_ref[tok * TOP_K + k], k, t).start()
        return carry

    lax.fori_loop(0, tb, issue, 0)

    def drain(t, carry):
        for k in range(TOP_K):
            row_copy(0, k, t).wait()
        return carry

    lax.fori_loop(0, tb, drain, 0)

    acc = x_ref[...]
    for k in range(TOP_K):
        acc = acc + gate_ref[:, k:k + 1] * buf[k]
    ms = jnp.mean(acc * acc, axis=-1, keepdims=True)
    o_ref[...] = acc * lax.rsqrt(ms + NORM_EPS) * g_ref[...]


def _combine(x, gates, dest, ys, g):
    n_tok, d = x.shape
    tb = _tile(n_tok, 128)
    grid_spec = pltpu.PrefetchScalarGridSpec(
        num_scalar_prefetch=1,
        grid=(n_tok // tb,),
        in_specs=[pl.BlockSpec((tb, d), lambda i, dest: (i, 0)),
                  pl.BlockSpec((tb, TOP_K), lambda i, dest: (i, 0)),
                  pl.BlockSpec((1, d), lambda i, dest: (0, 0)),
                  pl.BlockSpec(memory_space=pl.ANY)],
        out_specs=pl.BlockSpec((tb, d), lambda i, dest: (i, 0)),
        scratch_shapes=[pltpu.VMEM((TOP_K, tb, d), F32),
                        pltpu.SemaphoreType.DMA(())],
    )
    return pl.pallas_call(
        functools.partial(_combine_kernel, tb=tb),
        grid_spec=grid_spec,
        out_shape=jax.ShapeDtypeStruct((n_tok, d), F32),
        compiler_params=_params("arbitrary"),
        name="combine",
    )(dest, x, gates, g.reshape(1, d), ys)


def _routing_tables(top_idx, n_exp, tm):
    flat_e = top_idx.reshape(-1)
    n_pairs = flat_e.shape[0]
    onehot = (flat_e[:, None] == jnp.arange(n_exp, dtype=jnp.int32)[None, :]).astype(jnp.int32)
    csum = jnp.cumsum(onehot, axis=0)
    counts = csum[-1]
    rank = jnp.sum(onehot * csum, axis=1) - 1
    pcounts = (counts + tm - 1) // tm * tm
    pends = jnp.cumsum(pcounts)
    pstarts = pends - pcounts
    dest = jnp.sum(onehot * pstarts[None, :], axis=1) + rank
    n_blocks = n_pairs // tm + n_exp
    block_start = jnp.arange(n_blocks, dtype=jnp.int32) * tm
    block_e = jnp.minimum(jnp.searchsorted(pends, block_start, side='right'),
                          n_exp - 1).astype(jnp.int32)
    n_valid = (pends[-1:] // tm).astype(jnp.int32)
    return (dest.astype(jnp.int32), pends.astype(jnp.int32), pcounts.astype(jnp.int32),
            block_e, n_valid, n_blocks * tm)


def kernel(x, attn_norm_g, w_in, lambda_q1, lambda_k1, lambda_q2, lambda_k2, diff_subln_g,
           sgu_ln_g, sgu_ln_b, sgu_w, sgu_b, rel_bias, w_out, ffn_norm_g, router_w, router_b,
           w_gate_up, b_gate_up, w_down, b_down, final_norm_g):
    batch, seq, d = x.shape
    n_tok = batch * seq
    hd = lambda_q1.shape[-1]
    n_maps = rel_bias.shape[1]
    n_heads = n_maps // 2
    q_cols = n_maps * hd
    attn_width = n_heads * diff_subln_g.shape[-1]
    n_exp = w_gate_up.shape[1]
    tq = _tile(seq, 256)
    tm = 256

    xt = x.reshape(n_tok, d)
    h = _rmsnorm_rows(xt, attn_norm_g[0], BF16)
    proj = _in_projection(h, w_in[0].astype(BF16))

    lam = (jnp.exp(jnp.sum(lambda_q1[0].astype(F32) * lambda_k1[0].astype(F32)))
           - jnp.exp(jnp.sum(lambda_q2[0].astype(F32) * lambda_k2[0].astype(F32)))
           + LAMBDA_INIT).reshape(1)
    bias = _bias_tiles(rel_bias, tq)
    far_bias = rel_bias[-1].astype(F32)
    att = _diff_attention(proj, bias, far_bias, lam, diff_subln_g[0],
                          batch=batch, seq=seq, n_heads=n_heads, hd=hd, tq=tq)
    sgu = _spatial_gating(proj, sgu_ln_g[0], sgu_ln_b[0], sgu_w[0], sgu_b[0],
                          u_col0=2 * q_cols + attn_width)
    x1 = _out_projection(att, sgu, w_out[0].astype(BF16), xt)

    hp, top_idx, gates = _router(x1, ffn_norm_g[0], router_w[0], router_b[0])
    dest, pends, pcounts, block_e, n_valid, n_rows = _routing_tables(top_idx, n_exp, tm)
    xs = _dispatch(hp, dest, pends, pcounts, n_rows=n_rows, tm=tm)
    hs = _expert_up(xs, block_e, n_valid, w_gate_up[0], b_gate_up[0], tm=tm)
    ys = _expert_down(hs, block_e, n_valid, w_down[0], b_down[0], tm=tm)
    out = _combine(x1, gates, dest, ys, final_norm_g)
    return out.reshape(batch, seq, d)
```

```python
import functools
import math

import jax
import jax.numpy as jnp
from jax import lax
from jax.experimental import pallas as pl
from jax.experimental.pallas import tpu as pltpu

F32 = jnp.float32
BF16 = jnp.bfloat16

TOP_K = 4
MAX_DISTANCE = 128
SWIGLU_LIMIT = 7.0
SWIGLU_ALPHA = 1.702
NORM_EPS = 1e-5
LAMBDA_INIT = 0.8 - 0.6 * math.exp(0.0)
MASK_VALUE = -1e30

LANES = 128
VMEM_LIMIT_BYTES = 56 * 1024 * 1024


def _tile(dim, want):
    t = min(dim, want)
    while dim % t:
        t -= LANES
    assert t > 0, (dim, want)
    return t


def _params(*semantics):
    return pltpu.CompilerParams(dimension_semantics=semantics,
                                vmem_limit_bytes=VMEM_LIMIT_BYTES)


def _rmsnorm_kernel(x_ref, g_ref, o_ref):
    x = x_ref[...]
    ms = jnp.mean(x * x, axis=-1, keepdims=True)
    o_ref[...] = (x * lax.rsqrt(ms + NORM_EPS) * g_ref[...]).astype(o_ref.dtype)


def _rmsnorm_rows(x, g, out_dtype):
    n, d = x.shape
    tr = _tile(n, 256)
    return pl.pallas_call(
        _rmsnorm_kernel,
        grid=(n // tr,),
        in_specs=[pl.BlockSpec((tr, d), lambda i: (i, 0)),
                  pl.BlockSpec((1, d), lambda i: (0, 0))],
        out_specs=pl.BlockSpec((tr, d), lambda i: (i, 0)),
        out_shape=jax.ShapeDtypeStruct((n, d), out_dtype),
        compiler_params=_params("parallel"),
        name="rmsnorm_rows",
    )(x, g.reshape(1, d))


def _matmul_kernel(a_ref, b_ref, o_ref):
    o_ref[...] = jnp.dot(a_ref[...], b_ref[...],
                         preferred_element_type=F32).astype(o_ref.dtype)


def _in_projection(h, w):
    m, k = h.shape
    n = w.shape[1]
    tm, tn = _tile(m, 1024), _tile(n, 1024)
    return pl.pallas_call(
        _matmul_kernel,
        grid=(m // tm, n // tn),
        in_specs=[pl.BlockSpec((tm, k), lambda i, j: (i, 0)),
                  pl.BlockSpec((k, tn), lambda i, j: (0, j))],
        out_specs=pl.BlockSpec((tm, tn), lambda i, j: (i, j)),
        out_shape=jax.ShapeDtypeStruct((m, n), BF16),
        compiler_params=_params("parallel", "parallel"),
        name="in_projection",
    )(h, w)


def _out_projection_kernel(a_ref, s_ref, wa_ref, ws_ref, x_ref, o_ref):
    acc = jnp.dot(a_ref[...], wa_ref[...], preferred_element_type=F32)
    acc += jnp.dot(s_ref[...], ws_ref[...], preferred_element_type=F32)
    o_ref[...] = x_ref[...] + acc


def _out_projection(att, sgu, w, x):
    m, ka = att.shape
    ks = sgu.shape[1]
    n = w.shape[1]
    assert ka == ks and w.shape[0] == ka + ks
    tm, tn = _tile(m, 1024), _tile(n, 512)
    return pl.pallas_call(
        _out_projection_kernel,
        grid=(m // tm, n // tn),
        in_specs=[pl.BlockSpec((tm, ka), lambda i, j: (i, 0)),
                  pl.BlockSpec((tm, ks), lambda i, j: (i, 0)),
                  pl.BlockSpec((ka, tn), lambda i, j: (0, j)),
                  pl.BlockSpec((ks, tn), lambda i, j: (1, j)),
                  pl.BlockSpec((tm, tn), lambda i, j: (i, j))],
        out_specs=pl.BlockSpec((tm, tn), lambda i, j: (i, j)),
        out_shape=jax.ShapeDtypeStruct((m, n), F32),
        compiler_params=_params("parallel", "parallel"),
        name="out_projection",
    )(att, sgu, w, w, x)


def _bias_tiles_kernel(tbl_ref, o_ref, *, tq, n_buckets):
    h = pl.program_id(0)
    max_exact = n_buckets // 2
    key = lax.broadcasted_iota(jnp.int32, (tq, tq), 0)
    qry = lax.broadcasted_iota(jnp.int32, (tq, tq), 1)
    for w in range(2):
        dist = qry - key + w * tq
        d = jnp.maximum(dist, 1).astype(F32)
        large = max_exact + (jnp.log(d / max_exact) / math.log(MAX_DISTANCE / max_exact)
                             * (n_buckets - max_exact)).astype(jnp.int32)
        large = jnp.minimum(large, n_buckets - 1)
        bucket = jnp.where(dist < max_exact, dist, large)
        for m in range(2):
            val = jnp.zeros((tq, tq), F32)
            for b in range(n_buckets):
                val = jnp.where(bucket == b, tbl_ref[b, 2 * h + m], val)
            val = jnp.where(dist >= 0, val, MASK_VALUE)
            o_ref[0, w, :, m * tq:(m + 1) * tq] = val


def _bias_tiles(rel_bias, tq):
    n_buckets, n_maps = rel_bias.shape
    n_heads = n_maps // 2
    assert tq >= MAX_DISTANCE
    return pl.pallas_call(
        functools.partial(_bias_tiles_kernel, tq=tq, n_buckets=n_buckets),
        grid=(n_heads,),
        in_specs=[pl.BlockSpec(memory_space=pltpu.SMEM)],
        out_specs=pl.BlockSpec((1, 2, tq, 2 * tq), lambda h: (h, 0, 0, 0)),
        out_shape=jax.ShapeDtypeStruct((n_heads, 2, tq, 2 * tq), F32),
        compiler_params=_params("parallel"),
        name="bias_tiles",
    )(rel_bias.astype(F32))


def _attention_kernel(far_ref, lam_ref, q_ref, k_ref, v_ref, bias_ref, g_ref, o_ref,
                      vt_ref, sa_ref, sb_ref, acc_ref, *, tq, tk, hd):
    h = pl.program_id(1)
    i = pl.program_id(2)
    seq = k_ref.shape[0]
    sub = tq // tk
    assert sub == 2

    @pl.when(i == 0)
    def _():
        for c in range(seq // tk):
            vt_ref[c] = v_ref[c * tk:(c + 1) * tk, :].astype(F32).T.astype(BF16)

    qt = (q_ref[...].astype(F32) * hd ** -0.5).T
    dim = lax.broadcasted_iota(jnp.int32, qt.shape, 0)
    qqt = jnp.concatenate([jnp.where(dim < hd, qt, 0.0),
                           jnp.where(dim >= hd, qt, 0.0)], axis=1).astype(BF16)
    col = lax.broadcasted_iota(jnp.int32, (1, 2 * tq), 1)
    far = jnp.where(col < tq, far_ref[2 * h], far_ref[2 * h + 1])

    def scores(kb):
        kblk = k_ref[pl.ds(pl.multiple_of(kb * tk, tk), tk), :]
        return jnp.dot(kblk, qqt, preferred_element_type=F32)

    def weighted_values(p, kb):
        return jnp.dot(vt_ref[kb], p.astype(BF16), preferred_element_type=F32)

    kb_prev = jnp.maximum(i - 1, 0)
    no_prev = jnp.where(i >= 1, 0.0, MASK_VALUE)
    near = []
    for j in range(sub):
        rows = slice(j * tk, (j + 1) * tk)
        near.append((sub * i + j, scores(sub * i + j) + bias_ref[0, 0, rows, :]))
        near.append((sub * kb_prev + j,
                     scores(sub * kb_prev + j) + (bias_ref[0, 1, rows, :] + no_prev)))
    m = near[0][1].max(axis=0, keepdims=True)
    for _, s in near[1:]:
        m = jnp.maximum(m, jnp.max(s, axis=0, keepdims=True))
    l = jnp.zeros_like(m)
    acc = jnp.zeros(acc_ref.shape, F32)
    for kb, s in near:
        p = jnp.exp(s - m)
        l = l + jnp.sum(p, axis=0, keepdims=True)
        acc = acc + weighted_values(p, kb)
    acc_ref[...] = acc

    def online_update(s, kb, m_prev, l_prev):
        m_new = jnp.maximum(m_prev, jnp.max(s, axis=0, keepdims=True) + far)
        alpha = jnp.exp(m_prev - m_new)
        p = jnp.exp(s - (m_new - far))
        acc_ref[...] = alpha * acc_ref[...] + weighted_values(p, kb)
        return m_new, alpha * l_prev + jnp.sum(p, axis=0, keepdims=True)

    n_pairs = jnp.maximum(i - 1, 0)
    last = jnp.maximum(sub * n_pairs - 1, 0)

    def far_pair(n, carry):
        m_run, l_run = carry
        kb = sub * n
        sb_ref[...] = scores(kb + 1)
        m_run, l_run = online_update(sa_ref[...], kb, m_run, l_run)
        sa_ref[...] = scores(jnp.minimum(kb + 2, last))
        m_run, l_run = online_update(sb_ref[...], kb + 1, m_run, l_run)
        return m_run, l_run

    sa_ref[...] = scores(0)
    m, l = lax.fori_loop(0, n_pairs, far_pair, (m, l))

    o = acc_ref[...] * (1.0 / l)
    a = o[:, :tq] - lam_ref[0] * o[:, tq:]
    ms = jnp.mean(a * a, axis=0, keepdims=True)
    y = a * lax.rsqrt(ms + NORM_EPS) * g_ref[...] * (1.0 - LAMBDA_INIT)
    o_ref[...] = y.T.astype(o_ref.dtype)


def _diff_attention(proj, bias, far_bias, lam, subln_g, *, batch, seq, n_heads, hd, tq, tk):
    vd = subln_g.shape[-1]
    assert 2 * hd == LANES and vd == LANES
    nq = seq // tq
    k_col0 = n_heads
    v_col0 = 2 * n_heads
    grid_spec = pltpu.PrefetchScalarGridSpec(
        num_scalar_prefetch=2,
        grid=(batch, n_heads, nq),
        in_specs=[
            pl.BlockSpec((tq, LANES), lambda b, h, i, *_: (b * nq + i, h)),
            pl.BlockSpec((seq, LANES), lambda b, h, i, *_: (b, k_col0 + h)),
            pl.BlockSpec((seq, LANES), lambda b, h, i, *_: (b, v_col0 + h)),
            pl.BlockSpec((1, 2, tq, 2 * tq), lambda b, h, i, *_: (h, 0, 0, 0)),
            pl.BlockSpec((vd, 1), lambda b, h, i, *_: (0, 0)),
        ],
        out_specs=pl.BlockSpec((tq, vd), lambda b, h, i, *_: (b * nq + i, h)),
        scratch_shapes=[pltpu.VMEM((seq // tk, vd, tk), BF16),
                        pltpu.VMEM((tk, 2 * tq), F32),
                        pltpu.VMEM((tk, 2 * tq), F32),
                        pltpu.VMEM((vd, 2 * tq), F32)],
    )
    return pl.pallas_call(
        functools.partial(_attention_kernel, tq=tq, tk=tk, hd=hd),
        grid_spec=grid_spec,
        out_shape=jax.ShapeDtypeStruct((batch * seq, n_heads * vd), BF16),
        compiler_params=_params("parallel", "parallel", "arbitrary"),
        name="diff_attention",
    )(far_bias, lam, proj, proj, proj, bias, subln_g.reshape(vd, 1))


def _gelu(x):
    return 0.5 * x * (1.0 + lax.erf(x * math.sqrt(0.5)))


def _sgu_kernel(u_ref, v_ref, lng_ref, lnb_ref, w_ref, bt_ref, o_ref, *, n_heads, hdim):
    u = _gelu(u_ref[...].astype(F32))
    v = _gelu(v_ref[...].astype(F32))
    mu = jnp.mean(v, axis=-1, keepdims=True)
    vc = v - mu
    var = jnp.mean(vc * vc, axis=-1, keepdims=True)
    vn = (vc * lax.rsqrt(var + NORM_EPS) * lng_ref[...] + lnb_ref[...]).astype(BF16)
    chunk = w_ref.shape[-1]
    r = lax.broadcasted_iota(jnp.int32, (chunk, chunk), 0)
    c = lax.broadcasted_iota(jnp.int32, (chunk, chunk), 1)
    causal = r >= c
    for hh in range(n_heads):
        cols = slice(hh * hdim, (hh + 1) * hdim)
        w = jnp.where(causal, w_ref[hh], 0.0).astype(BF16)
        y = jnp.dot(w, vn[:, cols], preferred_element_type=F32) + bt_ref[:, hh:hh + 1]
        o_ref[:, cols] = (u[:, cols] * y).astype(o_ref.dtype)


def _spatial_gating(proj, ln_g, ln_b, w_s, b_s, *, u_col0):
    n_tok = proj.shape[0]
    n_heads, chunk, _ = w_s.shape
    width = ln_g.shape[-1]
    hdim = width // n_heads
    assert u_col0 % width == 0
    ub = u_col0 // width
    return pl.pallas_call(
        functools.partial(_sgu_kernel, n_heads=n_heads, hdim=hdim),
        grid=(n_tok // chunk,),
        in_specs=[pl.BlockSpec((chunk, width), lambda i: (i, ub)),
                  pl.BlockSpec((chunk, width), lambda i: (i, ub + 1)),
                  pl.BlockSpec((1, width), lambda i: (0, 0)),
                  pl.BlockSpec((1, width), lambda i: (0, 0)),
                  pl.BlockSpec((n_heads, chunk, chunk), lambda i: (0, 0, 0)),
                  pl.BlockSpec((chunk, n_heads), lambda i: (0, 0))],
        out_specs=pl.BlockSpec((chunk, width), lambda i: (i, 0)),
        out_shape=jax.ShapeDtypeStruct((n_tok, width), BF16),
        compiler_params=_params("parallel"),
        name="spatial_gating",
    )(proj, proj, ln_g.reshape(1, width), ln_b.reshape(1, width), w_s, b_s.T)


def _router_kernel(x_ref, g_ref, rw_ref, rb_ref, hp_ref, idx_ref, gate_ref):
    x = x_ref[...]
    ms = jnp.mean(x * x, axis=-1, keepdims=True)
    h = x * lax.rsqrt(ms + NORM_EPS) * g_ref[...]
    half = h.shape[-1] // 2
    hi = pltpu.bitcast(h[:, :half].astype(BF16).astype(F32), jnp.uint32)
    lo = pltpu.bitcast(h[:, half:].astype(BF16).astype(F32), jnp.uint32)
    hp_ref[...] = (hi & jnp.uint32(0xFFFF0000)) | (lo >> 16)

    logits = jnp.dot(h, rw_ref[...], preferred_element_type=F32,
                     precision=lax.Precision.HIGHEST) + rb_ref[...]
    n_exp = logits.shape[-1]
    lane = lax.broadcasted_iota(jnp.int32, logits.shape, 1)
    vals, idxs = [], []
    for _ in range(TOP_K):
        top = jnp.max(logits, axis=-1, keepdims=True)
        idx = jnp.min(jnp.where(logits == top, lane, n_exp), axis=-1, keepdims=True)
        vals.append(top)
        idxs.append(idx)
        logits = jnp.where(lane == idx, -jnp.inf, logits)
    exps = [jnp.exp(v - vals[0]) for v in vals]
    denom = exps[0]
    for e in exps[1:]:
        denom = denom + e
    for k in range(TOP_K):
        idx_ref[:, k:k + 1] = idxs[k]
        gate_ref[:, k:k + 1] = exps[k] / denom


def _router(x, g, rw, rb):
    n, d = x.shape
    n_exp = rw.shape[1]
    tr = _tile(n, 256)
    return pl.pallas_call(
        _router_kernel,
        grid=(n // tr,),
        in_specs=[pl.BlockSpec((tr, d), lambda i: (i, 0)),
                  pl.BlockSpec((1, d), lambda i: (0, 0)),
                  pl.BlockSpec((d, n_exp), lambda i: (0, 0)),
                  pl.BlockSpec((1, n_exp), lambda i: (0, 0))],
        out_specs=[pl.BlockSpec((tr, d // 2), lambda i: (i, 0)),
                   pl.BlockSpec((tr, TOP_K), lambda i: (i, 0)),
                   pl.BlockSpec((tr, TOP_K), lambda i: (i, 0))],
        out_shape=[jax.ShapeDtypeStruct((n, d // 2), jnp.uint32),
                   jax.ShapeDtypeStruct((n, TOP_K), jnp.int32),
                   jax.ShapeDtypeStruct((n, TOP_K), F32)],
        compiler_params=_params("parallel"),
        name="router",
    )(x, g.reshape(1, d), rw, rb.reshape(1, n_exp))


def _dispatch_kernel(dest_ref, pends_ref, pcnt_ref, h_ref, xs_hbm, zbuf, zsem, sem,
                     *, tb, tm, n_exp):
    s = pl.program_id(0)

    n_rows = xs_hbm.shape[0]

    def zero_block(start):
        return pltpu.make_async_copy(
            zbuf, xs_hbm.at[pl.ds(pl.multiple_of(start, tm), tm), :], zsem)

    def zero_fill(action):
        for e in range(n_exp):
            @pl.when(pcnt_ref[e] > 0)
            def _():
                action(zero_block(pends_ref[e] - tm))
        for b in range(n_exp):
            start = pends_ref[n_exp - 1] + b * tm

            @pl.when(start < n_rows)
            def _():
                action(zero_block(start))

    @pl.when(s == 0)
    def _():
        zbuf[...] = jnp.zeros_like(zbuf)
        zero_fill(lambda copy: copy.start())
        zero_fill(lambda copy: copy.wait())

    def row_copy(t, slot):
        return pltpu.make_async_copy(h_ref.at[pl.ds(t, 1), :],
                                     xs_hbm.at[pl.ds(slot, 1), :], sem)

    def issue(t, carry):
        tok = s * tb + t
        for k in range(TOP_K):
            row_copy(t, dest_ref[tok * TOP_K + k]).start()
        return carry

    lax.fori_loop(0, tb, issue, 0)

    def drain(t, carry):
        for k in range(TOP_K):
            row_copy(0, 0).wait()
        return carry

    lax.fori_loop(0, tb, drain, 0)


def _dispatch(hp, dest, pends, pcounts, *, n_rows, tm):
    n_tok, dw = hp.shape
    n_exp = pends.shape[0]
    tb = _tile(n_tok, 128)
    grid_spec = pltpu.PrefetchScalarGridSpec(
        num_scalar_prefetch=3,
        grid=(n_tok // tb,),
        in_specs=[pl.BlockSpec((tb, dw), lambda i, *_: (i, 0))],
        out_specs=pl.BlockSpec(memory_space=pl.ANY),
        scratch_shapes=[pltpu.VMEM((tm, dw), hp.dtype),
                        pltpu.SemaphoreType.DMA(()),
                        pltpu.SemaphoreType.DMA(())],
    )
    return pl.pallas_call(
        functools.partial(_dispatch_kernel, tb=tb, tm=tm, n_exp=n_exp),
        grid_spec=grid_spec,
        out_shape=jax.ShapeDtypeStruct((n_rows, dw), hp.dtype),
        compiler_params=_params("arbitrary"),
        name="dispatch",
    )(dest, pends, pcounts, hp)


def _weights_changed(be_ref, i):
    prev = be_ref[jnp.maximum(i - 1, 0)]
    return jnp.logical_or(i == 0, be_ref[i] != prev)


def _unpack_bf16_pairs(p):
    hi = pltpu.bitcast(p & jnp.uint32(0xFFFF0000), F32).astype(BF16)
    lo = pltpu.bitcast(p << 16, F32).astype(BF16)
    return hi, lo


def _expert_up_kernel(be_ref, nv_ref, xs_ref, wg_ref, wu_ref, bg_ref, bu_ref, o_ref,
                      wg_bf, wu_bf):
    i = pl.program_id(1)

    @pl.when(_weights_changed(be_ref, i))
    def _():
        wg_bf[...] = wg_ref[0].astype(BF16)
        wu_bf[...] = wu_ref[0].astype(BF16)

    @pl.when(i < nv_ref[0])
    def _():
        x_hi, x_lo = _unpack_bf16_pairs(xs_ref[...])
        half = x_hi.shape[-1]
        gate = (jnp.dot(x_hi, wg_bf[:half, :], preferred_element_type=F32)
                + jnp.dot(x_lo, wg_bf[half:, :], preferred_element_type=F32) + bg_ref[0])
        up = (jnp.dot(x_hi, wu_bf[:half, :], preferred_element_type=F32)
              + jnp.dot(x_lo, wu_bf[half:, :], preferred_element_type=F32) + bu_ref[0])
        gate = jnp.minimum(gate, SWIGLU_LIMIT)
        up = jnp.clip(up, -SWIGLU_LIMIT, SWIGLU_LIMIT)
        act = (up + 1.0) * gate * (1.0 / (1.0 + jnp.exp(-SWIGLU_ALPHA * gate)))
        o_ref[...] = act.astype(o_ref.dtype)

    @pl.when(i >= nv_ref[0])
    def _():
        o_ref[...] = jnp.zeros_like(o_ref)


def _expert_up(xs, block_e, n_valid, w_gu, b_gu, *, tm):
    n_rows, dw = xs.shape
    n_exp, d, ff2 = w_gu.shape
    ff = ff2 // 2
    assert d == 2 * dw
    tn = _tile(ff, 512)
    nj = ff // tn
    n_blocks = n_rows // tm
    b3 = b_gu.reshape(n_exp, 1, ff2)

    def row_block(j, i, be, nv):
        return (jnp.minimum(i, nv[0] - 1), 0)

    grid_spec = pltpu.PrefetchScalarGridSpec(
        num_scalar_prefetch=2,
        grid=(nj, n_blocks),
        in_specs=[
            pl.BlockSpec((tm, dw), row_block),
            pl.BlockSpec((1, d, tn), lambda j, i, be, nv: (be[i], 0, j)),
            pl.BlockSpec((1, d, tn), lambda j, i, be, nv: (be[i], 0, nj + j)),
            pl.BlockSpec((1, 1, tn), lambda j, i, be, nv: (be[i], 0, j)),
            pl.BlockSpec((1, 1, tn), lambda j, i, be, nv: (be[i], 0, nj + j)),
        ],
        out_specs=pl.BlockSpec((tm, tn), lambda j, i, be, nv: (i, j)),
        scratch_shapes=[pltpu.VMEM((d, tn), BF16), pltpu.VMEM((d, tn), BF16)],
    )
    return pl.pallas_call(
        _expert_up_kernel,
        grid_spec=grid_spec,
        out_shape=jax.ShapeDtypeStruct((n_rows, ff), BF16),
        compiler_params=_params("arbitrary", "arbitrary"),
        name="expert_up",
    )(block_e, n_valid, xs, w_gu, w_gu, b3, b3)


def _expert_down_kernel(be_ref, nv_ref, h_ref, w_ref, b_ref, o_ref, w_bf):
    i = pl.program_id(1)

    @pl.when(_weights_changed(be_ref, i))
    def _():
        w_bf[...] = w_ref[0].astype(BF16)

    @pl.when(i < nv_ref[0])
    def _():
        o_ref[...] = jnp.dot(h_ref[...], w_bf[...], preferred_element_type=F32) + b_ref[0]

    @pl.when(i >= nv_ref[0])
    def _():
        o_ref[...] = jnp.zeros_like(o_ref)


def _expert_down(hs, block_e, n_valid, w_dn, b_dn, *, tm):
    n_rows, ff = hs.shape
    n_exp, _, d = w_dn.shape
    tn = _tile(d, 1024)
    nj = d // tn
    n_blocks = n_rows // tm
    b3 = b_dn.reshape(n_exp, 1, d)
    grid_spec = pltpu.PrefetchScalarGridSpec(
        num_scalar_prefetch=2,
        grid=(nj, n_blocks),
        in_specs=[
            pl.BlockSpec((tm, ff), lambda j, i, be, nv: (jnp.minimum(i, nv[0] - 1), 0)),
            pl.BlockSpec((1, ff, tn), lambda j, i, be, nv: (be[i], 0, j)),
            pl.BlockSpec((1, 1, tn), lambda j, i, be, nv: (be[i], 0, j)),
        ],
        out_specs=pl.BlockSpec((tm, tn), lambda j, i, be, nv: (i, j)),
        scratch_shapes=[pltpu.VMEM((ff, tn), BF16)],
    )
    return pl.pallas_call(
        _expert_down_kernel,
        grid_spec=grid_spec,
        out_shape=jax.ShapeDtypeStruct((n_rows, d), F32),
        compiler_params=_params("arbitrary", "arbitrary"),
        name="expert_down",
    )(block_e, n_valid, hs, w_dn, b3)


def _combine_kernel(dest_ref, x_ref, gate_ref, g_ref, ys_hbm, o_ref, buf, sem, *, tb):
    s = pl.program_id(0)

    def row_copy(slot, k, t):
        return pltpu.make_async_copy(ys_hbm.at[pl.ds(slot, 1), :],
                                     buf.at[k, pl.ds(t, 1), :], sem)

    def issue(t, carry):
        tok = s * tb + t
        for k in range(TOP_K):
            row_copy(dest_ref[tok * TOP_K + k], k, t).start()
        return carry

    lax.fori_loop(0, tb, issue, 0)

    def drain(t, carry):
        for k in range(TOP_K):
            row_copy(0, k, t).wait()
        return carry

    lax.fori_loop(0, tb, drain, 0)

    acc = x_ref[...]
    for k in range(TOP_K):
        acc = acc + gate_ref[:, k:k + 1] * buf[k]
    ms = jnp.mean(acc * acc, axis=-1, keepdims=True)
    o_ref[...] = acc * lax.rsqrt(ms + NORM_EPS) * g_ref[...]


def _combine(x, gates, dest, ys, g):
    n_tok, d = x.shape
    tb = _tile(n_tok, 128)
    grid_spec = pltpu.PrefetchScalarGridSpec(
        num_scalar_prefetch=1,
        grid=(n_tok // tb,),
        in_specs=[pl.BlockSpec((tb, d), lambda i, dest: (i, 0)),
                  pl.BlockSpec((tb, TOP_K), lambda i, dest: (i, 0)),
                  pl.BlockSpec((1, d), lambda i, dest: (0, 0)),
                  pl.BlockSpec(memory_space=pl.ANY)],
        out_specs=pl.BlockSpec((tb, d), lambda i, dest: (i, 0)),
        scratch_shapes=[pltpu.VMEM((TOP_K, tb, d), F32),
                        pltpu.SemaphoreType.DMA(())],
    )
    return pl.pallas_call(
        functools.partial(_combine_kernel, tb=tb),
        grid_spec=grid_spec,
        out_shape=jax.ShapeDtypeStruct((n_tok, d), F32),
        compiler_params=_params("arbitrary"),
        name="combine",
    )(dest, x, gates, g.reshape(1, d), ys)


def _routing_tables(top_idx, n_exp, tm):
    flat_e = top_idx.reshape(-1)
    n_pairs = flat_e.shape[0]
    onehot = (flat_e[:, None] == jnp.arange(n_exp, dtype=jnp.int32)[None, :]).astype(jnp.int32)
    csum = jnp.cumsum(onehot, axis=0)
    counts = csum[-1]
    rank = jnp.sum(onehot * csum, axis=1) - 1
    pcounts = (counts + tm - 1) // tm * tm
    pends = jnp.cumsum(pcounts)
    pstarts = pends - pcounts
    dest = jnp.sum(onehot * pstarts[None, :], axis=1) + rank
    n_blocks = n_pairs // tm + n_exp
    block_start = jnp.arange(n_blocks, dtype=jnp.int32) * tm
    block_e = jnp.minimum(jnp.searchsorted(pends, block_start, side='right'),
                          n_exp - 1).astype(jnp.int32)
    n_valid = (pends[-1:] // tm).astype(jnp.int32)
    return (dest.astype(jnp.int32), pends.astype(jnp.int32), pcounts.astype(jnp.int32),
            block_e, n_valid, n_blocks * tm)


def kernel(x, attn_norm_g, w_in, lambda_q1, lambda_k1, lambda_q2, lambda_k2, diff_subln_g,
           sgu_ln_g, sgu_ln_b, sgu_w, sgu_b, rel_bias, w_out, ffn_norm_g, router_w, router_b,
           w_gate_up, b_gate_up, w_down, b_down, final_norm_g):
    batch, seq, d = x.shape
    n_tok = batch * seq
    hd = lambda_q1.shape[-1]
    n_maps = rel_bias.shape[1]
    n_heads = n_maps // 2
    q_cols = n_maps * hd
    attn_width = n_heads * diff_subln_g.shape[-1]
    n_exp = w_gate_up.shape[1]
    tq = _tile(seq, 512)
    tk = tq // 2
    tm = 256

    xt = x.reshape(n_tok, d)
    h = _rmsnorm_rows(xt, attn_norm_g[0], BF16)
    proj = _in_projection(h, w_in[0].astype(BF16))

    lam = (jnp.exp(jnp.sum(lambda_q1[0].astype(F32) * lambda_k1[0].astype(F32)))
           - jnp.exp(jnp.sum(lambda_q2[0].astype(F32) * lambda_k2[0].astype(F32)))
           + LAMBDA_INIT).reshape(1)
    bias = _bias_tiles(rel_bias, tq)
    far_bias = rel_bias[-1].astype(F32)
    att = _diff_attention(proj, bias, far_bias, lam, diff_subln_g[0],
                          batch=batch, seq=seq, n_heads=n_heads, hd=hd, tq=tq, tk=tk)
    sgu = _spatial_gating(proj, sgu_ln_g[0], sgu_ln_b[0], sgu_w[0], sgu_b[0],
                          u_col0=2 * q_cols + attn_width)
    x1 = _out_projection(att, sgu, w_out[0].astype(BF16), xt)

    hp, top_idx, gates = _router(x1, ffn_norm_g[0], router_w[0], router_b[0])
    dest, pends, pcounts, block_e, n_valid, n_rows = _routing_tables(top_idx, n_exp, tm)
    xs = _dispatch(hp, dest, pends, pcounts, n_rows=n_rows, tm=tm)
    hs = _expert_up(xs, block_e, n_valid, w_gate_up[0], b_gate_up[0], tm=tm)
    ys = _expert_down(hs, block_e, n_valid, w_down[0], b_down[0], tm=tm)
    out = _combine(x1, gates, dest, ys, final_norm_g)
    return out.reshape(batch, seq, d)
```

```python
import functools
import math

import jax
import jax.numpy as jnp
from jax import lax
from jax.experimental import pallas as pl
from jax.experimental.pallas import tpu as pltpu

F32 = jnp.float32
BF16 = jnp.bfloat16

TOP_K = 4
MAX_DISTANCE = 128
SWIGLU_LIMIT = 7.0
SWIGLU_ALPHA = 1.702
NORM_EPS = 1e-5
LAMBDA_INIT = 0.8 - 0.6 * math.exp(0.0)
MASK_VALUE = -1e30

LANES = 128
VMEM_LIMIT_BYTES = 56 * 1024 * 1024


def _tile(dim, want):
    t = min(dim, want)
    while dim % t:
        t -= LANES
    assert t > 0, (dim, want)
    return t


def _params(*semantics):
    return pltpu.CompilerParams(dimension_semantics=semantics,
                                vmem_limit_bytes=VMEM_LIMIT_BYTES)


def _rmsnorm_kernel(x_ref, g_ref, o_ref):
    x = x_ref[...]
    ms = jnp.mean(x * x, axis=-1, keepdims=True)
    o_ref[...] = (x * lax.rsqrt(ms + NORM_EPS) * g_ref[...]).astype(o_ref.dtype)


def _rmsnorm_rows(x, g, out_dtype):
    n, d = x.shape
    tr = _tile(n, 256)
    return pl.pallas_call(
        _rmsnorm_kernel,
        grid=(n // tr,),
        in_specs=[pl.BlockSpec((tr, d), lambda i: (i, 0)),
                  pl.BlockSpec((1, d), lambda i: (0, 0))],
        out_specs=pl.BlockSpec((tr, d), lambda i: (i, 0)),
        out_shape=jax.ShapeDtypeStruct((n, d), out_dtype),
        compiler_params=_params("parallel"),
        name="rmsnorm_rows",
    )(x, g.reshape(1, d))


def _matmul_kernel(a_ref, b_ref, o_ref):
    o_ref[...] = jnp.dot(a_ref[...], b_ref[...],
                         preferred_element_type=F32).astype(o_ref.dtype)


def _in_projection(h, w):
    m, k = h.shape
    n = w.shape[1]
    tm, tn = _tile(m, 1024), _tile(n, 1024)
    return pl.pallas_call(
        _matmul_kernel,
        grid=(m // tm, n // tn),
        in_specs=[pl.BlockSpec((tm, k), lambda i, j: (i, 0)),
                  pl.BlockSpec((k, tn), lambda i, j: (0, j))],
        out_specs=pl.BlockSpec((tm, tn), lambda i, j: (i, j)),
        out_shape=jax.ShapeDtypeStruct((m, n), BF16),
        compiler_params=_params("parallel", "parallel"),
        name="in_projection",
    )(h, w)


def _out_projection_kernel(a_ref, s_ref, wa_ref, ws_ref, x_ref, o_ref):
    acc = jnp.dot(a_ref[...], wa_ref[...], preferred_element_type=F32)
    acc += jnp.dot(s_ref[...], ws_ref[...], preferred_element_type=F32)
    o_ref[...] = x_ref[...] + acc


def _out_projection(att, sgu, w, x):
    m, ka = att.shape
    ks = sgu.shape[1]
    n = w.shape[1]
    assert ka == ks and w.shape[0] == ka + ks
    tm, tn = _tile(m, 1024), _tile(n, 512)
    return pl.pallas_call(
        _out_projection_kernel,
        grid=(m // tm, n // tn),
        in_specs=[pl.BlockSpec((tm, ka), lambda i, j: (i, 0)),
                  pl.BlockSpec((tm, ks), lambda i, j: (i, 0)),
                  pl.BlockSpec((ka, tn), lambda i, j: (0, j)),
                  pl.BlockSpec((ks, tn), lambda i, j: (1, j)),
                  pl.BlockSpec((tm, tn), lambda i, j: (i, j))],
        out_specs=pl.BlockSpec((tm, tn), lambda i, j: (i, j)),
        out_shape=jax.ShapeDtypeStruct((m, n), F32),
        compiler_params=_params("parallel", "parallel"),
        name="out_projection",
    )(att, sgu, w, w, x)


def _bias_tiles_kernel(tbl_ref, o_ref, *, tq, n_buckets):
    h = pl.program_id(0)
    max_exact = n_buckets // 2
    key = lax.broadcasted_iota(jnp.int32, (tq, tq), 0)
    qry = lax.broadcasted_iota(jnp.int32, (tq, tq), 1)
    for w in range(2):
        dist = qry - key + w * tq
        d = jnp.maximum(dist, 1).astype(F32)
        large = max_exact + (jnp.log(d / max_exact) / math.log(MAX_DISTANCE / max_exact)
                             * (n_buckets - max_exact)).astype(jnp.int32)
        large = jnp.minimum(large, n_buckets - 1)
        bucket = jnp.where(dist < max_exact, dist, large)
        for m in range(2):
            val = jnp.zeros((tq, tq), F32)
            for b in range(n_buckets):
                val = jnp.where(bucket == b, tbl_ref[b, 2 * h + m], val)
            val = jnp.where(dist >= 0, val, MASK_VALUE)
            o_ref[0, w, :, m * tq:(m + 1) * tq] = val


def _bias_tiles(rel_bias, tq):
    n_buckets, n_maps = rel_bias.shape
    n_heads = n_maps // 2
    assert tq >= MAX_DISTANCE
    return pl.pallas_call(
        functools.partial(_bias_tiles_kernel, tq=tq, n_buckets=n_buckets),
        grid=(n_heads,),
        in_specs=[pl.BlockSpec(memory_space=pltpu.SMEM)],
        out_specs=pl.BlockSpec((1, 2, tq, 2 * tq), lambda h: (h, 0, 0, 0)),
        out_shape=jax.ShapeDtypeStruct((n_heads, 2, tq, 2 * tq), F32),
        compiler_params=_params("parallel"),
        name="bias_tiles",
    )(rel_bias.astype(F32))


def _attention_kernel(far_ref, lam_ref, q_ref, k_ref, v_ref, bias_ref, g_ref, o_ref,
                      vt_ref, sa_ref, sb_ref, acc_ref, *, tq, tk, hd):
    h = pl.program_id(1)
    i = pl.program_id(2)
    seq = k_ref.shape[0]
    sub = tq // tk
    assert sub == 2

    @pl.when(i == 0)
    def _():
        for c in range(seq // tk):
            vt_ref[c] = v_ref[c * tk:(c + 1) * tk, :].astype(F32).T.astype(BF16)

    qt = (q_ref[...].astype(F32) * hd ** -0.5).T
    dim = lax.broadcasted_iota(jnp.int32, qt.shape, 0)
    qqt = jnp.concatenate([jnp.where(dim < hd, qt, 0.0),
                           jnp.where(dim >= hd, qt, 0.0)], axis=1).astype(BF16)
    col = lax.broadcasted_iota(jnp.int32, (1, 2 * tq), 1)
    far = jnp.where(col < tq, far_ref[2 * h], far_ref[2 * h + 1])

    def scores(kb):
        kblk = k_ref[pl.ds(pl.multiple_of(kb * tk, tk), tk), :]
        return jnp.dot(kblk, qqt, preferred_element_type=F32)

    def weighted_values(p, kb):
        return jnp.dot(vt_ref[kb], p.astype(BF16), preferred_element_type=F32)

    kb_prev = jnp.maximum(i - 1, 0)
    no_prev = jnp.where(i >= 1, 0.0, MASK_VALUE)
    near = []
    for j in range(sub):
        rows = slice(j * tk, (j + 1) * tk)
        near.append((sub * i + j, scores(sub * i + j) + bias_ref[0, 0, rows, :]))
        near.append((sub * kb_prev + j,
                     scores(sub * kb_prev + j) + (bias_ref[0, 1, rows, :] + no_prev)))
    m = near[0][1].max(axis=0, keepdims=True)
    for _, s in near[1:]:
        m = jnp.maximum(m, jnp.max(s, axis=0, keepdims=True))
    l = jnp.zeros_like(m)
    acc = jnp.zeros(acc_ref.shape, F32)
    for kb, s in near:
        p = jnp.exp(s - m)
        l = l + jnp.sum(p, axis=0, keepdims=True)
        acc = acc + weighted_values(p, kb)
    acc_ref[...] = acc

    def online_update(s, kb, m_prev, l_prev):
        m_new = jnp.maximum(m_prev, jnp.max(s, axis=0, keepdims=True) + far)
        alpha = jnp.exp(m_prev - m_new)
        p = jnp.exp(s - (m_new - far))
        acc_ref[...] = alpha * acc_ref[...] + weighted_values(p, kb)
        return m_new, alpha * l_prev + jnp.sum(p, axis=0, keepdims=True)

    n_pairs = jnp.maximum(i - 1, 0)
    last = jnp.maximum(sub * n_pairs - 1, 0)

    def far_pair(n, carry):
        m_run, l_run = carry
        kb = sub * n
        sb_ref[...] = scores(kb + 1)
        m_run, l_run = online_update(sa_ref[...], kb, m_run, l_run)
        sa_ref[...] = scores(jnp.minimum(kb + 2, last))
        m_run, l_run = online_update(sb_ref[...], kb + 1, m_run, l_run)
        return m_run, l_run

    sa_ref[...] = scores(0)
    m, l = lax.fori_loop(0, n_pairs, far_pair, (m, l))

    o = acc_ref[...] * (1.0 / l)
    a = o[:, :tq] - lam_ref[0] * o[:, tq:]
    ms = jnp.mean(a * a, axis=0, keepdims=True)
    y = a * lax.rsqrt(ms + NORM_EPS) * g_ref[...] * (1.0 - LAMBDA_INIT)
    o_ref[...] = y.T.astype(o_ref.dtype)


def _diff_attention(proj, bias, far_bias, lam, subln_g, *, batch, seq, n_heads, hd, tq, tk):
    vd = subln_g.shape[-1]
    assert 2 * hd == LANES and vd == LANES
    nq = seq // tq
    k_col0 = n_heads
    v_col0 = 2 * n_heads
    grid_spec = pltpu.PrefetchScalarGridSpec(
        num_scalar_prefetch=2,
        grid=(batch, n_heads, nq),
        in_specs=[
            pl.BlockSpec((tq, LANES), lambda b, h, i, *_: (b * nq + i, h)),
            pl.BlockSpec((seq, LANES), lambda b, h, i, *_: (b, k_col0 + h)),
            pl.BlockSpec((seq, LANES), lambda b, h, i, *_: (b, v_col0 + h)),
            pl.BlockSpec((1, 2, tq, 2 * tq), lambda b, h, i, *_: (h, 0, 0, 0)),
            pl.BlockSpec((vd, 1), lambda b, h, i, *_: (0, 0)),
        ],
        out_specs=pl.BlockSpec((tq, vd), lambda b, h, i, *_: (b * nq + i, h)),
        scratch_shapes=[pltpu.VMEM((seq // tk, vd, tk), BF16),
                        pltpu.VMEM((tk, 2 * tq), F32),
                        pltpu.VMEM((tk, 2 * tq), F32),
                        pltpu.VMEM((vd, 2 * tq), F32)],
    )
    return pl.pallas_call(
        functools.partial(_attention_kernel, tq=tq, tk=tk, hd=hd),
        grid_spec=grid_spec,
        out_shape=jax.ShapeDtypeStruct((batch * seq, n_heads * vd), BF16),
        compiler_params=_params("parallel", "parallel", "arbitrary"),
        name="diff_attention",
    )(far_bias, lam, proj, proj, proj, bias, subln_g.reshape(vd, 1))


def _gelu(x):
    return 0.5 * x * (1.0 + lax.erf(x * math.sqrt(0.5)))


def _sgu_kernel(u_ref, v_ref, lng_ref, lnb_ref, w_ref, bt_ref, o_ref, *, n_heads, hdim):
    u = _gelu(u_ref[...].astype(F32))
    v = _gelu(v_ref[...].astype(F32))
    mu = jnp.mean(v, axis=-1, keepdims=True)
    vc = v - mu
    var = jnp.mean(vc * vc, axis=-1, keepdims=True)
    vn = (vc * lax.rsqrt(var + NORM_EPS) * lng_ref[...] + lnb_ref[...]).astype(BF16)
    chunk = w_ref.shape[-1]
    r = lax.broadcasted_iota(jnp.int32, (chunk, chunk), 0)
    c = lax.broadcasted_iota(jnp.int32, (chunk, chunk), 1)
    causal = r >= c
    for hh in range(n_heads):
        cols = slice(hh * hdim, (hh + 1) * hdim)
        w = jnp.where(causal, w_ref[hh], 0.0).astype(BF16)
        y = jnp.dot(w, vn[:, cols], preferred_element_type=F32) + bt_ref[:, hh:hh + 1]
        o_ref[:, cols] = (u[:, cols] * y).astype(o_ref.dtype)


def _spatial_gating(proj, ln_g, ln_b, w_s, b_s, *, u_col0):
    n_tok = proj.shape[0]
    n_heads, chunk, _ = w_s.shape
    width = ln_g.shape[-1]
    hdim = width // n_heads
    assert u_col0 % width == 0
    ub = u_col0 // width
    return pl.pallas_call(
        functools.partial(_sgu_kernel, n_heads=n_heads, hdim=hdim),
        grid=(n_tok // chunk,),
        in_specs=[pl.BlockSpec((chunk, width), lambda i: (i, ub)),
                  pl.BlockSpec((chunk, width), lambda i: (i, ub + 1)),
                  pl.BlockSpec((1, width), lambda i: (0, 0)),
                  pl.BlockSpec((1, width), lambda i: (0, 0)),
                  pl.BlockSpec((n_heads, chunk, chunk), lambda i: (0, 0, 0)),
                  pl.BlockSpec((chunk, n_heads), lambda i: (0, 0))],
        out_specs=pl.BlockSpec((chunk, width), lambda i: (i, 0)),
        out_shape=jax.ShapeDtypeStruct((n_tok, width), BF16),
        compiler_params=_params("parallel"),
        name="spatial_gating",
    )(proj, proj, ln_g.reshape(1, width), ln_b.reshape(1, width), w_s, b_s.T)


def _router_kernel(x_ref, g_ref, rw_ref, rb_ref, hp_ref, idx_ref, gate_ref):
    x = x_ref[...]
    ms = jnp.mean(x * x, axis=-1, keepdims=True)
    h = x * lax.rsqrt(ms + NORM_EPS) * g_ref[...]
    half = h.shape[-1] // 2
    hp_ref[...] = _pack_bf16_pairs(h[:, :half], h[:, half:])

    logits = jnp.dot(h, rw_ref[...], preferred_element_type=F32,
                     precision=lax.Precision.HIGHEST) + rb_ref[...]
    n_exp = logits.shape[-1]
    lane = lax.broadcasted_iota(jnp.int32, logits.shape, 1)
    vals, idxs = [], []
    for _ in range(TOP_K):
        top = jnp.max(logits, axis=-1, keepdims=True)
        idx = jnp.min(jnp.where(logits == top, lane, n_exp), axis=-1, keepdims=True)
        vals.append(top)
        idxs.append(idx)
        logits = jnp.where(lane == idx, -jnp.inf, logits)
    exps = [jnp.exp(v - vals[0]) for v in vals]
    denom = exps[0]
    for e in exps[1:]:
        denom = denom + e
    for k in range(TOP_K):
        idx_ref[:, k:k + 1] = idxs[k]
        gate_ref[:, k:k + 1] = exps[k] / denom


def _router(x, g, rw, rb):
    n, d = x.shape
    n_exp = rw.shape[1]
    tr = _tile(n, 256)
    return pl.pallas_call(
        _router_kernel,
        grid=(n // tr,),
        in_specs=[pl.BlockSpec((tr, d), lambda i: (i, 0)),
                  pl.BlockSpec((1, d), lambda i: (0, 0)),
                  pl.BlockSpec((d, n_exp), lambda i: (0, 0)),
                  pl.BlockSpec((1, n_exp), lambda i: (0, 0))],
        out_specs=[pl.BlockSpec((tr, d // 2), lambda i: (i, 0)),
                   pl.BlockSpec((tr, TOP_K), lambda i: (i, 0)),
                   pl.BlockSpec((tr, TOP_K), lambda i: (i, 0))],
        out_shape=[jax.ShapeDtypeStruct((n, d // 2), jnp.uint32),
                   jax.ShapeDtypeStruct((n, TOP_K), jnp.int32),
                   jax.ShapeDtypeStruct((n, TOP_K), F32)],
        compiler_params=_params("parallel"),
        name="router",
    )(x, g.reshape(1, d), rw, rb.reshape(1, n_exp))


def _dispatch_kernel(dest_ref, pends_ref, pcnt_ref, h_ref, xs_hbm, zbuf, zsem, sem,
                     *, tb, tm, n_exp):
    s = pl.program_id(0)

    n_rows = xs_hbm.shape[0]

    def zero_block(start):
        return pltpu.make_async_copy(
            zbuf, xs_hbm.at[pl.ds(pl.multiple_of(start, tm), tm), :], zsem)

    def zero_fill(action):
        for e in range(n_exp):
            @pl.when(pcnt_ref[e] > 0)
            def _():
                action(zero_block(pends_ref[e] - tm))
        for b in range(n_exp):
            start = pends_ref[n_exp - 1] + b * tm

            @pl.when(start < n_rows)
            def _():
                action(zero_block(start))

    @pl.when(s == 0)
    def _():
        zbuf[...] = jnp.zeros_like(zbuf)
        zero_fill(lambda copy: copy.start())
        zero_fill(lambda copy: copy.wait())

    def row_copy(t, slot):
        return pltpu.make_async_copy(h_ref.at[pl.ds(t, 1), :],
                                     xs_hbm.at[pl.ds(slot, 1), :], sem)

    def issue(t, carry):
        tok = s * tb + t
        for k in range(TOP_K):
            row_copy(t, dest_ref[tok * TOP_K + k]).start()
        return carry

    lax.fori_loop(0, tb, issue, 0)

    def drain(t, carry):
        for k in range(TOP_K):
            row_copy(0, 0).wait()
        return carry

    lax.fori_loop(0, tb, drain, 0)


def _dispatch(hp, dest, pends, pcounts, *, n_rows, tm):
    n_tok, dw = hp.shape
    n_exp = pends.shape[0]
    tb = _tile(n_tok, 128)
    grid_spec = pltpu.PrefetchScalarGridSpec(
        num_scalar_prefetch=3,
        grid=(n_tok // tb,),
        in_specs=[pl.BlockSpec((tb, dw), lambda i, *_: (i, 0))],
        out_specs=pl.BlockSpec(memory_space=pl.ANY),
        scratch_shapes=[pltpu.VMEM((tm, dw), hp.dtype),
                        pltpu.SemaphoreType.DMA(()),
                        pltpu.SemaphoreType.DMA(())],
    )
    return pl.pallas_call(
        functools.partial(_dispatch_kernel, tb=tb, tm=tm, n_exp=n_exp),
        grid_spec=grid_spec,
        out_shape=jax.ShapeDtypeStruct((n_rows, dw), hp.dtype),
        compiler_params=_params("arbitrary"),
        name="dispatch",
    )(dest, pends, pcounts, hp)


def _pack_bf16_pairs(hi, lo):
    hb = pltpu.bitcast(hi.astype(BF16).astype(F32), jnp.uint32)
    lb = pltpu.bitcast(lo.astype(BF16).astype(F32), jnp.uint32)
    return hb | (lb >> 16)


def _unpack_pairs_f32(p):
    return (pltpu.bitcast(p & jnp.uint32(0xFFFF0000), F32), pltpu.bitcast(p << 16, F32))


def _stream_expert_weights(be_ref, nv_ref, chg_ref, nxt_ref, w_hbm, stage, sems, w_bf,
                           *, half_stride, tn, nj):
    j = pl.program_id(0)
    i = pl.program_id(1)

    def copies(e, jj):
        return [pltpu.make_async_copy(
            w_hbm.at[e, :, pl.ds(pl.multiple_of(part * half_stride + jj * tn, tn), tn)],
            stage.at[part], sems.at[part]) for part in range(2)]

    change = chg_ref[i]

    @pl.when(change > 0)
    def _():
        @pl.when(jnp.logical_and(j == 0, change == 1))
        def _():
            for cp in copies(be_ref[i], j):
                cp.start()

        for part, cp in enumerate(copies(be_ref[i], j)):
            cp.wait()
            w_bf[part] = stage[part].astype(BF16)

        last = nxt_ref[i] < 0
        next_e = jnp.where(last, be_ref[0], nxt_ref[i])
        next_j = jnp.where(last, j + 1, j)

        @pl.when(next_j < nj)
        def _():
            for cp in copies(next_e, next_j):
                cp.start()


def _expert_up_kernel(be_ref, nv_ref, chg_ref, nxt_ref, xs_ref, w_hbm, bg_ref, bu_ref, o_ref,
                      stage, w_bf, sems, *, ff, tn, nj):
    i = pl.program_id(1)
    _stream_expert_weights(be_ref, nv_ref, chg_ref, nxt_ref, w_hbm, stage, sems, w_bf,
                           half_stride=ff, tn=tn, nj=nj)

    @pl.when(i < nv_ref[0])
    def _():
        x_hi, x_lo = _unpack_pairs_f32(xs_ref[...])
        x_hi, x_lo = x_hi.astype(BF16), x_lo.astype(BF16)
        half = x_hi.shape[-1]
        gate = (jnp.dot(x_hi, w_bf[0, :half, :], preferred_element_type=F32)
                + jnp.dot(x_lo, w_bf[0, half:, :], preferred_element_type=F32) + bg_ref[0])
        up = (jnp.dot(x_hi, w_bf[1, :half, :], preferred_element_type=F32)
              + jnp.dot(x_lo, w_bf[1, half:, :], preferred_element_type=F32) + bu_ref[0])
        gate = jnp.minimum(gate, SWIGLU_LIMIT)
        up = jnp.clip(up, -SWIGLU_LIMIT, SWIGLU_LIMIT)
        act = (up + 1.0) * gate * (1.0 / (1.0 + jnp.exp(-SWIGLU_ALPHA * gate)))
        o_ref[...] = act.astype(o_ref.dtype)

    @pl.when(i >= nv_ref[0])
    def _():
        o_ref[...] = jnp.zeros_like(o_ref)


def _expert_scratch(k, tn):
    return [pltpu.VMEM((2, k, tn), F32),
            pltpu.VMEM((2, k, tn), BF16),
            pltpu.SemaphoreType.DMA((2,))]


def _expert_up(xs, meta, w_gu, b_gu, *, tm):
    n_rows, dw = xs.shape
    n_exp, d, ff2 = w_gu.shape
    ff = ff2 // 2
    assert d == 2 * dw
    tn = _tile(ff, 512)
    nj = ff // tn
    n_blocks = n_rows // tm
    b3 = b_gu.reshape(n_exp, 1, ff2)
    grid_spec = pltpu.PrefetchScalarGridSpec(
        num_scalar_prefetch=4,
        grid=(nj, n_blocks),
        in_specs=[
            pl.BlockSpec((tm, dw), lambda j, i, be, nv, *_: (jnp.minimum(i, nv[0] - 1), 0)),
            pl.BlockSpec(memory_space=pl.ANY),
            pl.BlockSpec((1, 1, tn), lambda j, i, be, *_: (be[i], 0, j)),
            pl.BlockSpec((1, 1, tn), lambda j, i, be, *_: (be[i], 0, nj + j)),
        ],
        out_specs=pl.BlockSpec((tm, tn), lambda j, i, *_: (i, j)),
        scratch_shapes=_expert_scratch(d, tn),
    )
    return pl.pallas_call(
        functools.partial(_expert_up_kernel, ff=ff, tn=tn, nj=nj),
        grid_spec=grid_spec,
        out_shape=jax.ShapeDtypeStruct((n_rows, ff), BF16),
        compiler_params=_params("arbitrary", "arbitrary"),
        name="expert_up",
    )(*meta, xs, w_gu, b3, b3)


def _expert_down_kernel(be_ref, nv_ref, chg_ref, nxt_ref, h_ref, w_hbm, ba_ref, bb_ref, o_ref,
                        stage, w_bf, sems, *, half, tn, nj):
    i = pl.program_id(1)
    _stream_expert_weights(be_ref, nv_ref, chg_ref, nxt_ref, w_hbm, stage, sems, w_bf,
                           half_stride=half, tn=tn, nj=nj)

    @pl.when(i < nv_ref[0])
    def _():
        h = h_ref[...]
        ya = jnp.dot(h, w_bf[0], preferred_element_type=F32) + ba_ref[0]
        yb = jnp.dot(h, w_bf[1], preferred_element_type=F32) + bb_ref[0]
        o_ref[...] = _pack_bf16_pairs(ya, yb)

    @pl.when(i >= nv_ref[0])
    def _():
        o_ref[...] = jnp.zeros_like(o_ref)


def _expert_down(hs, meta, w_dn, b_dn, *, tm):
    n_rows, ff = hs.shape
    n_exp, _, d = w_dn.shape
    half = d // 2
    tn = _tile(half, 1024)
    nj = half // tn
    n_blocks = n_rows // tm
    b3 = b_dn.reshape(n_exp, 1, d)
    grid_spec = pltpu.PrefetchScalarGridSpec(
        num_scalar_prefetch=4,
        grid=(nj, n_blocks),
        in_specs=[
            pl.BlockSpec((tm, ff), lambda j, i, be, nv, *_: (jnp.minimum(i, nv[0] - 1), 0)),
            pl.BlockSpec(memory_space=pl.ANY),
            pl.BlockSpec((1, 1, tn), lambda j, i, be, *_: (be[i], 0, j)),
            pl.BlockSpec((1, 1, tn), lambda j, i, be, *_: (be[i], 0, nj + j)),
        ],
        out_specs=pl.BlockSpec((tm, tn), lambda j, i, *_: (i, j)),
        scratch_shapes=_expert_scratch(ff, tn),
    )
    return pl.pallas_call(
        functools.partial(_expert_down_kernel, half=half, tn=tn, nj=nj),
        grid_spec=grid_spec,
        out_shape=jax.ShapeDtypeStruct((n_rows, half), jnp.uint32),
        compiler_params=_params("arbitrary", "arbitrary"),
        name="expert_down",
    )(*meta, hs, w_dn, b3, b3)


def _combine_kernel(dest_ref, x_ref, gate_ref, g_ref, ys_hbm, o_ref, buf, sems, *, tb, n_steps):
    s = pl.program_id(0)
    cur = s & 1

    def row_copy(row, buf_slot, k, t):
        return pltpu.make_async_copy(ys_hbm.at[pl.ds(row, 1), :],
                                     buf.at[buf_slot, k, pl.ds(t, 1), :], sems.at[buf_slot])

    def gather(step, buf_slot):
        def issue(t, carry):
            tok = step * tb + t
            for k in range(TOP_K):
                row_copy(dest_ref[tok * TOP_K + k], buf_slot, k, t).start()
            return carry

        lax.fori_loop(0, tb, issue, 0)

    @pl.when(s == 0)
    def _():
        gather(0, 0)

    @pl.when(s + 1 < n_steps)
    def _():
        gather(s + 1, 1 - cur)

    def drain(t, carry):
        for k in range(TOP_K):
            row_copy(0, cur, k, t).wait()
        return carry

    lax.fori_loop(0, tb, drain, 0)

    half = x_ref.shape[-1] // 2
    acc_hi = x_ref[:, :half]
    acc_lo = x_ref[:, half:]
    for k in range(TOP_K):
        y_hi, y_lo = _unpack_pairs_f32(buf[cur, k])
        gate = gate_ref[:, k:k + 1]
        acc_hi = acc_hi + gate * y_hi
        acc_lo = acc_lo + gate * y_lo
    ms = (jnp.sum(acc_hi * acc_hi, axis=-1, keepdims=True)
          + jnp.sum(acc_lo * acc_lo, axis=-1, keepdims=True)) / (2 * half)
    r = lax.rsqrt(ms + NORM_EPS)
    o_ref[:, :half] = acc_hi * r * g_ref[:, :half]
    o_ref[:, half:] = acc_lo * r * g_ref[:, half:]


def _combine(x, gates, dest, ys, g):
    n_tok, d = x.shape
    tb = _tile(n_tok, 128)
    n_steps = n_tok // tb
    grid_spec = pltpu.PrefetchScalarGridSpec(
        num_scalar_prefetch=1,
        grid=(n_steps,),
        in_specs=[pl.BlockSpec((tb, d), lambda i, dest: (i, 0)),
                  pl.BlockSpec((tb, TOP_K), lambda i, dest: (i, 0)),
                  pl.BlockSpec((1, d), lambda i, dest: (0, 0)),
                  pl.BlockSpec(memory_space=pl.ANY)],
        out_specs=pl.BlockSpec((tb, d), lambda i, dest: (i, 0)),
        scratch_shapes=[pltpu.VMEM((2, TOP_K, tb, d // 2), ys.dtype),
                        pltpu.SemaphoreType.DMA((2,))],
    )
    return pl.pallas_call(
        functools.partial(_combine_kernel, tb=tb, n_steps=n_steps),
        grid_spec=grid_spec,
        out_shape=jax.ShapeDtypeStruct((n_tok, d), F32),
        compiler_params=_params("arbitrary"),
        name="combine",
    )(dest, x, gates, g.reshape(1, d), ys)


def _routing_tables(top_idx, n_exp, tm):
    flat_e = top_idx.reshape(-1)
    n_pairs = flat_e.shape[0]
    onehot = (flat_e[:, None] == jnp.arange(n_exp, dtype=jnp.int32)[None, :]).astype(jnp.int32)
    csum = jnp.cumsum(onehot, axis=0)
    counts = csum[-1]
    rank = jnp.sum(onehot * csum, axis=1) - 1
    pcounts = (counts + tm - 1) // tm * tm
    pends = jnp.cumsum(pcounts)
    pstarts = pends - pcounts
    dest = jnp.sum(onehot * pstarts[None, :], axis=1) + rank
    assert n_pairs % tm == 0
    n_blocks = n_pairs // tm + n_exp
    block_id = jnp.arange(n_blocks, dtype=jnp.int32)
    block_e = jnp.minimum(jnp.searchsorted(pends, block_id * tm, side='right'),
                          n_exp - 1).astype(jnp.int32)
    n_valid = (pends[-1] // tm).astype(jnp.int32)
    prev_e = jnp.concatenate([jnp.full((1,), -1, jnp.int32), block_e[:-1]])
    is_change = (block_id < n_valid) & (block_e != prev_e)
    change_ord = jnp.cumsum(is_change.astype(jnp.int32))
    change = jnp.where(is_change, change_ord, 0).astype(jnp.int32)
    later = is_change[None, :] & (block_id[None, :] > block_id[:, None])
    next_e = jnp.where(jnp.any(later, axis=1), block_e[jnp.argmax(later, axis=1)], -1)
    meta = (block_e, n_valid.reshape(1), change, next_e.astype(jnp.int32))
    return (dest.astype(jnp.int32), pends.astype(jnp.int32), pcounts.astype(jnp.int32),
            meta, n_blocks * tm)


def kernel(x, attn_norm_g, w_in, lambda_q1, lambda_k1, lambda_q2, lambda_k2, diff_subln_g,
           sgu_ln_g, sgu_ln_b, sgu_w, sgu_b, rel_bias, w_out, ffn_norm_g, router_w, router_b,
           w_gate_up, b_gate_up, w_down, b_down, final_norm_g):
    batch, seq, d = x.shape
    n_tok = batch * seq
    hd = lambda_q1.shape[-1]
    n_maps = rel_bias.shape[1]
    n_heads = n_maps // 2
    q_cols = n_maps * hd
    attn_width = n_heads * diff_subln_g.shape[-1]
    n_exp = w_gate_up.shape[1]
    tq = _tile(seq, 512)
    tk = tq // 2
    tm = 256

    xt = x.reshape(n_tok, d)
    h = _rmsnorm_rows(xt, attn_norm_g[0], BF16)
    proj = _in_projection(h, w_in[0].astype(BF16))

    lam = (jnp.exp(jnp.sum(lambda_q1[0].astype(F32) * lambda_k1[0].astype(F32)))
           - jnp.exp(jnp.sum(lambda_q2[0].astype(F32) * lambda_k2[0].astype(F32)))
           + LAMBDA_INIT).reshape(1)
    bias = _bias_tiles(rel_bias, tq)
    far_bias = rel_bias[-1].astype(F32)
    att = _diff_attention(proj, bias, far_bias, lam, diff_subln_g[0],
                          batch=batch, seq=seq, n_heads=n_heads, hd=hd, tq=tq, tk=tk)
    sgu = _spatial_gating(proj, sgu_ln_g[0], sgu_ln_b[0], sgu_w[0], sgu_b[0],
                          u_col0=2 * q_cols + attn_width)
    x1 = _out_projection(att, sgu, w_out[0].astype(BF16), xt)

    hp, top_idx, gates = _router(x1, ffn_norm_g[0], router_w[0], router_b[0])
    dest, pends, pcounts, meta, n_rows = _routing_tables(top_idx, n_exp, tm)
    xs = _dispatch(hp, dest, pends, pcounts, n_rows=n_rows, tm=tm)
    hs = _expert_up(xs, meta, w_gate_up[0], b_gate_up[0], tm=tm)
    ys = _expert_down(hs, meta, w_down[0], b_down[0], tm=tm)
    out = _combine(x1, gates, dest, ys, final_norm_g)
    return out.reshape(batch, seq, d)
```

```python
import functools
import math

import jax
import jax.numpy as jnp
from jax import lax
from jax.experimental import pallas as pl
from jax.experimental.pallas import tpu as pltpu

F32 = jnp.float32
BF16 = jnp.bfloat16

TOP_K = 4
MAX_DISTANCE = 128
SWIGLU_LIMIT = 7.0
SWIGLU_ALPHA = 1.702
NORM_EPS = 1e-5
LAMBDA_INIT = 0.8 - 0.6 * math.exp(0.0)
MASK_VALUE = -1e30
LOG2E = math.log2(math.e)

LANES = 128
BF16_SUBLANES = 16
VMEM_LIMIT_BYTES = 56 * 1024 * 1024


def _tile(dim, want):
    t = min(dim, want)
    while dim % t:
        t -= LANES
    assert t > 0, (dim, want)
    return t


def _params(*semantics):
    return pltpu.CompilerParams(dimension_semantics=semantics,
                                vmem_limit_bytes=VMEM_LIMIT_BYTES)


def _rmsnorm_kernel(x_ref, g_ref, o_ref):
    x = x_ref[...]
    ms = jnp.mean(x * x, axis=-1, keepdims=True)
    o_ref[...] = (x * lax.rsqrt(ms + NORM_EPS) * g_ref[...]).astype(o_ref.dtype)


def _rmsnorm_rows(x, g, out_dtype):
    n, d = x.shape
    tr = _tile(n, 256)
    return pl.pallas_call(
        _rmsnorm_kernel,
        grid=(n // tr,),
        in_specs=[pl.BlockSpec((tr, d), lambda i: (i, 0)),
                  pl.BlockSpec((1, d), lambda i: (0, 0))],
        out_specs=pl.BlockSpec((tr, d), lambda i: (i, 0)),
        out_shape=jax.ShapeDtypeStruct((n, d), out_dtype),
        compiler_params=_params("parallel"),
        name="rmsnorm_rows",
    )(x, g.reshape(1, d))


def _matmul_kernel(a_ref, b_ref, o_ref):
    o_ref[...] = jnp.dot(a_ref[...], b_ref[...],
                         preferred_element_type=F32).astype(o_ref.dtype)


def _in_projection(h, w):
    m, k = h.shape
    n = w.shape[1]
    tm, tn = _tile(m, 1024), _tile(n, 1024)
    return pl.pallas_call(
        _matmul_kernel,
        grid=(m // tm, n // tn),
        in_specs=[pl.BlockSpec((tm, k), lambda i, j: (i, 0)),
                  pl.BlockSpec((k, tn), lambda i, j: (0, j))],
        out_specs=pl.BlockSpec((tm, tn), lambda i, j: (i, j)),
        out_shape=jax.ShapeDtypeStruct((m, n), BF16),
        compiler_params=_params("parallel", "parallel"),
        name="in_projection",
    )(h, w)


def _out_projection_kernel(a_ref, s_ref, wa_ref, ws_ref, x_ref, o_ref):
    acc = jnp.dot(a_ref[...], wa_ref[...], preferred_element_type=F32)
    acc += jnp.dot(s_ref[...], ws_ref[...], preferred_element_type=F32)
    o_ref[...] = x_ref[...] + acc


def _out_projection(att, sgu, w, x):
    m, ka = att.shape
    ks = sgu.shape[1]
    n = w.shape[1]
    assert ka == ks and w.shape[0] == ka + ks
    tm, tn = _tile(m, 1024), _tile(n, 512)
    return pl.pallas_call(
        _out_projection_kernel,
        grid=(m // tm, n // tn),
        in_specs=[pl.BlockSpec((tm, ka), lambda i, j: (i, 0)),
                  pl.BlockSpec((tm, ks), lambda i, j: (i, 0)),
                  pl.BlockSpec((ka, tn), lambda i, j: (0, j)),
                  pl.BlockSpec((ks, tn), lambda i, j: (1, j)),
                  pl.BlockSpec((tm, tn), lambda i, j: (i, j))],
        out_specs=pl.BlockSpec((tm, tn), lambda i, j: (i, j)),
        out_shape=jax.ShapeDtypeStruct((m, n), F32),
        compiler_params=_params("parallel", "parallel"),
        name="out_projection",
    )(att, sgu, w, w, x)


def _bias_tiles_kernel(tbl_ref, o_ref, *, tq, n_buckets):
    h = pl.program_id(0)
    max_exact = n_buckets // 2
    key = lax.broadcasted_iota(jnp.int32, (tq, tq), 0)
    qry = lax.broadcasted_iota(jnp.int32, (tq, tq), 1)
    for w in range(2):
        dist = qry - key + w * tq
        d = jnp.maximum(dist, 1).astype(F32)
        large = max_exact + (jnp.log(d / max_exact) / math.log(MAX_DISTANCE / max_exact)
                             * (n_buckets - max_exact)).astype(jnp.int32)
        large = jnp.minimum(large, n_buckets - 1)
        bucket = jnp.where(dist < max_exact, dist, large)
        for m in range(2):
            val = jnp.zeros((tq, tq), F32)
            for b in range(n_buckets):
                val = jnp.where(bucket == b, tbl_ref[b, 2 * h + m], val)
            val = jnp.where(dist >= 0, val * LOG2E, MASK_VALUE)
            o_ref[0, w, :, m * tq:(m + 1) * tq] = val


def _bias_tiles(rel_bias, tq):
    n_buckets, n_maps = rel_bias.shape
    n_heads = n_maps // 2
    assert tq >= MAX_DISTANCE
    return pl.pallas_call(
        functools.partial(_bias_tiles_kernel, tq=tq, n_buckets=n_buckets),
        grid=(n_heads,),
        in_specs=[pl.BlockSpec(memory_space=pltpu.SMEM)],
        out_specs=pl.BlockSpec((1, 2, tq, 2 * tq), lambda h: (h, 0, 0, 0)),
        out_shape=jax.ShapeDtypeStruct((n_heads, 2, tq, 2 * tq), F32),
        compiler_params=_params("parallel"),
        name="bias_tiles",
    )(rel_bias.astype(F32))


def _attention_kernel(far_ref, lam_ref, q_ref, k_ref, v_ref, bias_ref, g_ref, o_ref,
                      vt_ref, sa_ref, sb_ref, acc_ref, *, tq, tk, hd):
    h = pl.program_id(1)
    i = pl.program_id(2)
    seq, vd = v_ref.shape
    sub = tq // tk
    assert sub == 2

    @pl.when(i == 0)
    def _():
        ones_row = lax.broadcasted_iota(jnp.int32, (BF16_SUBLANES, tk), 0) == 0
        for c in range(seq // tk):
            vt_ref[c, :vd, :] = v_ref[c * tk:(c + 1) * tk, :].astype(F32).T.astype(BF16)
            vt_ref[c, vd:, :] = jnp.where(ones_row, 1.0, 0.0).astype(BF16)

    qt = (q_ref[...].astype(F32) * (hd ** -0.5 * LOG2E)).T
    dim = lax.broadcasted_iota(jnp.int32, qt.shape, 0)
    qqt = jnp.concatenate([jnp.where(dim < hd, qt, 0.0),
                           jnp.where(dim >= hd, qt, 0.0)], axis=1).astype(BF16)
    col = lax.broadcasted_iota(jnp.int32, (1, 2 * tq), 1)
    far = jnp.where(col < tq, far_ref[2 * h], far_ref[2 * h + 1]) * LOG2E

    def scores(kb):
        kblk = k_ref[pl.ds(pl.multiple_of(kb * tk, tk), tk), :]
        return jnp.dot(kblk, qqt, preferred_element_type=F32)

    def online_update(s, kb, m_prev, shift):
        m_new = jnp.maximum(m_prev, jnp.max(s, axis=0, keepdims=True) + shift)
        alpha = jnp.exp2(m_prev - m_new)
        p = jnp.exp2((s - (m_new - shift)).astype(BF16))
        acc_ref[...] = alpha * acc_ref[...] + jnp.dot(vt_ref[kb], p,
                                                      preferred_element_type=F32)
        return m_new

    kb_prev = sub * jnp.maximum(i - 1, 0)
    no_prev = jnp.where(i >= 1, 0.0, MASK_VALUE)
    n_pairs = jnp.maximum(i - 1, 0)
    last = jnp.maximum(sub * n_pairs - 1, 0)
    acc_ref[...] = jnp.zeros_like(acc_ref)
    m = jnp.full((1, 2 * tq), MASK_VALUE, F32)

    sa_ref[...] = scores(sub * i) + bias_ref[0, 0, :tk, :]
    sb_ref[...] = scores(sub * i + 1) + bias_ref[0, 0, tk:, :]
    m = online_update(sa_ref[...], sub * i, m, 0.0)
    sa_ref[...] = scores(kb_prev) + (bias_ref[0, 1, :tk, :] + no_prev)
    m = online_update(sb_ref[...], sub * i + 1, m, 0.0)
    sb_ref[...] = scores(kb_prev + 1) + (bias_ref[0, 1, tk:, :] + no_prev)
    m = online_update(sa_ref[...], kb_prev, m, 0.0)
    sa_ref[...] = scores(0)
    m = online_update(sb_ref[...], kb_prev + 1, m, 0.0)

    def far_pair(n, m_run):
        kb = sub * n
        sb_ref[...] = scores(kb + 1)
        m_run = online_update(sa_ref[...], kb, m_run, far)
        sa_ref[...] = scores(jnp.minimum(kb + 2, last))
        return online_update(sb_ref[...], kb + 1, m_run, far)

    lax.fori_loop(0, n_pairs, far_pair, m)

    acc = acc_ref[...]
    o = acc[:vd] * (1.0 / acc[vd:vd + 1])
    a = o[:, :tq] - lam_ref[0] * o[:, tq:]
    ms = jnp.mean(a * a, axis=0, keepdims=True)
    y = a * lax.rsqrt(ms + NORM_EPS) * g_ref[...] * (1.0 - LAMBDA_INIT)
    o_ref[...] = y.T.astype(o_ref.dtype)


def _diff_attention(proj, bias, far_bias, lam, subln_g, *, batch, seq, n_heads, hd, tq, tk):
    vd = subln_g.shape[-1]
    assert 2 * hd == LANES and vd == LANES
    nq = seq // tq
    k_col0 = n_heads
    v_col0 = 2 * n_heads
    grid_spec = pltpu.PrefetchScalarGridSpec(
        num_scalar_prefetch=2,
        grid=(batch, n_heads, nq),
        in_specs=[
            pl.BlockSpec((tq, LANES), lambda b, h, i, *_: (b * nq + i, h)),
            pl.BlockSpec((seq, LANES), lambda b, h, i, *_: (b, k_col0 + h)),
            pl.BlockSpec((seq, LANES), lambda b, h, i, *_: (b, v_col0 + h)),
            pl.BlockSpec((1, 2, tq, 2 * tq), lambda b, h, i, *_: (h, 0, 0, 0)),
            pl.BlockSpec((vd, 1), lambda b, h, i, *_: (0, 0)),
        ],
        out_specs=pl.BlockSpec((tq, vd), lambda b, h, i, *_: (b * nq + i, h)),
        scratch_shapes=[pltpu.VMEM((seq // tk, vd + BF16_SUBLANES, tk), BF16),
                        pltpu.VMEM((tk, 2 * tq), F32),
                        pltpu.VMEM((tk, 2 * tq), F32),
                        pltpu.VMEM((vd + BF16_SUBLANES, 2 * tq), F32)],
    )
    return pl.pallas_call(
        functools.partial(_attention_kernel, tq=tq, tk=tk, hd=hd),
        grid_spec=grid_spec,
        out_shape=jax.ShapeDtypeStruct((batch * seq, n_heads * vd), BF16),
        compiler_params=_params("parallel", "parallel", "arbitrary"),
        name="diff_attention",
    )(far_bias, lam, proj, proj, proj, bias, subln_g.reshape(vd, 1))


def _gelu(x):
    return 0.5 * x * (1.0 + lax.erf(x * math.sqrt(0.5)))


def _sgu_kernel(u_ref, v_ref, lng_ref, lnb_ref, w_ref, bt_ref, o_ref, *, n_heads, hdim):
    u = _gelu(u_ref[...].astype(F32))
    v = _gelu(v_ref[...].astype(F32))
    mu = jnp.mean(v, axis=-1, keepdims=True)
    vc = v - mu
    var = jnp.mean(vc * vc, axis=-1, keepdims=True)
    vn = (vc * lax.rsqrt(var + NORM_EPS) * lng_ref[...] + lnb_ref[...]).astype(BF16)
    chunk = w_ref.shape[-1]
    r = lax.broadcasted_iota(jnp.int32, (chunk, chunk), 0)
    c = lax.broadcasted_iota(jnp.int32, (chunk, chunk), 1)
    causal = r >= c
    for hh in range(n_heads):
        cols = slice(hh * hdim, (hh + 1) * hdim)
        w = jnp.where(causal, w_ref[hh], 0.0).astype(BF16)
        y = jnp.dot(w, vn[:, cols], preferred_element_type=F32) + bt_ref[:, hh:hh + 1]
        o_ref[:, cols] = (u[:, cols] * y).astype(o_ref.dtype)


def _spatial_gating(proj, ln_g, ln_b, w_s, b_s, *, u_col0):
    n_tok = proj.shape[0]
    n_heads, chunk, _ = w_s.shape
    width = ln_g.shape[-1]
    hdim = width // n_heads
    assert u_col0 % width == 0
    ub = u_col0 // width
    return pl.pallas_call(
        functools.partial(_sgu_kernel, n_heads=n_heads, hdim=hdim),
        grid=(n_tok // chunk,),
        in_specs=[pl.BlockSpec((chunk, width), lambda i: (i, ub)),
                  pl.BlockSpec((chunk, width), lambda i: (i, ub + 1)),
                  pl.BlockSpec((1, width), lambda i: (0, 0)),
                  pl.BlockSpec((1, width), lambda i: (0, 0)),
                  pl.BlockSpec((n_heads, chunk, chunk), lambda i: (0, 0, 0)),
                  pl.BlockSpec((chunk, n_heads), lambda i: (0, 0))],
        out_specs=pl.BlockSpec((chunk, width), lambda i: (i, 0)),
        out_shape=jax.ShapeDtypeStruct((n_tok, width), BF16),
        compiler_params=_params("parallel"),
        name="spatial_gating",
    )(proj, proj, ln_g.reshape(1, width), ln_b.reshape(1, width), w_s, b_s.T)


def _split_bf16(x):
    hi = x.astype(BF16)
    return hi, (x - hi.astype(F32)).astype(BF16)


def _router_kernel(x_ref, g_ref, rw_ref, rb_ref, hp_ref, idx_ref, gate_ref, rank_ref, cnt_ref):
    @pl.when(pl.program_id(0) == 0)
    def _():
        cnt_ref[...] = jnp.zeros_like(cnt_ref)

    x = x_ref[...]
    ms = jnp.mean(x * x, axis=-1, keepdims=True)
    h = x * lax.rsqrt(ms + NORM_EPS) * g_ref[...]
    half = h.shape[-1] // 2
    hp_ref[...] = _pack_bf16_pairs(h[:, :half], h[:, half:])

    h_hi, h_lo = _split_bf16(h)
    w_hi, w_lo = _split_bf16(rw_ref[...])
    logits = (jnp.dot(h_hi, w_hi, preferred_element_type=F32)
              + jnp.dot(h_lo, w_hi, preferred_element_type=F32)
              + jnp.dot(h_hi, w_lo, preferred_element_type=F32)) + rb_ref[...]
    tr, n_exp = logits.shape
    lane = lax.broadcasted_iota(jnp.int32, logits.shape, 1)
    vals, idxs = [], []
    for _ in range(TOP_K):
        top = jnp.max(logits, axis=-1, keepdims=True)
        idx = jnp.min(jnp.where(logits == top, lane, n_exp), axis=-1, keepdims=True)
        vals.append(top)
        idxs.append(idx)
        logits = jnp.where(lane == idx, -jnp.inf, logits)
    exps = [jnp.exp(v - vals[0]) for v in vals]
    denom = exps[0]
    for e in exps[1:]:
        denom = denom + e

    earlier = (lax.broadcasted_iota(jnp.int32, (tr, tr), 1)
               < lax.broadcasted_iota(jnp.int32, (tr, tr), 0))
    earlier = jnp.where(earlier, 1.0, 0.0).astype(BF16)
    count = cnt_ref[...]
    for k in range(TOP_K):
        chosen = lane == idxs[k]
        onehot = jnp.where(chosen, 1.0, 0.0)
        before = jnp.dot(earlier, onehot.astype(BF16), preferred_element_type=F32) + count
        rank = jnp.sum(jnp.where(chosen, before, 0.0), axis=-1, keepdims=True)
        idx_ref[:, k:k + 1] = idxs[k]
        gate_ref[:, k:k + 1] = exps[k] / denom
        rank_ref[:, k:k + 1] = rank.astype(jnp.int32)
        count = count + jnp.sum(onehot, axis=0, keepdims=True)
    cnt_ref[...] = count


def _router(x, g, rw, rb):
    n, d = x.shape
    n_exp = rw.shape[1]
    tr = _tile(n, 256)
    assert n * TOP_K < 2 ** 24
    return pl.pallas_call(
        _router_kernel,
        grid=(n // tr,),
        in_specs=[pl.BlockSpec((tr, d), lambda i: (i, 0)),
                  pl.BlockSpec((1, d), lambda i: (0, 0)),
                  pl.BlockSpec((d, n_exp), lambda i: (0, 0)),
                  pl.BlockSpec((1, n_exp), lambda i: (0, 0))],
        out_specs=[pl.BlockSpec((tr, d // 2), lambda i: (i, 0)),
                   pl.BlockSpec((tr, TOP_K), lambda i: (i, 0)),
                   pl.BlockSpec((tr, TOP_K), lambda i: (i, 0)),
                   pl.BlockSpec((tr, TOP_K), lambda i: (i, 0)),
                   pl.BlockSpec((1, n_exp), lambda i: (0, 0))],
        out_shape=[jax.ShapeDtypeStruct((n, d // 2), jnp.uint32),
                   jax.ShapeDtypeStruct((n, TOP_K), jnp.int32),
                   jax.ShapeDtypeStruct((n, TOP_K), F32),
                   jax.ShapeDtypeStruct((n, TOP_K), jnp.int32),
                   jax.ShapeDtypeStruct((1, n_exp), F32)],
        compiler_params=_params("arbitrary"),
        name="router",
    )(x, g.reshape(1, d), rw, rb.reshape(1, n_exp))


def _dispatch_kernel(dest_ref, pends_ref, pcnt_ref, h_ref, xs_hbm, zbuf, zsem, sem,
                     *, tb, tm, n_exp):
    s = pl.program_id(0)

    n_rows = xs_hbm.shape[0]

    def zero_block(start):
        return pltpu.make_async_copy(
            zbuf, xs_hbm.at[pl.ds(pl.multiple_of(start, tm), tm), :], zsem)

    def zero_fill(action):
        for e in range(n_exp):
            @pl.when(pcnt_ref[e] > 0)
            def _():
                action(zero_block(pends_ref[e] - tm))
        for b in range(n_exp):
            start = pends_ref[n_exp - 1] + b * tm

            @pl.when(start < n_rows)
            def _():
                action(zero_block(start))

    @pl.when(s == 0)
    def _():
        zbuf[...] = jnp.zeros_like(zbuf)
        zero_fill(lambda copy: copy.start())
        zero_fill(lambda copy: copy.wait())

    def row_copy(t, slot):
        return pltpu.make_async_copy(h_ref.at[pl.ds(t, 1), :],
                                     xs_hbm.at[pl.ds(slot, 1), :], sem)

    def issue(t, carry):
        tok = s * tb + t
        for k in range(TOP_K):
            row_copy(t, dest_ref[tok * TOP_K + k]).start()
        return carry

    lax.fori_loop(0, tb, issue, 0)

    def drain(t, carry):
        for k in range(TOP_K):
            row_copy(0, 0).wait()
        return carry

    lax.fori_loop(0, tb, drain, 0)


def _dispatch(hp, dest, pends, pcounts, *, n_rows, tm):
    n_tok, dw = hp.shape
    n_exp = pends.shape[0]
    tb = _tile(n_tok, 128)
    grid_spec = pltpu.PrefetchScalarGridSpec(
        num_scalar_prefetch=3,
        grid=(n_tok // tb,),
        in_specs=[pl.BlockSpec((tb, dw), lambda i, *_: (i, 0))],
        out_specs=pl.BlockSpec(memory_space=pl.ANY),
        scratch_shapes=[pltpu.VMEM((tm, dw), hp.dtype),
                        pltpu.SemaphoreType.DMA(()),
                        pltpu.SemaphoreType.DMA(())],
    )
    return pl.pallas_call(
        functools.partial(_dispatch_kernel, tb=tb, tm=tm, n_exp=n_exp),
        grid_spec=grid_spec,
        out_shape=jax.ShapeDtypeStruct((n_rows, dw), hp.dtype),
        compiler_params=_params("arbitrary"),
        name="dispatch",
    )(dest, pends, pcounts, hp)


def _pack_bf16_pairs(hi, lo):
    hb = pltpu.bitcast(hi.astype(BF16).astype(F32), jnp.uint32)
    lb = pltpu.bitcast(lo.astype(BF16).astype(F32), jnp.uint32)
    return hb | (lb >> 16)


def _unpack_pairs_f32(p):
    return (pltpu.bitcast(p & jnp.uint32(0xFFFF0000), F32), pltpu.bitcast(p << 16, F32))


def _stream_expert_weights(be_ref, nv_ref, chg_ref, nxt_ref, w_hbm, stage, sems, w_bf,
                           *, half_stride, tn, nj):
    j = pl.program_id(0)
    i = pl.program_id(1)

    def copies(e, jj):
        return [pltpu.make_async_copy(
            w_hbm.at[e, :, pl.ds(pl.multiple_of(part * half_stride + jj * tn, tn), tn)],
            stage.at[part], sems.at[part]) for part in range(2)]

    change = chg_ref[i]

    @pl.when(change > 0)
    def _():
        @pl.when(jnp.logical_and(j == 0, change == 1))
        def _():
            for cp in copies(be_ref[i], j):
                cp.start()

        for part, cp in enumerate(copies(be_ref[i], j)):
            cp.wait()
            w_bf[part] = stage[part].astype(BF16)

        last = nxt_ref[i] < 0
        next_e = jnp.where(last, be_ref[0], nxt_ref[i])
        next_j = jnp.where(last, j + 1, j)

        @pl.when(next_j < nj)
        def _():
            for cp in copies(next_e, next_j):
                cp.start()


def _expert_up_kernel(be_ref, nv_ref, chg_ref, nxt_ref, xs_ref, w_hbm, bg_ref, bu_ref, o_ref,
                      stage, w_bf, sems, *, ff, tn, nj):
    i = pl.program_id(1)
    _stream_expert_weights(be_ref, nv_ref, chg_ref, nxt_ref, w_hbm, stage, sems, w_bf,
                           half_stride=ff, tn=tn, nj=nj)

    @pl.when(i < nv_ref[0])
    def _():
        x_hi, x_lo = _unpack_pairs_f32(xs_ref[...])
        x_hi, x_lo = x_hi.astype(BF16), x_lo.astype(BF16)
        half = x_hi.shape[-1]
        gate = (jnp.dot(x_hi, w_bf[0, :half, :], preferred_element_type=F32)
                + jnp.dot(x_lo, w_bf[0, half:, :], preferred_element_type=F32) + bg_ref[0])
        up = (jnp.dot(x_hi, w_bf[1, :half, :], preferred_element_type=F32)
              + jnp.dot(x_lo, w_bf[1, half:, :], preferred_element_type=F32) + bu_ref[0])
        gate = jnp.minimum(gate, SWIGLU_LIMIT)
        up = jnp.clip(up, -SWIGLU_LIMIT, SWIGLU_LIMIT)
        act = (up + 1.0) * gate * (1.0 / (1.0 + jnp.exp(-SWIGLU_ALPHA * gate)))
        o_ref[...] = act.astype(o_ref.dtype)

    @pl.when(i >= nv_ref[0])
    def _():
        o_ref[...] = jnp.zeros_like(o_ref)


def _expert_scratch(k, tn):
    return [pltpu.VMEM((2, k, tn), F32),
            pltpu.VMEM((2, k, tn), BF16),
            pltpu.SemaphoreType.DMA((2,))]


def _expert_up(xs, meta, w_gu, b_gu, *, tm):
    n_rows, dw = xs.shape
    n_exp, d, ff2 = w_gu.shape
    ff = ff2 // 2
    assert d == 2 * dw
    tn = _tile(ff, 512)
    nj = ff // tn
    n_blocks = n_rows // tm
    b3 = b_gu.reshape(n_exp, 1, ff2)
    grid_spec = pltpu.PrefetchScalarGridSpec(
        num_scalar_prefetch=4,
        grid=(nj, n_blocks),
        in_specs=[
            pl.BlockSpec((tm, dw), lambda j, i, be, nv, *_: (jnp.minimum(i, nv[0] - 1), 0)),
            pl.BlockSpec(memory_space=pl.ANY),
            pl.BlockSpec((1, 1, tn), lambda j, i, be, *_: (be[i], 0, j)),
            pl.BlockSpec((1, 1, tn), lambda j, i, be, *_: (be[i], 0, nj + j)),
        ],
        out_specs=pl.BlockSpec((tm, tn), lambda j, i, *_: (i, j)),
        scratch_shapes=_expert_scratch(d, tn),
    )
    return pl.pallas_call(
        functools.partial(_expert_up_kernel, ff=ff, tn=tn, nj=nj),
        grid_spec=grid_spec,
        out_shape=jax.ShapeDtypeStruct((n_rows, ff), BF16),
        compiler_params=_params("arbitrary", "arbitrary"),
        name="expert_up",
    )(*meta, xs, w_gu, b3, b3)


def _expert_down_kernel(be_ref, nv_ref, chg_ref, nxt_ref, h_ref, w_hbm, ba_ref, bb_ref, o_ref,
                        stage, w_bf, sems, *, half, tn, nj):
    i = pl.program_id(1)
    _stream_expert_weights(be_ref, nv_ref, chg_ref, nxt_ref, w_hbm, stage, sems, w_bf,
                           half_stride=half, tn=tn, nj=nj)

    @pl.when(i < nv_ref[0])
    def _():
        h = h_ref[...]
        ya = jnp.dot(h, w_bf[0], preferred_element_type=F32) + ba_ref[0]
        yb = jnp.dot(h, w_bf[1], preferred_element_type=F32) + bb_ref[0]
        o_ref[...] = _pack_bf16_pairs(ya, yb)

    @pl.when(i >= nv_ref[0])
    def _():
        o_ref[...] = jnp.zeros_like(o_ref)


def _expert_down(hs, meta, w_dn, b_dn, *, tm):
    n_rows, ff = hs.shape
    n_exp, _, d = w_dn.shape
    half = d // 2
    tn = _tile(half, 1024)
    nj = half // tn
    n_blocks = n_rows // tm
    b3 = b_dn.reshape(n_exp, 1, d)
    grid_spec = pltpu.PrefetchScalarGridSpec(
        num_scalar_prefetch=4,
        grid=(nj, n_blocks),
        in_specs=[
            pl.BlockSpec((tm, ff), lambda j, i, be, nv, *_: (jnp.minimum(i, nv[0] - 1), 0)),
            pl.BlockSpec(memory_space=pl.ANY),
            pl.BlockSpec((1, 1, tn), lambda j, i, be, *_: (be[i], 0, j)),
            pl.BlockSpec((1, 1, tn), lambda j, i, be, *_: (be[i], 0, nj + j)),
        ],
        out_specs=pl.BlockSpec((tm, tn), lambda j, i, *_: (i, j)),
        scratch_shapes=_expert_scratch(ff, tn),
    )
    return pl.pallas_call(
        functools.partial(_expert_down_kernel, half=half, tn=tn, nj=nj),
        grid_spec=grid_spec,
        out_shape=jax.ShapeDtypeStruct((n_rows, half), jnp.uint32),
        compiler_params=_params("arbitrary", "arbitrary"),
        name="expert_down",
    )(*meta, hs, w_dn, b3, b3)


def _combine_kernel(dest_ref, x_ref, gate_ref, g_ref, ys_hbm, o_ref, buf, sems, *, tb, n_steps):
    s = pl.program_id(0)
    cur = s & 1

    def row_copy(row, buf_slot, k, t):
        return pltpu.make_async_copy(ys_hbm.at[pl.ds(row, 1), :],
                                     buf.at[buf_slot, k, pl.ds(t, 1), :], sems.at[buf_slot])

    def gather(step, buf_slot):
        def issue(t, carry):
            tok = step * tb + t
            for k in range(TOP_K):
                row_copy(dest_ref[tok * TOP_K + k], buf_slot, k, t).start()
            return carry

        lax.fori_loop(0, tb, issue, 0)

    @pl.when(s == 0)
    def _():
        gather(0, 0)

    @pl.when(s + 1 < n_steps)
    def _():
        gather(s + 1, 1 - cur)

    def drain(t, carry):
        for k in range(TOP_K):
            row_copy(0, cur, k, t).wait()
        return carry

    lax.fori_loop(0, tb, drain, 0)

    half = x_ref.shape[-1] // 2
    acc_hi = x_ref[:, :half]
    acc_lo = x_ref[:, half:]
    for k in range(TOP_K):
        y_hi, y_lo = _unpack_pairs_f32(buf[cur, k])
        gate = gate_ref[:, k:k + 1]
        acc_hi = acc_hi + gate * y_hi
        acc_lo = acc_lo + gate * y_lo
    ms = (jnp.sum(acc_hi * acc_hi, axis=-1, keepdims=True)
          + jnp.sum(acc_lo * acc_lo, axis=-1, keepdims=True)) / (2 * half)
    r = lax.rsqrt(ms + NORM_EPS)
    o_ref[:, :half] = acc_hi * r * g_ref[:, :half]
    o_ref[:, half:] = acc_lo * r * g_ref[:, half:]


def _combine(x, gates, dest, ys, g):
    n_tok, d = x.shape
    tb = _tile(n_tok, 128)
    n_steps = n_tok // tb
    grid_spec = pltpu.PrefetchScalarGridSpec(
        num_scalar_prefetch=1,
        grid=(n_steps,),
        in_specs=[pl.BlockSpec((tb, d), lambda i, dest: (i, 0)),
                  pl.BlockSpec((tb, TOP_K), lambda i, dest: (i, 0)),
                  pl.BlockSpec((1, d), lambda i, dest: (0, 0)),
                  pl.BlockSpec(memory_space=pl.ANY)],
        out_specs=pl.BlockSpec((tb, d), lambda i, dest: (i, 0)),
        scratch_shapes=[pltpu.VMEM((2, TOP_K, tb, d // 2), ys.dtype),
                        pltpu.SemaphoreType.DMA((2,))],
    )
    return pl.pallas_call(
        functools.partial(_combine_kernel, tb=tb, n_steps=n_steps),
        grid_spec=grid_spec,
        out_shape=jax.ShapeDtypeStruct((n_tok, d), F32),
        compiler_params=_params("arbitrary"),
        name="combine",
    )(dest, x, gates, g.reshape(1, d), ys)


def _routing_tables(top_idx, rank, counts, tm):
    n_exp = counts.shape[0]
    flat_e = top_idx.reshape(-1)
    n_pairs = flat_e.shape[0]
    pcounts = (counts + tm - 1) // tm * tm
    pends = jnp.cumsum(pcounts)
    pstarts = pends - pcounts
    onehot = flat_e[:, None] == jnp.arange(n_exp, dtype=jnp.int32)[None, :]
    dest = jnp.sum(jnp.where(onehot, pstarts[None, :], 0), axis=1) + rank.reshape(-1)
    assert n_pairs % tm == 0
    n_blocks = n_pairs // tm + n_exp
    block_id = jnp.arange(n_blocks, dtype=jnp.int32)
    block_e = jnp.minimum(jnp.searchsorted(pends, block_id * tm, side='right'),
                          n_exp - 1).astype(jnp.int32)
    n_valid = (pends[-1] // tm).astype(jnp.int32)
    prev_e = jnp.concatenate([jnp.full((1,), -1, jnp.int32), block_e[:-1]])
    is_change = (block_id < n_valid) & (block_e != prev_e)
    change_ord = jnp.cumsum(is_change.astype(jnp.int32))
    change = jnp.where(is_change, change_ord, 0).astype(jnp.int32)
    later = is_change[None, :] & (block_id[None, :] > block_id[:, None])
    next_e = jnp.where(jnp.any(later, axis=1), block_e[jnp.argmax(later, axis=1)], -1)
    meta = (block_e, n_valid.reshape(1), change, next_e.astype(jnp.int32))
    return (dest.astype(jnp.int32), pends.astype(jnp.int32), pcounts.astype(jnp.int32),
            meta, n_blocks * tm)


def kernel(x, attn_norm_g, w_in, lambda_q1, lambda_k1, lambda_q2, lambda_k2, diff_subln_g,
           sgu_ln_g, sgu_ln_b, sgu_w, sgu_b, rel_bias, w_out, ffn_norm_g, router_w, router_b,
           w_gate_up, b_gate_up, w_down, b_down, final_norm_g):
    batch, seq, d = x.shape
    n_tok = batch * seq
    hd = lambda_q1.shape[-1]
    n_maps = rel_bias.shape[1]
    n_heads = n_maps // 2
    q_cols = n_maps * hd
    attn_width = n_heads * diff_subln_g.shape[-1]
    n_exp = w_gate_up.shape[1]
    tq = _tile(seq, 512)
    tk = tq // 2
    tm = 256

    xt = x.reshape(n_tok, d)
    h = _rmsnorm_rows(xt, attn_norm_g[0], BF16)
    proj = _in_projection(h, w_in[0].astype(BF16))

    lam = (jnp.exp(jnp.sum(lambda_q1[0].astype(F32) * lambda_k1[0].astype(F32)))
           - jnp.exp(jnp.sum(lambda_q2[0].astype(F32) * lambda_k2[0].astype(F32)))
           + LAMBDA_INIT).reshape(1)
    bias = _bias_tiles(rel_bias, tq)
    far_bias = rel_bias[-1].astype(F32)
    att = _diff_attention(proj, bias, far_bias, lam, diff_subln_g[0],
                          batch=batch, seq=seq, n_heads=n_heads, hd=hd, tq=tq, tk=tk)
    sgu = _spatial_gating(proj, sgu_ln_g[0], sgu_ln_b[0], sgu_w[0], sgu_b[0],
                          u_col0=2 * q_cols + attn_width)
    x1 = _out_projection(att, sgu, w_out[0].astype(BF16), xt)

    hp, top_idx, gates, rank, counts = _router(x1, ffn_norm_g[0], router_w[0], router_b[0])
    dest, pends, pcounts, meta, n_rows = _routing_tables(
        top_idx, rank, counts[0].astype(jnp.int32), tm)
    xs = _dispatch(hp, dest, pends, pcounts, n_rows=n_rows, tm=tm)
    hs = _expert_up(xs, meta, w_gate_up[0], b_gate_up[0], tm=tm)
    ys = _expert_down(hs, meta, w_down[0], b_down[0], tm=tm)
    out = _combine(x1, gates, dest, ys, final_norm_g)
    return out.reshape(batch, seq, d)
```

```python
import functools
import math

import jax
import jax.numpy as jnp
from jax import lax
from jax.experimental import pallas as pl
from jax.experimental.pallas import tpu as pltpu

F32 = jnp.float32
BF16 = jnp.bfloat16

TOP_K = 4
MAX_DISTANCE = 128
SWIGLU_LIMIT = 7.0
SWIGLU_ALPHA = 1.702
NORM_EPS = 1e-5
LAMBDA_INIT = 0.8 - 0.6 * math.exp(0.0)
MASK_VALUE = -1e30
LOG2E = math.log2(math.e)

LANES = 128
BF16_SUBLANES = 16
ROW_GROUP = 8
VMEM_LIMIT_BYTES = 56 * 1024 * 1024


def _tile(dim, want):
    t = min(dim, want)
    while dim % t:
        t -= LANES
    assert t > 0, (dim, want)
    return t


def _params(*semantics):
    return pltpu.CompilerParams(dimension_semantics=semantics,
                                vmem_limit_bytes=VMEM_LIMIT_BYTES)


def _rmsnorm_kernel(x_ref, g_ref, o_ref):
    x = x_ref[...]
    ms = jnp.mean(x * x, axis=-1, keepdims=True)
    o_ref[...] = (x * lax.rsqrt(ms + NORM_EPS) * g_ref[...]).astype(o_ref.dtype)


def _rmsnorm_rows(x, g, out_dtype):
    n, d = x.shape
    tr = _tile(n, 256)
    return pl.pallas_call(
        _rmsnorm_kernel,
        grid=(n // tr,),
        in_specs=[pl.BlockSpec((tr, d), lambda i: (i, 0)),
                  pl.BlockSpec((1, d), lambda i: (0, 0))],
        out_specs=pl.BlockSpec((tr, d), lambda i: (i, 0)),
        out_shape=jax.ShapeDtypeStruct((n, d), out_dtype),
        compiler_params=_params("parallel"),
        name="rmsnorm_rows",
    )(x, g.reshape(1, d))


def _matmul_kernel(a_ref, b_ref, o_ref):
    o_ref[...] = jnp.dot(a_ref[...], b_ref[...],
                         preferred_element_type=F32).astype(o_ref.dtype)


def _in_projection(h, w):
    m, k = h.shape
    n = w.shape[1]
    tm, tn = _tile(m, 1024), _tile(n, 1024)
    return pl.pallas_call(
        _matmul_kernel,
        grid=(m // tm, n // tn),
        in_specs=[pl.BlockSpec((tm, k), lambda i, j: (i, 0)),
                  pl.BlockSpec((k, tn), lambda i, j: (0, j))],
        out_specs=pl.BlockSpec((tm, tn), lambda i, j: (i, j)),
        out_shape=jax.ShapeDtypeStruct((m, n), BF16),
        compiler_params=_params("parallel", "parallel"),
        name="in_projection",
    )(h, w)


def _out_projection_kernel(a_ref, s_ref, wa_ref, ws_ref, x_ref, o_ref):
    acc = jnp.dot(a_ref[...], wa_ref[...], preferred_element_type=F32)
    acc += jnp.dot(s_ref[...], ws_ref[...], preferred_element_type=F32)
    o_ref[...] = x_ref[...] + acc


def _out_projection(att, sgu, w, x):
    m, ka = att.shape
    ks = sgu.shape[1]
    n = w.shape[1]
    assert ka == ks and w.shape[0] == ka + ks
    tm, tn = _tile(m, 1024), _tile(n, 512)
    return pl.pallas_call(
        _out_projection_kernel,
        grid=(m // tm, n // tn),
        in_specs=[pl.BlockSpec((tm, ka), lambda i, j: (i, 0)),
                  pl.BlockSpec((tm, ks), lambda i, j: (i, 0)),
                  pl.BlockSpec((ka, tn), lambda i, j: (0, j)),
                  pl.BlockSpec((ks, tn), lambda i, j: (1, j)),
                  pl.BlockSpec((tm, tn), lambda i, j: (i, j))],
        out_specs=pl.BlockSpec((tm, tn), lambda i, j: (i, j)),
        out_shape=jax.ShapeDtypeStruct((m, n), F32),
        compiler_params=_params("parallel", "parallel"),
        name="out_projection",
    )(att, sgu, w, w, x)


def _bias_tiles_kernel(tbl_ref, o_ref, *, tq, n_buckets):
    h = pl.program_id(0)
    max_exact = n_buckets // 2
    key = lax.broadcasted_iota(jnp.int32, (tq, tq), 0)
    qry = lax.broadcasted_iota(jnp.int32, (tq, tq), 1)
    for w in range(2):
        dist = qry - key + w * tq
        d = jnp.maximum(dist, 1).astype(F32)
        large = max_exact + (jnp.log(d / max_exact) / math.log(MAX_DISTANCE / max_exact)
                             * (n_buckets - max_exact)).astype(jnp.int32)
        large = jnp.minimum(large, n_buckets - 1)
        bucket = jnp.where(dist < max_exact, dist, large)
        for m in range(2):
            val = jnp.zeros((tq, tq), F32)
            for b in range(n_buckets):
                val = jnp.where(bucket == b, tbl_ref[b, 2 * h + m], val)
            val = jnp.where(dist >= 0, val * LOG2E, MASK_VALUE)
            o_ref[0, w, :, m * tq:(m + 1) * tq] = val


def _bias_tiles(rel_bias, tq):
    n_buckets, n_maps = rel_bias.shape
    n_heads = n_maps // 2
    assert tq >= MAX_DISTANCE
    return pl.pallas_call(
        functools.partial(_bias_tiles_kernel, tq=tq, n_buckets=n_buckets),
        grid=(n_heads,),
        in_specs=[pl.BlockSpec(memory_space=pltpu.SMEM)],
        out_specs=pl.BlockSpec((1, 2, tq, 2 * tq), lambda h: (h, 0, 0, 0)),
        out_shape=jax.ShapeDtypeStruct((n_heads, 2, tq, 2 * tq), F32),
        compiler_params=_params("parallel"),
        name="bias_tiles",
    )(rel_bias.astype(F32))


def _attention_kernel(far_ref, lam_ref, q_ref, k_ref, v_ref, bias_ref, g_ref, o_ref,
                      vt_ref, sa_ref, sb_ref, acc_ref, *, tq, tk, hd):
    h = pl.program_id(1)
    i = pl.program_id(2)
    seq, vd = v_ref.shape
    sub = tq // tk
    assert sub == 2

    @pl.when(i == 0)
    def _():
        ones_row = lax.broadcasted_iota(jnp.int32, (BF16_SUBLANES, tk), 0) == 0
        for c in range(seq // tk):
            vt_ref[c, :vd, :] = v_ref[c * tk:(c + 1) * tk, :].astype(F32).T.astype(BF16)
            vt_ref[c, vd:, :] = jnp.where(ones_row, 1.0, 0.0).astype(BF16)

    qt = (q_ref[...].astype(F32) * (hd ** -0.5 * LOG2E)).T
    dim = lax.broadcasted_iota(jnp.int32, qt.shape, 0)
    qqt = jnp.concatenate([jnp.where(dim < hd, qt, 0.0),
                           jnp.where(dim >= hd, qt, 0.0)], axis=1).astype(BF16)
    col = lax.broadcasted_iota(jnp.int32, (1, 2 * tq), 1)
    far = jnp.where(col < tq, far_ref[2 * h], far_ref[2 * h + 1]) * LOG2E

    def scores(kb):
        kblk = k_ref[pl.ds(pl.multiple_of(kb * tk, tk), tk), :]
        return jnp.dot(kblk, qqt, preferred_element_type=F32)

    def online_update(s, kb, m_prev, shift):
        m_new = jnp.maximum(m_prev, jnp.max(s, axis=0, keepdims=True) + shift)
        alpha = jnp.exp2(m_prev - m_new)
        p = jnp.exp2((s - (m_new - shift)).astype(BF16))
        acc_ref[...] = alpha * acc_ref[...] + jnp.dot(vt_ref[kb], p,
                                                      preferred_element_type=F32)
        return m_new

    kb_prev = sub * jnp.maximum(i - 1, 0)
    no_prev = jnp.where(i >= 1, 0.0, MASK_VALUE)
    n_pairs = jnp.maximum(i - 1, 0)
    last = jnp.maximum(sub * n_pairs - 1, 0)
    acc_ref[...] = jnp.zeros_like(acc_ref)
    m = jnp.full((1, 2 * tq), MASK_VALUE, F32)

    sa_ref[...] = scores(sub * i) + bias_ref[0, 0, :tk, :]
    sb_ref[...] = scores(sub * i + 1) + bias_ref[0, 0, tk:, :]
    m = online_update(sa_ref[...], sub * i, m, 0.0)
    sa_ref[...] = scores(kb_prev) + (bias_ref[0, 1, :tk, :] + no_prev)
    m = online_update(sb_ref[...], sub * i + 1, m, 0.0)
    sb_ref[...] = scores(kb_prev + 1) + (bias_ref[0, 1, tk:, :] + no_prev)
    m = online_update(sa_ref[...], kb_prev, m, 0.0)
    sa_ref[...] = scores(0)
    m = online_update(sb_ref[...], kb_prev + 1, m, 0.0)

    def far_pair(n, m_run):
        kb = sub * n
        sb_ref[...] = scores(kb + 1)
        m_run = online_update(sa_ref[...], kb, m_run, far)
        sa_ref[...] = scores(jnp.minimum(kb + 2, last))
        return online_update(sb_ref[...], kb + 1, m_run, far)

    lax.fori_loop(0, n_pairs, far_pair, m)

    acc = acc_ref[...]
    o = acc[:vd] * (1.0 / acc[vd:vd + 1])
    a = o[:, :tq] - lam_ref[0] * o[:, tq:]
    ms = jnp.mean(a * a, axis=0, keepdims=True)
    y = a * lax.rsqrt(ms + NORM_EPS) * g_ref[...] * (1.0 - LAMBDA_INIT)
    o_ref[...] = y.T.astype(o_ref.dtype)


def _diff_attention(proj, bias, far_bias, lam, subln_g, *, batch, seq, n_heads, hd, tq, tk):
    vd = subln_g.shape[-1]
    assert 2 * hd == LANES and vd == LANES
    nq = seq // tq
    k_col0 = n_heads
    v_col0 = 2 * n_heads
    grid_spec = pltpu.PrefetchScalarGridSpec(
        num_scalar_prefetch=2,
        grid=(batch, n_heads, nq),
        in_specs=[
            pl.BlockSpec((tq, LANES), lambda b, h, i, *_: (b * nq + i, h)),
            pl.BlockSpec((seq, LANES), lambda b, h, i, *_: (b, k_col0 + h)),
            pl.BlockSpec((seq, LANES), lambda b, h, i, *_: (b, v_col0 + h)),
            pl.BlockSpec((1, 2, tq, 2 * tq), lambda b, h, i, *_: (h, 0, 0, 0)),
            pl.BlockSpec((vd, 1), lambda b, h, i, *_: (0, 0)),
        ],
        out_specs=pl.BlockSpec((tq, vd), lambda b, h, i, *_: (b * nq + i, h)),
        scratch_shapes=[pltpu.VMEM((seq // tk, vd + BF16_SUBLANES, tk), BF16),
                        pltpu.VMEM((tk, 2 * tq), F32),
                        pltpu.VMEM((tk, 2 * tq), F32),
                        pltpu.VMEM((vd + BF16_SUBLANES, 2 * tq), F32)],
    )
    return pl.pallas_call(
        functools.partial(_attention_kernel, tq=tq, tk=tk, hd=hd),
        grid_spec=grid_spec,
        out_shape=jax.ShapeDtypeStruct((batch * seq, n_heads * vd), BF16),
        compiler_params=_params("parallel", "parallel", "arbitrary"),
        name="diff_attention",
    )(far_bias, lam, proj, proj, proj, bias, subln_g.reshape(vd, 1))


def _gelu(x):
    return 0.5 * x * (1.0 + lax.erf(x * math.sqrt(0.5)))


def _sgu_kernel(u_ref, v_ref, lng_ref, lnb_ref, w_ref, bt_ref, o_ref, *, n_heads, hdim):
    u = _gelu(u_ref[...].astype(F32))
    v = _gelu(v_ref[...].astype(F32))
    mu = jnp.mean(v, axis=-1, keepdims=True)
    vc = v - mu
    var = jnp.mean(vc * vc, axis=-1, keepdims=True)
    vn = (vc * lax.rsqrt(var + NORM_EPS) * lng_ref[...] + lnb_ref[...]).astype(BF16)
    chunk = w_ref.shape[-1]
    r = lax.broadcasted_iota(jnp.int32, (chunk, chunk), 0)
    c = lax.broadcasted_iota(jnp.int32, (chunk, chunk), 1)
    causal = r >= c
    for hh in range(n_heads):
        cols = slice(hh * hdim, (hh + 1) * hdim)
        w = jnp.where(causal, w_ref[hh], 0.0).astype(BF16)
        y = jnp.dot(w, vn[:, cols], preferred_element_type=F32) + bt_ref[:, hh:hh + 1]
        o_ref[:, cols] = (u[:, cols] * y).astype(o_ref.dtype)


def _spatial_gating(proj, ln_g, ln_b, w_s, b_s, *, u_col0):
    n_tok = proj.shape[0]
    n_heads, chunk, _ = w_s.shape
    width = ln_g.shape[-1]
    hdim = width // n_heads
    assert u_col0 % width == 0
    ub = u_col0 // width
    return pl.pallas_call(
        functools.partial(_sgu_kernel, n_heads=n_heads, hdim=hdim),
        grid=(n_tok // chunk,),
        in_specs=[pl.BlockSpec((chunk, width), lambda i: (i, ub)),
                  pl.BlockSpec((chunk, width), lambda i: (i, ub + 1)),
                  pl.BlockSpec((1, width), lambda i: (0, 0)),
                  pl.BlockSpec((1, width), lambda i: (0, 0)),
                  pl.BlockSpec((n_heads, chunk, chunk), lambda i: (0, 0, 0)),
                  pl.BlockSpec((chunk, n_heads), lambda i: (0, 0))],
        out_specs=pl.BlockSpec((chunk, width), lambda i: (i, 0)),
        out_shape=jax.ShapeDtypeStruct((n_tok, width), BF16),
        compiler_params=_params("parallel"),
        name="spatial_gating",
    )(proj, proj, ln_g.reshape(1, width), ln_b.reshape(1, width), w_s, b_s.T)


def _split_bf16(x):
    hi = x.astype(BF16)
    return hi, (x - hi.astype(F32)).astype(BF16)


def _router_kernel(x_ref, g_ref, rw_ref, rb_ref, hp_ref, idx_ref, gate_ref, rank_ref, cnt_ref):
    @pl.when(pl.program_id(0) == 0)
    def _():
        cnt_ref[...] = jnp.zeros_like(cnt_ref)

    x = x_ref[...]
    ms = jnp.mean(x * x, axis=-1, keepdims=True)
    h = x * lax.rsqrt(ms + NORM_EPS) * g_ref[...]
    half = h.shape[-1] // 2
    hp_ref[...] = _pack_bf16_pairs(h[:, :half], h[:, half:])

    h_hi, h_lo = _split_bf16(h)
    w_hi, w_lo = _split_bf16(rw_ref[...])
    logits = (jnp.dot(h_hi, w_hi, preferred_element_type=F32)
              + jnp.dot(h_lo, w_hi, preferred_element_type=F32)
              + jnp.dot(h_hi, w_lo, preferred_element_type=F32)) + rb_ref[...]
    tr, n_exp = logits.shape
    lane = lax.broadcasted_iota(jnp.int32, logits.shape, 1)
    vals, idxs = [], []
    for _ in range(TOP_K):
        top = jnp.max(logits, axis=-1, keepdims=True)
        idx = jnp.min(jnp.where(logits == top, lane, n_exp), axis=-1, keepdims=True)
        vals.append(top)
        idxs.append(idx)
        logits = jnp.where(lane == idx, -jnp.inf, logits)
    exps = [jnp.exp(v - vals[0]) for v in vals]
    denom = exps[0]
    for e in exps[1:]:
        denom = denom + e

    earlier = (lax.broadcasted_iota(jnp.int32, (tr, tr), 1)
               < lax.broadcasted_iota(jnp.int32, (tr, tr), 0))
    earlier = jnp.where(earlier, 1.0, 0.0).astype(BF16)
    count = cnt_ref[...]
    for k in range(TOP_K):
        chosen = lane == idxs[k]
        onehot = jnp.where(chosen, 1.0, 0.0)
        before = jnp.dot(earlier, onehot.astype(BF16), preferred_element_type=F32) + count
        rank = jnp.sum(jnp.where(chosen, before, 0.0), axis=-1, keepdims=True)
        idx_ref[:, k:k + 1] = idxs[k]
        gate_ref[:, k:k + 1] = exps[k] / denom
        rank_ref[:, k:k + 1] = rank.astype(jnp.int32)
        count = count + jnp.sum(onehot, axis=0, keepdims=True)
    cnt_ref[...] = count


def _router(x, g, rw, rb):
    n, d = x.shape
    n_exp = rw.shape[1]
    tr = _tile(n, 256)
    assert n * TOP_K < 2 ** 24
    return pl.pallas_call(
        _router_kernel,
        grid=(n // tr,),
        in_specs=[pl.BlockSpec((tr, d), lambda i: (i, 0)),
                  pl.BlockSpec((1, d), lambda i: (0, 0)),
                  pl.BlockSpec((d, n_exp), lambda i: (0, 0)),
                  pl.BlockSpec((1, n_exp), lambda i: (0, 0))],
        out_specs=[pl.BlockSpec((tr, d // 2), lambda i: (i, 0)),
                   pl.BlockSpec((tr, TOP_K), lambda i: (i, 0)),
                   pl.BlockSpec((tr, TOP_K), lambda i: (i, 0)),
                   pl.BlockSpec((tr, TOP_K), lambda i: (i, 0)),
                   pl.BlockSpec((1, n_exp), lambda i: (0, 0))],
        out_shape=[jax.ShapeDtypeStruct((n, d // 2), jnp.uint32),
                   jax.ShapeDtypeStruct((n, TOP_K), jnp.int32),
                   jax.ShapeDtypeStruct((n, TOP_K), F32),
                   jax.ShapeDtypeStruct((n, TOP_K), jnp.int32),
                   jax.ShapeDtypeStruct((1, n_exp), F32)],
        compiler_params=_params("arbitrary"),
        name="router",
    )(x, g.reshape(1, d), rw, rb.reshape(1, n_exp))


def _dispatch_kernel(dest_ref, pends_ref, pcnt_ref, h_ref, xs_hbm, zbuf, zsem, sem,
                     *, tb, tm, n_exp):
    s = pl.program_id(0)

    n_rows = xs_hbm.shape[0]

    def zero_block(start):
        return pltpu.make_async_copy(
            zbuf, xs_hbm.at[pl.ds(pl.multiple_of(start, tm), tm), :], zsem)

    def zero_fill(action):
        for e in range(n_exp):
            @pl.when(pcnt_ref[e] > 0)
            def _():
                action(zero_block(pends_ref[e] - tm))
        for b in range(n_exp):
            start = pends_ref[n_exp - 1] + b * tm

            @pl.when(start < n_rows)
            def _():
                action(zero_block(start))

    @pl.when(s == 0)
    def _():
        zbuf[...] = jnp.zeros_like(zbuf)
        zero_fill(lambda copy: copy.start())
        zero_fill(lambda copy: copy.wait())

    def row_copy(t, slot):
        return pltpu.make_async_copy(h_ref.at[pl.ds(t, 1), :],
                                     xs_hbm.at[pl.ds(slot, 1), :], sem)

    def issue(group, carry):
        t0 = pl.multiple_of(group * ROW_GROUP, ROW_GROUP)
        for u in range(ROW_GROUP):
            tok = s * tb + t0 + u
            for k in range(TOP_K):
                row_copy(t0 + u, dest_ref[tok * TOP_K + k]).start(priority=k % 2)
        return carry

    lax.fori_loop(0, tb // ROW_GROUP, issue, 0)

    def drain(t, carry):
        for k in range(TOP_K):
            row_copy(0, 0).wait()
        return carry

    lax.fori_loop(0, tb, drain, 0)


def _dispatch(hp, dest, pends, pcounts, *, n_rows, tm):
    n_tok, dw = hp.shape
    n_exp = pends.shape[0]
    tb = _tile(n_tok, 128)
    grid_spec = pltpu.PrefetchScalarGridSpec(
        num_scalar_prefetch=3,
        grid=(n_tok // tb,),
        in_specs=[pl.BlockSpec((tb, dw), lambda i, *_: (i, 0))],
        out_specs=pl.BlockSpec(memory_space=pl.ANY),
        scratch_shapes=[pltpu.VMEM((tm, dw), hp.dtype),
                        pltpu.SemaphoreType.DMA(()),
                        pltpu.SemaphoreType.DMA(())],
    )
    return pl.pallas_call(
        functools.partial(_dispatch_kernel, tb=tb, tm=tm, n_exp=n_exp),
        grid_spec=grid_spec,
        out_shape=jax.ShapeDtypeStruct((n_rows, dw), hp.dtype),
        compiler_params=_params("arbitrary"),
        name="dispatch",
    )(dest, pends, pcounts, hp)


def _pack_bf16_pairs(hi, lo):
    hb = pltpu.bitcast(hi.astype(BF16).astype(F32), jnp.uint32)
    lb = pltpu.bitcast(lo.astype(BF16).astype(F32), jnp.uint32)
    return hb | (lb >> 16)


def _unpack_pairs_f32(p):
    return (pltpu.bitcast(p & jnp.uint32(0xFFFF0000), F32), pltpu.bitcast(p << 16, F32))


def _stream_expert_weights(be_ref, nv_ref, chg_ref, nxt_ref, w_hbm, stage, sems, w_bf,
                           *, half_stride, tn, nj):
    j = pl.program_id(0)
    i = pl.program_id(1)

    def copies(e, jj):
        return [pltpu.make_async_copy(
            w_hbm.at[e, :, pl.ds(pl.multiple_of(part * half_stride + jj * tn, tn), tn)],
            stage.at[part], sems.at[part]) for part in range(2)]

    change = chg_ref[i]

    @pl.when(change > 0)
    def _():
        @pl.when(jnp.logical_and(j == 0, change == 1))
        def _():
            for cp in copies(be_ref[i], j):
                cp.start()

        for part, cp in enumerate(copies(be_ref[i], j)):
            cp.wait()
            w_bf[part] = stage[part].astype(BF16)

        last = nxt_ref[i] < 0
        next_e = jnp.where(last, be_ref[0], nxt_ref[i])
        next_j = jnp.where(last, j + 1, j)

        @pl.when(next_j < nj)
        def _():
            for cp in copies(next_e, next_j):
                cp.start(priority=1)


def _expert_up_kernel(be_ref, nv_ref, chg_ref, nxt_ref, xs_ref, w_hbm, bg_ref, bu_ref, o_ref,
                      stage, w_bf, sems, *, ff, tn, nj):
    i = pl.program_id(1)
    _stream_expert_weights(be_ref, nv_ref, chg_ref, nxt_ref, w_hbm, stage, sems, w_bf,
                           half_stride=ff, tn=tn, nj=nj)

    @pl.when(i < nv_ref[0])
    def _():
        x_hi, x_lo = _unpack_pairs_f32(xs_ref[...])
        x_hi, x_lo = x_hi.astype(BF16), x_lo.astype(BF16)
        half = x_hi.shape[-1]
        gate = (jnp.dot(x_hi, w_bf[0, :half, :], preferred_element_type=F32)
                + jnp.dot(x_lo, w_bf[0, half:, :], preferred_element_type=F32) + bg_ref[0])
        up = (jnp.dot(x_hi, w_bf[1, :half, :], preferred_element_type=F32)
              + jnp.dot(x_lo, w_bf[1, half:, :], preferred_element_type=F32) + bu_ref[0])
        gate = jnp.minimum(gate, SWIGLU_LIMIT)
        up = jnp.clip(up, -SWIGLU_LIMIT, SWIGLU_LIMIT)
        act = (up + 1.0) * gate * (1.0 / (1.0 + jnp.exp(-SWIGLU_ALPHA * gate)))
        o_ref[...] = act.astype(o_ref.dtype)

    @pl.when(i >= nv_ref[0])
    def _():
        o_ref[...] = jnp.zeros_like(o_ref)


def _expert_scratch(k, tn):
    return [pltpu.VMEM((2, k, tn), F32),
            pltpu.VMEM((2, k, tn), BF16),
            pltpu.SemaphoreType.DMA((2,))]


def _expert_up(xs, meta, w_gu, b_gu, *, tm):
    n_rows, dw = xs.shape
    n_exp, d, ff2 = w_gu.shape
    ff = ff2 // 2
    assert d == 2 * dw
    tn = _tile(ff, 512)
    nj = ff // tn
    n_blocks = n_rows // tm
    b3 = b_gu.reshape(n_exp, 1, ff2)
    grid_spec = pltpu.PrefetchScalarGridSpec(
        num_scalar_prefetch=4,
        grid=(nj, n_blocks),
        in_specs=[
            pl.BlockSpec((tm, dw), lambda j, i, be, nv, *_: (jnp.minimum(i, nv[0] - 1), 0)),
            pl.BlockSpec(memory_space=pl.ANY),
            pl.BlockSpec((1, 1, tn), lambda j, i, be, *_: (be[i], 0, j)),
            pl.BlockSpec((1, 1, tn), lambda j, i, be, *_: (be[i], 0, nj + j)),
        ],
        out_specs=pl.BlockSpec((tm, tn), lambda j, i, *_: (i, j)),
        scratch_shapes=_expert_scratch(d, tn),
    )
    return pl.pallas_call(
        functools.partial(_expert_up_kernel, ff=ff, tn=tn, nj=nj),
        grid_spec=grid_spec,
        out_shape=jax.ShapeDtypeStruct((n_rows, ff), BF16),
        compiler_params=_params("arbitrary", "arbitrary"),
        name="expert_up",
    )(*meta, xs, w_gu, b3, b3)


def _expert_down_kernel(be_ref, nv_ref, chg_ref, nxt_ref, h_ref, w_hbm, ba_ref, bb_ref, o_ref,
                        stage, w_bf, sems, *, half, tn, nj):
    i = pl.program_id(1)
    _stream_expert_weights(be_ref, nv_ref, chg_ref, nxt_ref, w_hbm, stage, sems, w_bf,
                           half_stride=half, tn=tn, nj=nj)

    @pl.when(i < nv_ref[0])
    def _():
        h = h_ref[...]
        ya = jnp.dot(h, w_bf[0], preferred_element_type=F32) + ba_ref[0]
        yb = jnp.dot(h, w_bf[1], preferred_element_type=F32) + bb_ref[0]
        o_ref[...] = _pack_bf16_pairs(ya, yb)

    @pl.when(i >= nv_ref[0])
    def _():
        o_ref[...] = jnp.zeros_like(o_ref)


def _expert_down(hs, meta, w_dn, b_dn, *, tm):
    n_rows, ff = hs.shape
    n_exp, _, d = w_dn.shape
    half = d // 2
    tn = _tile(half, 1024)
    nj = half // tn
    n_blocks = n_rows // tm
    b3 = b_dn.reshape(n_exp, 1, d)
    grid_spec = pltpu.PrefetchScalarGridSpec(
        num_scalar_prefetch=4,
        grid=(nj, n_blocks),
        in_specs=[
            pl.BlockSpec((tm, ff), lambda j, i, be, nv, *_: (jnp.minimum(i, nv[0] - 1), 0)),
            pl.BlockSpec(memory_space=pl.ANY),
            pl.BlockSpec((1, 1, tn), lambda j, i, be, *_: (be[i], 0, j)),
            pl.BlockSpec((1, 1, tn), lambda j, i, be, *_: (be[i], 0, nj + j)),
        ],
        out_specs=pl.BlockSpec((tm, tn), lambda j, i, *_: (i, j)),
        scratch_shapes=_expert_scratch(ff, tn),
    )
    return pl.pallas_call(
        functools.partial(_expert_down_kernel, half=half, tn=tn, nj=nj),
        grid_spec=grid_spec,
        out_shape=jax.ShapeDtypeStruct((n_rows, half), jnp.uint32),
        compiler_params=_params("arbitrary", "arbitrary"),
        name="expert_down",
    )(*meta, hs, w_dn, b3, b3)


def _combine_kernel(dest_ref, x_ref, gate_ref, g_ref, ys_hbm, o_ref, buf, sems, *, tb, n_steps):
    s = pl.program_id(0)
    cur = s & 1

    def row_copy(row, buf_slot, k, t):
        return pltpu.make_async_copy(ys_hbm.at[pl.ds(row, 1), :],
                                     buf.at[buf_slot, k, pl.ds(t, 1), :], sems.at[buf_slot])

    def gather(step, buf_slot):
        def issue(group, carry):
            t0 = pl.multiple_of(group * ROW_GROUP, ROW_GROUP)
            for u in range(ROW_GROUP):
                tok = step * tb + t0 + u
                for k in range(TOP_K):
                    row_copy(dest_ref[tok * TOP_K + k], buf_slot, k, t0 + u).start(
                        priority=k % 2)
            return carry

        lax.fori_loop(0, tb // ROW_GROUP, issue, 0)

    @pl.when(s == 0)
    def _():
        gather(0, 0)

    @pl.when(s + 1 < n_steps)
    def _():
        gather(s + 1, 1 - cur)

    def drain(t, carry):
        for k in range(TOP_K):
            row_copy(0, cur, k, t).wait()
        return carry

    lax.fori_loop(0, tb, drain, 0)

    half = x_ref.shape[-1] // 2
    acc_hi = x_ref[:, :half]
    acc_lo = x_ref[:, half:]
    for k in range(TOP_K):
        y_hi, y_lo = _unpack_pairs_f32(buf[cur, k])
        gate = gate_ref[:, k:k + 1]
        acc_hi = acc_hi + gate * y_hi
        acc_lo = acc_lo + gate * y_lo
    ms = (jnp.sum(acc_hi * acc_hi, axis=-1, keepdims=True)
          + jnp.sum(acc_lo * acc_lo, axis=-1, keepdims=True)) / (2 * half)
    r = lax.rsqrt(ms + NORM_EPS)
    o_ref[:, :half] = acc_hi * r * g_ref[:, :half]
    o_ref[:, half:] = acc_lo * r * g_ref[:, half:]


def _combine(x, gates, dest, ys, g):
    n_tok, d = x.shape
    tb = _tile(n_tok, 128)
    n_steps = n_tok // tb
    grid_spec = pltpu.PrefetchScalarGridSpec(
        num_scalar_prefetch=1,
        grid=(n_steps,),
        in_specs=[pl.BlockSpec((tb, d), lambda i, dest: (i, 0)),
                  pl.BlockSpec((tb, TOP_K), lambda i, dest: (i, 0)),
                  pl.BlockSpec((1, d), lambda i, dest: (0, 0)),
                  pl.BlockSpec(memory_space=pl.ANY)],
        out_specs=pl.BlockSpec((tb, d), lambda i, dest: (i, 0)),
        scratch_shapes=[pltpu.VMEM((2, TOP_K, tb, d // 2), ys.dtype),
                        pltpu.SemaphoreType.DMA((2,))],
    )
    return pl.pallas_call(
        functools.partial(_combine_kernel, tb=tb, n_steps=n_steps),
        grid_spec=grid_spec,
        out_shape=jax.ShapeDtypeStruct((n_tok, d), F32),
        compiler_params=_params("arbitrary"),
        name="combine",
    )(dest, x, gates, g.reshape(1, d), ys)


def _routing_tables(top_idx, rank, counts, tm):
    n_exp = counts.shape[0]
    flat_e = top_idx.reshape(-1)
    n_pairs = flat_e.shape[0]
    pcounts = (counts + tm - 1) // tm * tm
    pends = jnp.cumsum(pcounts)
    pstarts = pends - pcounts
    onehot = flat_e[:, None] == jnp.arange(n_exp, dtype=jnp.int32)[None, :]
    dest = jnp.sum(jnp.where(onehot, pstarts[None, :], 0), axis=1) + rank.reshape(-1)
    assert n_pairs % tm == 0
    n_blocks = n_pairs // tm + n_exp
    block_id = jnp.arange(n_blocks, dtype=jnp.int32)
    block_e = jnp.minimum(jnp.sum(pends[None, :] <= (block_id * tm)[:, None], axis=1),
                          n_exp - 1).astype(jnp.int32)
    n_valid = (pends[-1] // tm).astype(jnp.int32)
    prev_e = jnp.concatenate([jnp.full((1,), -1, jnp.int32), block_e[:-1]])
    is_change = (block_id < n_valid) & (block_e != prev_e)
    change_ord = jnp.cumsum(is_change.astype(jnp.int32))
    change = jnp.where(is_change, change_ord, 0).astype(jnp.int32)
    later = is_change[None, :] & (block_id[None, :] > block_id[:, None])
    next_e = jnp.where(jnp.any(later, axis=1), block_e[jnp.argmax(later, axis=1)], -1)
    meta = (block_e, n_valid.reshape(1), change, next_e.astype(jnp.int32))
    return (dest.astype(jnp.int32), pends.astype(jnp.int32), pcounts.astype(jnp.int32),
            meta, n_blocks * tm)


def kernel(x, attn_norm_g, w_in, lambda_q1, lambda_k1, lambda_q2, lambda_k2, diff_subln_g,
           sgu_ln_g, sgu_ln_b, sgu_w, sgu_b, rel_bias, w_out, ffn_norm_g, router_w, router_b,
           w_gate_up, b_gate_up, w_down, b_down, final_norm_g):
    batch, seq, d = x.shape
    n_tok = batch * seq
    hd = lambda_q1.shape[-1]
    n_maps = rel_bias.shape[1]
    n_heads = n_maps // 2
    q_cols = n_maps * hd
    attn_width = n_heads * diff_subln_g.shape[-1]
    n_exp = w_gate_up.shape[1]
    tq = _tile(seq, 512)
    tk = tq // 2
    tm = 256

    xt = x.reshape(n_tok, d)
    h = _rmsnorm_rows(xt, attn_norm_g[0], BF16)
    proj = _in_projection(h, w_in[0].astype(BF16))

    lam = (jnp.exp(jnp.sum(lambda_q1[0].astype(F32) * lambda_k1[0].astype(F32)))
           - jnp.exp(jnp.sum(lambda_q2[0].astype(F32) * lambda_k2[0].astype(F32)))
           + LAMBDA_INIT).reshape(1)
    bias = _bias_tiles(rel_bias, tq)
    far_bias = rel_bias[-1].astype(F32)
    att = _diff_attention(proj, bias, far_bias, lam, diff_subln_g[0],
                          batch=batch, seq=seq, n_heads=n_heads, hd=hd, tq=tq, tk=tk)
    sgu = _spatial_gating(proj, sgu_ln_g[0], sgu_ln_b[0], sgu_w[0], sgu_b[0],
                          u_col0=2 * q_cols + attn_width)
    x1 = _out_projection(att, sgu, w_out[0].astype(BF16), xt)

    hp, top_idx, gates, rank, counts = _router(x1, ffn_norm_g[0], router_w[0], router_b[0])
    dest, pends, pcounts, meta, n_rows = _routing_tables(
        top_idx, rank, counts[0].astype(jnp.int32), tm)
    xs = _dispatch(hp, dest, pends, pcounts, n_rows=n_rows, tm=tm)
    hs = _expert_up(xs, meta, w_gate_up[0], b_gate_up[0], tm=tm)
    ys = _expert_down(hs, meta, w_down[0], b_down[0], tm=tm)
    out = _combine(x1, gates, dest, ys, final_norm_g)
    return out.reshape(batch, seq, d)
```

```python
import functools
import math

import jax
import jax.numpy as jnp
from jax import lax
from jax.experimental import pallas as pl
from jax.experimental.pallas import tpu as pltpu

F32 = jnp.float32
BF16 = jnp.bfloat16

TOP_K = 4
MAX_DISTANCE = 128
SWIGLU_LIMIT = 7.0
SWIGLU_ALPHA = 1.702
NORM_EPS = 1e-5
LAMBDA_INIT = 0.8 - 0.6 * math.exp(0.0)
MASK_VALUE = -1e30
LOG2E = math.log2(math.e)

LANES = 128
BF16_SUBLANES = 16
ROW_GROUP = 8
VMEM_LIMIT_BYTES = 56 * 1024 * 1024


def _tile(dim, want):
    t = min(dim, want)
    while dim % t:
        t -= LANES
    assert t > 0, (dim, want)
    return t


def _params(*semantics):
    return pltpu.CompilerParams(dimension_semantics=semantics,
                                vmem_limit_bytes=VMEM_LIMIT_BYTES)


def _rmsnorm_kernel(x_ref, g_ref, o_ref):
    x = x_ref[...]
    ms = jnp.mean(x * x, axis=-1, keepdims=True)
    o_ref[...] = (x * lax.rsqrt(ms + NORM_EPS) * g_ref[...]).astype(o_ref.dtype)


def _rmsnorm_rows(x, g, out_dtype):
    n, d = x.shape
    tr = _tile(n, 256)
    return pl.pallas_call(
        _rmsnorm_kernel,
        grid=(n // tr,),
        in_specs=[pl.BlockSpec((tr, d), lambda i: (i, 0)),
                  pl.BlockSpec((1, d), lambda i: (0, 0))],
        out_specs=pl.BlockSpec((tr, d), lambda i: (i, 0)),
        out_shape=jax.ShapeDtypeStruct((n, d), out_dtype),
        compiler_params=_params("parallel"),
        name="rmsnorm_rows",
    )(x, g.reshape(1, d))


def _in_projection_kernel(a_ref, b_ref, o_ref):
    acc = jnp.dot(a_ref[...], b_ref[...], preferred_element_type=F32)
    for c in range(o_ref.shape[0]):
        o_ref[c] = acc[:, c * LANES:(c + 1) * LANES].astype(o_ref.dtype)


def _in_projection(h, w):
    m, k = h.shape
    n = w.shape[1]
    tm, tn = _tile(m, 1024), _tile(n, 1024)
    return pl.pallas_call(
        _in_projection_kernel,
        grid=(m // tm, n // tn),
        in_specs=[pl.BlockSpec((tm, k), lambda i, j: (i, 0)),
                  pl.BlockSpec((k, tn), lambda i, j: (0, j))],
        out_specs=pl.BlockSpec((tn // LANES, tm, LANES), lambda i, j: (j, i, 0)),
        out_shape=jax.ShapeDtypeStruct((n // LANES, m, LANES), BF16),
        compiler_params=_params("parallel", "parallel"),
        name="in_projection",
    )(h, w)


def _out_projection_kernel(a_ref, s_ref, wa_ref, ws_ref, x_ref, o_ref):
    acc = jnp.dot(a_ref[...], wa_ref[...], preferred_element_type=F32)
    acc += jnp.dot(s_ref[...], ws_ref[...], preferred_element_type=F32)
    o_ref[...] = x_ref[...] + acc


def _out_projection(att, sgu, w, x):
    m, ka = att.shape
    ks = sgu.shape[1]
    n = w.shape[1]
    assert ka == ks and w.shape[0] == ka + ks
    tm, tn = _tile(m, 1024), _tile(n, 512)
    return pl.pallas_call(
        _out_projection_kernel,
        grid=(m // tm, n // tn),
        in_specs=[pl.BlockSpec((tm, ka), lambda i, j: (i, 0)),
                  pl.BlockSpec((tm, ks), lambda i, j: (i, 0)),
                  pl.BlockSpec((ka, tn), lambda i, j: (0, j)),
                  pl.BlockSpec((ks, tn), lambda i, j: (1, j)),
                  pl.BlockSpec((tm, tn), lambda i, j: (i, j))],
        out_specs=pl.BlockSpec((tm, tn), lambda i, j: (i, j)),
        out_shape=jax.ShapeDtypeStruct((m, n), F32),
        compiler_params=_params("parallel", "parallel"),
        name="out_projection",
    )(att, sgu, w, w, x)


def _bias_tiles_kernel(tbl_ref, o_ref, *, tq, n_buckets):
    h = pl.program_id(0)
    max_exact = n_buckets // 2
    nb = tq // LANES
    key = lax.broadcasted_iota(jnp.int32, (LANES, LANES), 0)
    qry = lax.broadcasted_iota(jnp.int32, (LANES, LANES), 1)

    def lookup(dist, m):
        d = jnp.maximum(dist, 1).astype(F32)
        large = max_exact + (jnp.log(d / max_exact) / math.log(MAX_DISTANCE / max_exact)
                             * (n_buckets - max_exact)).astype(jnp.int32)
        large = jnp.minimum(large, n_buckets - 1)
        bucket = jnp.where(dist < max_exact, dist, large)
        val = jnp.zeros(dist.shape, F32)
        for b in range(n_buckets):
            val = jnp.where(bucket == b, tbl_ref[b, 2 * h + m], val)
        return jnp.where(dist >= 0, val * LOG2E, MASK_VALUE)

    for w in range(2):
        for delta in range(-(nb - 1), nb):
            offset = delta * LANES + w * tq
            for m in range(2):
                if offset + LANES - 1 < 0:
                    block = jnp.full((LANES, LANES), MASK_VALUE, F32)
                elif offset - (LANES - 1) >= MAX_DISTANCE:
                    block = jnp.full((LANES, LANES), tbl_ref[n_buckets - 1, 2 * h + m], F32) * LOG2E
                else:
                    block = lookup(qry - key + offset, m)
                for kc in range(max(0, -delta), min(nb, nb - delta)):
                    qc = kc + delta
                    o_ref[0, w, kc * LANES:(kc + 1) * LANES,
                          m * tq + qc * LANES:m * tq + (qc + 1) * LANES] = block


def _bias_tiles(rel_bias, tq):
    n_buckets, n_maps = rel_bias.shape
    n_heads = n_maps // 2
    assert tq >= MAX_DISTANCE
    return pl.pallas_call(
        functools.partial(_bias_tiles_kernel, tq=tq, n_buckets=n_buckets),
        grid=(n_heads,),
        in_specs=[pl.BlockSpec(memory_space=pltpu.SMEM)],
        out_specs=pl.BlockSpec((1, 2, tq, 2 * tq), lambda h: (h, 0, 0, 0)),
        out_shape=jax.ShapeDtypeStruct((n_heads, 2, tq, 2 * tq), F32),
        compiler_params=_params("parallel"),
        name="bias_tiles",
    )(rel_bias.astype(F32))


def _attention_kernel(far_ref, lam_ref, q_ref, k_ref, v_ref, bias_ref, g_ref, o_ref,
                      vt_ref, sa_ref, sb_ref, acc_ref, *, tq, tk, hd):
    h = pl.program_id(1)
    i = pl.program_id(2)
    _, seq, vd = v_ref.shape
    sub = tq // tk
    assert sub == 2

    @pl.when(i == 0)
    def _():
        ones_row = lax.broadcasted_iota(jnp.int32, (BF16_SUBLANES, tk), 0) == 0
        for c in range(seq // tk):
            vt_ref[c, :vd, :] = v_ref[0, c * tk:(c + 1) * tk, :].astype(F32).T.astype(BF16)
            vt_ref[c, vd:, :] = jnp.where(ones_row, 1.0, 0.0).astype(BF16)

    qt = (q_ref[0].astype(F32) * (hd ** -0.5 * LOG2E)).T
    dim = lax.broadcasted_iota(jnp.int32, qt.shape, 0)
    qqt = jnp.concatenate([jnp.where(dim < hd, qt, 0.0),
                           jnp.where(dim >= hd, qt, 0.0)], axis=1).astype(BF16)
    col = lax.broadcasted_iota(jnp.int32, (1, 2 * tq), 1)
    far = jnp.where(col < tq, far_ref[2 * h], far_ref[2 * h + 1]) * LOG2E

    def scores(kb):
        kblk = k_ref[0, pl.ds(pl.multiple_of(kb * tk, tk), tk), :]
        return jnp.dot(kblk, qqt, preferred_element_type=F32)

    def online_update(s, kb, m_prev, shift):
        m_new = jnp.maximum(m_prev, jnp.max(s, axis=0, keepdims=True) + shift)
        alpha = jnp.exp2(m_prev - m_new)
        p = jnp.exp2((s - (m_new - shift)).astype(BF16))
        acc_ref[...] = alpha * acc_ref[...] + jnp.dot(vt_ref[kb], p,
                                                      preferred_element_type=F32)
        return m_new

    kb_prev = sub * jnp.maximum(i - 1, 0)
    no_prev = jnp.where(i >= 1, 0.0, MASK_VALUE)
    n_pairs = jnp.maximum(i - 1, 0)
    last = jnp.maximum(sub * n_pairs - 1, 0)
    acc_ref[...] = jnp.zeros_like(acc_ref)
    m = jnp.full((1, 2 * tq), MASK_VALUE, F32)

    sa_ref[...] = scores(sub * i) + bias_ref[0, 0, :tk, :]
    sb_ref[...] = scores(sub * i + 1) + bias_ref[0, 0, tk:, :]
    m = online_update(sa_ref[...], sub * i, m, 0.0)
    sa_ref[...] = scores(kb_prev) + (bias_ref[0, 1, :tk, :] + no_prev)
    m = online_update(sb_ref[...], sub * i + 1, m, 0.0)
    sb_ref[...] = scores(kb_prev + 1) + (bias_ref[0, 1, tk:, :] + no_prev)
    m = online_update(sa_ref[...], kb_prev, m, 0.0)
    sa_ref[...] = scores(0)
    m = online_update(sb_ref[...], kb_prev + 1, m, 0.0)

    def far_pair(n, m_run):
        kb = sub * n
        sb_ref[...] = scores(kb + 1)
        m_run = online_update(sa_ref[...], kb, m_run, far)
        sa_ref[...] = scores(jnp.minimum(kb + 2, last))
        return online_update(sb_ref[...], kb + 1, m_run, far)

    lax.fori_loop(0, n_pairs, far_pair, m)

    acc = acc_ref[...]
    o = acc[:vd] * (1.0 / acc[vd:vd + 1])
    a = o[:, :tq] - lam_ref[0] * o[:, tq:]
    ms = jnp.mean(a * a, axis=0, keepdims=True)
    y = a * lax.rsqrt(ms + NORM_EPS) * g_ref[...] * (1.0 - LAMBDA_INIT)
    o_ref[...] = y.T.astype(o_ref.dtype)


def _diff_attention(proj, bias, far_bias, lam, subln_g, *, batch, seq, n_heads, hd, tq, tk):
    vd = subln_g.shape[-1]
    assert 2 * hd == LANES and vd == LANES
    nq = seq // tq
    k_col0 = n_heads
    v_col0 = 2 * n_heads
    grid_spec = pltpu.PrefetchScalarGridSpec(
        num_scalar_prefetch=2,
        grid=(batch, n_heads, nq),
        in_specs=[
            pl.BlockSpec((1, tq, LANES), lambda b, h, i, *_: (h, b * nq + i, 0)),
            pl.BlockSpec((1, seq, LANES), lambda b, h, i, *_: (k_col0 + h, b, 0)),
            pl.BlockSpec((1, seq, LANES), lambda b, h, i, *_: (v_col0 + h, b, 0)),
            pl.BlockSpec((1, 2, tq, 2 * tq), lambda b, h, i, *_: (h, 0, 0, 0)),
            pl.BlockSpec((vd, 1), lambda b, h, i, *_: (0, 0)),
        ],
        out_specs=pl.BlockSpec((tq, vd), lambda b, h, i, *_: (b * nq + i, h)),
        scratch_shapes=[pltpu.VMEM((seq // tk, vd + BF16_SUBLANES, tk), BF16),
                        pltpu.VMEM((tk, 2 * tq), F32),
                        pltpu.VMEM((tk, 2 * tq), F32),
                        pltpu.VMEM((vd + BF16_SUBLANES, 2 * tq), F32)],
    )
    return pl.pallas_call(
        functools.partial(_attention_kernel, tq=tq, tk=tk, hd=hd),
        grid_spec=grid_spec,
        out_shape=jax.ShapeDtypeStruct((batch * seq, n_heads * vd), BF16),
        compiler_params=_params("parallel", "parallel", "arbitrary"),
        name="diff_attention",
    )(far_bias, lam, proj, proj, proj, bias, subln_g.reshape(vd, 1))


def _gelu(x):
    return 0.5 * x * (1.0 + lax.erf(x * math.sqrt(0.5)))


def _sgu_kernel(u_ref, v_ref, lng_ref, lnb_ref, w_ref, bt_ref, o_ref):
    n_heads, chunk, hdim = u_ref.shape
    width = n_heads * hdim
    u = _gelu(u_ref[...].astype(F32))
    v = _gelu(v_ref[...].astype(F32))
    mu = jnp.sum(jnp.sum(v, axis=0), axis=-1, keepdims=True) / width
    vc = v - mu
    var = jnp.sum(jnp.sum(vc * vc, axis=0), axis=-1, keepdims=True) / width
    vn = (vc * lax.rsqrt(var + NORM_EPS) * lng_ref[...] + lnb_ref[...]).astype(BF16)
    r = lax.broadcasted_iota(jnp.int32, (chunk, chunk), 0)
    c = lax.broadcasted_iota(jnp.int32, (chunk, chunk), 1)
    causal = r >= c
    for hh in range(n_heads):
        w = jnp.where(causal, w_ref[hh], 0.0).astype(BF16)
        y = jnp.dot(w, vn[hh], preferred_element_type=F32) + bt_ref[:, hh:hh + 1]
        o_ref[:, hh * hdim:(hh + 1) * hdim] = (u[hh] * y).astype(o_ref.dtype)


def _spatial_gating(proj, ln_g, ln_b, w_s, b_s, *, u_col0):
    _, n_tok, hdim = proj.shape
    n_heads, chunk, _ = w_s.shape
    width = ln_g.shape[-1]
    assert width == n_heads * hdim and u_col0 % width == 0
    ub = u_col0 // width
    return pl.pallas_call(
        _sgu_kernel,
        grid=(n_tok // chunk,),
        in_specs=[pl.BlockSpec((n_heads, chunk, hdim), lambda i: (ub, i, 0)),
                  pl.BlockSpec((n_heads, chunk, hdim), lambda i: (ub + 1, i, 0)),
                  pl.BlockSpec((n_heads, 1, hdim), lambda i: (0, 0, 0)),
                  pl.BlockSpec((n_heads, 1, hdim), lambda i: (0, 0, 0)),
                  pl.BlockSpec((n_heads, chunk, chunk), lambda i: (0, 0, 0)),
                  pl.BlockSpec((chunk, n_heads), lambda i: (0, 0))],
        out_specs=pl.BlockSpec((chunk, width), lambda i: (i, 0)),
        out_shape=jax.ShapeDtypeStruct((n_tok, width), BF16),
        compiler_params=_params("parallel"),
        name="spatial_gating",
    )(proj, proj, ln_g.reshape(n_heads, 1, hdim), ln_b.reshape(n_heads, 1, hdim), w_s, b_s.T)


def _split_bf16(x):
    hi = x.astype(BF16)
    return hi, (x - hi.astype(F32)).astype(BF16)


def _router_kernel(x_ref, g_ref, rw_ref, rb_ref, hp_ref, idx_ref, gate_ref, rank_ref, cnt_ref):
    @pl.when(pl.program_id(0) == 0)
    def _():
        cnt_ref[...] = jnp.zeros_like(cnt_ref)

    x = x_ref[...]
    ms = jnp.mean(x * x, axis=-1, keepdims=True)
    h = x * lax.rsqrt(ms + NORM_EPS) * g_ref[...]
    half = h.shape[-1] // 2
    hp_ref[...] = _pack_bf16_pairs(h[:, :half], h[:, half:])

    h_hi, h_lo = _split_bf16(h)
    w_hi, w_lo = _split_bf16(rw_ref[...])
    logits = (jnp.dot(h_hi, w_hi, preferred_element_type=F32)
              + jnp.dot(h_lo, w_hi, preferred_element_type=F32)
              + jnp.dot(h_hi, w_lo, preferred_element_type=F32)) + rb_ref[...]
    tr, n_exp = logits.shape
    lane = lax.broadcasted_iota(jnp.int32, logits.shape, 1)
    vals, idxs = [], []
    for _ in range(TOP_K):
        top = jnp.max(logits, axis=-1, keepdims=True)
        idx = jnp.min(jnp.where(logits == top, lane, n_exp), axis=-1, keepdims=True)
        vals.append(top)
        idxs.append(idx)
        logits = jnp.where(lane == idx, -jnp.inf, logits)
    exps = [jnp.exp(v - vals[0]) for v in vals]
    denom = exps[0]
    for e in exps[1:]:
        denom = denom + e

    earlier = (lax.broadcasted_iota(jnp.int32, (tr, tr), 1)
               < lax.broadcasted_iota(jnp.int32, (tr, tr), 0))
    earlier = jnp.where(earlier, 1.0, 0.0).astype(BF16)
    count = cnt_ref[...]
    for k in range(TOP_K):
        chosen = lane == idxs[k]
        onehot = jnp.where(chosen, 1.0, 0.0)
        before = jnp.dot(earlier, onehot.astype(BF16), preferred_element_type=F32) + count
        rank = jnp.sum(jnp.where(chosen, before, 0.0), axis=-1, keepdims=True)
        idx_ref[:, k:k + 1] = idxs[k]
        gate_ref[:, k:k + 1] = exps[k] / denom
        rank_ref[:, k:k + 1] = rank.astype(jnp.int32)
        count = count + jnp.sum(onehot, axis=0, keepdims=True)
    cnt_ref[...] = count


def _router(x, g, rw, rb):
    n, d = x.shape
    n_exp = rw.shape[1]
    tr = _tile(n, 256)
    assert n * TOP_K < 2 ** 24
    return pl.pallas_call(
        _router_kernel,
        grid=(n // tr,),
        in_specs=[pl.BlockSpec((tr, d), lambda i: (i, 0)),
                  pl.BlockSpec((1, d), lambda i: (0, 0)),
                  pl.BlockSpec((d, n_exp), lambda i: (0, 0)),
                  pl.BlockSpec((1, n_exp), lambda i: (0, 0))],
        out_specs=[pl.BlockSpec((tr, d // 2), lambda i: (i, 0)),
                   pl.BlockSpec((tr, TOP_K), lambda i: (i, 0)),
                   pl.BlockSpec((tr, TOP_K), lambda i: (i, 0)),
                   pl.BlockSpec((tr, TOP_K), lambda i: (i, 0)),
                   pl.BlockSpec((1, n_exp), lambda i: (0, 0))],
        out_shape=[jax.ShapeDtypeStruct((n, d // 2), jnp.uint32),
                   jax.ShapeDtypeStruct((n, TOP_K), jnp.int32),
                   jax.ShapeDtypeStruct((n, TOP_K), F32),
                   jax.ShapeDtypeStruct((n, TOP_K), jnp.int32),
                   jax.ShapeDtypeStruct((1, n_exp), F32)],
        compiler_params=_params("arbitrary"),
        name="router",
    )(x, g.reshape(1, d), rw, rb.reshape(1, n_exp))


def _dispatch_kernel(dest_ref, pends_ref, pcnt_ref, h_ref, xs_hbm, zbuf, zsem, sem,
                     *, tb, tm, n_exp):
    s = pl.program_id(0)

    n_rows = xs_hbm.shape[0]

    def zero_block(start):
        return pltpu.make_async_copy(
            zbuf, xs_hbm.at[pl.ds(pl.multiple_of(start, tm), tm), :], zsem)

    def zero_fill(action):
        for e in range(n_exp):
            @pl.when(pcnt_ref[e] > 0)
            def _():
                action(zero_block(pends_ref[e] - tm))
        for b in range(n_exp):
            start = pends_ref[n_exp - 1] + b * tm

            @pl.when(start < n_rows)
            def _():
                action(zero_block(start))

    @pl.when(s == 0)
    def _():
        zbuf[...] = jnp.zeros_like(zbuf)
        zero_fill(lambda copy: copy.start())
        zero_fill(lambda copy: copy.wait())

    def row_copy(t, slot):
        return pltpu.make_async_copy(h_ref.at[pl.ds(t, 1), :],
                                     xs_hbm.at[pl.ds(slot, 1), :], sem)

    def issue(group, carry):
        t0 = pl.multiple_of(group * ROW_GROUP, ROW_GROUP)
        for u in range(ROW_GROUP):
            tok = s * tb + t0 + u
            for k in range(TOP_K):
                row_copy(t0 + u, dest_ref[tok * TOP_K + k]).start(priority=k % 2)
        return carry

    lax.fori_loop(0, tb // ROW_GROUP, issue, 0)

    def drain(t, carry):
        for k in range(TOP_K):
            row_copy(0, 0).wait()
        return carry

    lax.fori_loop(0, tb, drain, 0)


def _dispatch(hp, dest, pends, pcounts, *, n_rows, tm):
    n_tok, dw = hp.shape
    n_exp = pends.shape[0]
    tb = _tile(n_tok, 128)
    grid_spec = pltpu.PrefetchScalarGridSpec(
        num_scalar_prefetch=3,
        grid=(n_tok // tb,),
        in_specs=[pl.BlockSpec((tb, dw), lambda i, *_: (i, 0))],
        out_specs=pl.BlockSpec(memory_space=pl.ANY),
        scratch_shapes=[pltpu.VMEM((tm, dw), hp.dtype),
                        pltpu.SemaphoreType.DMA(()),
                        pltpu.SemaphoreType.DMA(())],
    )
    return pl.pallas_call(
        functools.partial(_dispatch_kernel, tb=tb, tm=tm, n_exp=n_exp),
        grid_spec=grid_spec,
        out_shape=jax.ShapeDtypeStruct((n_rows, dw), hp.dtype),
        compiler_params=_params("arbitrary"),
        name="dispatch",
    )(dest, pends, pcounts, hp)


def _pack_bf16_pairs(hi, lo):
    hb = pltpu.bitcast(hi.astype(BF16).astype(F32), jnp.uint32)
    lb = pltpu.bitcast(lo.astype(BF16).astype(F32), jnp.uint32)
    return hb | (lb >> 16)


def _unpack_pairs_f32(p):
    return (pltpu.bitcast(p & jnp.uint32(0xFFFF0000), F32), pltpu.bitcast(p << 16, F32))


def _stream_expert_weights(be_ref, nv_ref, chg_ref, nxt_ref, w_hbm, stage, sems, w_bf,
                           *, half_stride, tn, nj):
    j = pl.program_id(0)
    i = pl.program_id(1)

    def copies(e, jj):
        return [pltpu.make_async_copy(
            w_hbm.at[e, :, pl.ds(pl.multiple_of(part * half_stride + jj * tn, tn), tn)],
            stage.at[part], sems.at[part]) for part in range(2)]

    change = chg_ref[i]

    @pl.when(change > 0)
    def _():
        @pl.when(jnp.logical_and(j == 0, change == 1))
        def _():
            for cp in copies(be_ref[i], j):
                cp.start()

        for part, cp in enumerate(copies(be_ref[i], j)):
            cp.wait()
            w_bf[part] = stage[part].astype(BF16)

        last = nxt_ref[i] < 0
        next_e = jnp.where(last, be_ref[0], nxt_ref[i])
        next_j = jnp.where(last, j + 1, j)

        @pl.when(next_j < nj)
        def _():
            for cp in copies(next_e, next_j):
                cp.start(priority=1)


def _expert_up_kernel(be_ref, nv_ref, chg_ref, nxt_ref, xs_ref, w_hbm, bg_ref, bu_ref, o_ref,
                      stage, w_bf, sems, *, ff, tn, nj):
    i = pl.program_id(1)
    _stream_expert_weights(be_ref, nv_ref, chg_ref, nxt_ref, w_hbm, stage, sems, w_bf,
                           half_stride=ff, tn=tn, nj=nj)

    @pl.when(i < nv_ref[0])
    def _():
        x_hi, x_lo = _unpack_pairs_f32(xs_ref[...])
        x_hi, x_lo = x_hi.astype(BF16), x_lo.astype(BF16)
        half = x_hi.shape[-1]
        gate = (jnp.dot(x_hi, w_bf[0, :half, :], preferred_element_type=F32)
                + jnp.dot(x_lo, w_bf[0, half:, :], preferred_element_type=F32) + bg_ref[0])
        up = (jnp.dot(x_hi, w_bf[1, :half, :], preferred_element_type=F32)
              + jnp.dot(x_lo, w_bf[1, half:, :], preferred_element_type=F32) + bu_ref[0])
        gate = jnp.minimum(gate, SWIGLU_LIMIT)
        up = jnp.clip(up, -SWIGLU_LIMIT, SWIGLU_LIMIT)
        act = (up + 1.0) * gate * (1.0 / (1.0 + jnp.exp(-SWIGLU_ALPHA * gate)))
        o_ref[...] = act.astype(o_ref.dtype)

    @pl.when(i >= nv_ref[0])
    def _():
        o_ref[...] = jnp.zeros_like(o_ref)


def _expert_scratch(k, tn):
    return [pltpu.VMEM((2, k, tn), F32),
            pltpu.VMEM((2, k, tn), BF16),
            pltpu.SemaphoreType.DMA((2,))]


def _expert_up(xs, meta, w_gu, b_gu, *, tm):
    n_rows, dw = xs.shape
    n_exp, d, ff2 = w_gu.shape
    ff = ff2 // 2
    assert d == 2 * dw
    tn = _tile(ff, 512)
    nj = ff // tn
    n_blocks = n_rows // tm
    b3 = b_gu.reshape(n_exp, 1, ff2)
    grid_spec = pltpu.PrefetchScalarGridSpec(
        num_scalar_prefetch=4,
        grid=(nj, n_blocks),
        in_specs=[
            pl.BlockSpec((tm, dw), lambda j, i, be, nv, *_: (jnp.minimum(i, nv[0] - 1), 0)),
            pl.BlockSpec(memory_space=pl.ANY),
            pl.BlockSpec((1, 1, tn), lambda j, i, be, *_: (be[i], 0, j)),
            pl.BlockSpec((1, 1, tn), lambda j, i, be, *_: (be[i], 0, nj + j)),
        ],
        out_specs=pl.BlockSpec((tm, tn), lambda j, i, *_: (i, j)),
        scratch_shapes=_expert_scratch(d, tn),
    )
    return pl.pallas_call(
        functools.partial(_expert_up_kernel, ff=ff, tn=tn, nj=nj),
        grid_spec=grid_spec,
        out_shape=jax.ShapeDtypeStruct((n_rows, ff), BF16),
        compiler_params=_params("arbitrary", "arbitrary"),
        name="expert_up",
    )(*meta, xs, w_gu, b3, b3)


def _expert_down_kernel(be_ref, nv_ref, chg_ref, nxt_ref, h_ref, w_hbm, ba_ref, bb_ref, o_ref,
                        stage, w_bf, sems, *, half, tn, nj):
    i = pl.program_id(1)
    _stream_expert_weights(be_ref, nv_ref, chg_ref, nxt_ref, w_hbm, stage, sems, w_bf,
                           half_stride=half, tn=tn, nj=nj)

    @pl.when(i < nv_ref[0])
    def _():
        h = h_ref[...]
        ya = jnp.dot(h, w_bf[0], preferred_element_type=F32) + ba_ref[0]
        yb = jnp.dot(h, w_bf[1], preferred_element_type=F32) + bb_ref[0]
        o_ref[...] = _pack_bf16_pairs(ya, yb)

    @pl.when(i >= nv_ref[0])
    def _():
        o_ref[...] = jnp.zeros_like(o_ref)


def _expert_down(hs, meta, w_dn, b_dn, *, tm):
    n_rows, ff = hs.shape
    n_exp, _, d = w_dn.shape
    half = d // 2
    tn = _tile(half, 1024)
    nj = half // tn
    n_blocks = n_rows // tm
    b3 = b_dn.reshape(n_exp, 1, d)
    grid_spec = pltpu.PrefetchScalarGridSpec(
        num_scalar_prefetch=4,
        grid=(nj, n_blocks),
        in_specs=[
            pl.BlockSpec((tm, ff), lambda j, i, be, nv, *_: (jnp.minimum(i, nv[0] - 1), 0)),
            pl.BlockSpec(memory_space=pl.ANY),
            pl.BlockSpec((1, 1, tn), lambda j, i, be, *_: (be[i], 0, j)),
            pl.BlockSpec((1, 1, tn), lambda j, i, be, *_: (be[i], 0, nj + j)),
        ],
        out_specs=pl.BlockSpec((tm, tn), lambda j, i, *_: (i, j)),
        scratch_shapes=_expert_scratch(ff, tn),
    )
    return pl.pallas_call(
        functools.partial(_expert_down_kernel, half=half, tn=tn, nj=nj),
        grid_spec=grid_spec,
        out_shape=jax.ShapeDtypeStruct((n_rows, half), jnp.uint32),
        compiler_params=_params("arbitrary", "arbitrary"),
        name="expert_down",
    )(*meta, hs, w_dn, b3, b3)


def _combine_kernel(dest_ref, x_ref, gate_ref, g_ref, ys_hbm, o_ref, buf, sems, *, tb, n_steps):
    s = pl.program_id(0)
    cur = s & 1

    def row_copy(row, buf_slot, k, t):
        return pltpu.make_async_copy(ys_hbm.at[pl.ds(row, 1), :],
                                     buf.at[buf_slot, k, pl.ds(t, 1), :], sems.at[buf_slot])

    def gather(step, buf_slot):
        def issue(group, carry):
            t0 = pl.multiple_of(group * ROW_GROUP, ROW_GROUP)
            for u in range(ROW_GROUP):
                tok = step * tb + t0 + u
                for k in range(TOP_K):
                    row_copy(dest_ref[tok * TOP_K + k], buf_slot, k, t0 + u).start(
                        priority=k % 2)
            return carry

        lax.fori_loop(0, tb // ROW_GROUP, issue, 0)

    @pl.when(s == 0)
    def _():
        gather(0, 0)

    @pl.when(s + 1 < n_steps)
    def _():
        gather(s + 1, 1 - cur)

    def drain(t, carry):
        for k in range(TOP_K):
            row_copy(0, cur, k, t).wait()
        return carry

    lax.fori_loop(0, tb, drain, 0)

    half = x_ref.shape[-1] // 2
    acc_hi = x_ref[:, :half]
    acc_lo = x_ref[:, half:]
    for k in range(TOP_K):
        y_hi, y_lo = _unpack_pairs_f32(buf[cur, k])
        gate = gate_ref[:, k:k + 1]
        acc_hi = acc_hi + gate * y_hi
        acc_lo = acc_lo + gate * y_lo
    ms = (jnp.sum(acc_hi * acc_hi, axis=-1, keepdims=True)
          + jnp.sum(acc_lo * acc_lo, axis=-1, keepdims=True)) / (2 * half)
    r = lax.rsqrt(ms + NORM_EPS)
    o_ref[:, :half] = acc_hi * r * g_ref[:, :half]
    o_ref[:, half:] = acc_lo * r * g_ref[:, half:]


def _combine(x, gates, dest, ys, g):
    n_tok, d = x.shape
    tb = _tile(n_tok, 128)
    n_steps = n_tok // tb
    grid_spec = pltpu.PrefetchScalarGridSpec(
        num_scalar_prefetch=1,
        grid=(n_steps,),
        in_specs=[pl.BlockSpec((tb, d), lambda i, dest: (i, 0)),
                  pl.BlockSpec((tb, TOP_K), lambda i, dest: (i, 0)),
                  pl.BlockSpec((1, d), lambda i, dest: (0, 0)),
                  pl.BlockSpec(memory_space=pl.ANY)],
        out_specs=pl.BlockSpec((tb, d), lambda i, dest: (i, 0)),
        scratch_shapes=[pltpu.VMEM((2, TOP_K, tb, d // 2), ys.dtype),
                        pltpu.SemaphoreType.DMA((2,))],
    )
    return pl.pallas_call(
        functools.partial(_combine_kernel, tb=tb, n_steps=n_steps),
        grid_spec=grid_spec,
        out_shape=jax.ShapeDtypeStruct((n_tok, d), F32),
        compiler_params=_params("arbitrary"),
        name="combine",
    )(dest, x, gates, g.reshape(1, d), ys)


def _routing_tables(top_idx, rank, counts, tm):
    n_exp = counts.shape[0]
    flat_e = top_idx.reshape(-1)
    n_pairs = flat_e.shape[0]
    pcounts = (counts + tm - 1) // tm * tm
    pends = jnp.cumsum(pcounts)
    pstarts = pends - pcounts
    onehot = flat_e[:, None] == jnp.arange(n_exp, dtype=jnp.int32)[None, :]
    dest = jnp.sum(jnp.where(onehot, pstarts[None, :], 0), axis=1) + rank.reshape(-1)
    assert n_pairs % tm == 0
    n_blocks = n_pairs // tm + n_exp
    block_id = jnp.arange(n_blocks, dtype=jnp.int32)
    block_e = jnp.minimum(jnp.sum(pends[None, :] <= (block_id * tm)[:, None], axis=1),
                          n_exp - 1).astype(jnp.int32)
    n_valid = (pends[-1] // tm).astype(jnp.int32)
    prev_e = jnp.concatenate([jnp.full((1,), -1, jnp.int32), block_e[:-1]])
    is_change = (block_id < n_valid) & (block_e != prev_e)
    change_ord = jnp.cumsum(is_change.astype(jnp.int32))
    change = jnp.where(is_change, change_ord, 0).astype(jnp.int32)
    later = is_change[None, :] & (block_id[None, :] > block_id[:, None])
    next_e = jnp.where(jnp.any(later, axis=1), block_e[jnp.argmax(later, axis=1)], -1)
    meta = (block_e, n_valid.reshape(1), change, next_e.astype(jnp.int32))
    return (dest.astype(jnp.int32), pends.astype(jnp.int32), pcounts.astype(jnp.int32),
            meta, n_blocks * tm)


def kernel(x, attn_norm_g, w_in, lambda_q1, lambda_k1, lambda_q2, lambda_k2, diff_subln_g,
           sgu_ln_g, sgu_ln_b, sgu_w, sgu_b, rel_bias, w_out, ffn_norm_g, router_w, router_b,
           w_gate_up, b_gate_up, w_down, b_down, final_norm_g):
    batch, seq, d = x.shape
    n_tok = batch * seq
    hd = lambda_q1.shape[-1]
    n_maps = rel_bias.shape[1]
    n_heads = n_maps // 2
    q_cols = n_maps * hd
    attn_width = n_heads * diff_subln_g.shape[-1]
    n_exp = w_gate_up.shape[1]
    tq = _tile(seq, 512)
    tk = tq // 2
    tm = 256

    xt = x.reshape(n_tok, d)
    h = _rmsnorm_rows(xt, attn_norm_g[0], BF16)
    proj = _in_projection(h, w_in[0].astype(BF16))

    lam = (jnp.exp(jnp.sum(lambda_q1[0].astype(F32) * lambda_k1[0].astype(F32)))
           - jnp.exp(jnp.sum(lambda_q2[0].astype(F32) * lambda_k2[0].astype(F32)))
           + LAMBDA_INIT).reshape(1)
    bias = _bias_tiles(rel_bias, tq)
    far_bias = rel_bias[-1].astype(F32)
    att = _diff_attention(proj, bias, far_bias, lam, diff_subln_g[0],
                          batch=batch, seq=seq, n_heads=n_heads, hd=hd, tq=tq, tk=tk)
    sgu = _spatial_gating(proj, sgu_ln_g[0], sgu_ln_b[0], sgu_w[0], sgu_b[0],
                          u_col0=2 * q_cols + attn_width)
    x1 = _out_projection(att, sgu, w_out[0].astype(BF16), xt)

    hp, top_idx, gates, rank, counts = _router(x1, ffn_norm_g[0], router_w[0], router_b[0])
    dest, pends, pcounts, meta, n_rows = _routing_tables(
        top_idx, rank, counts[0].astype(jnp.int32), tm)
    xs = _dispatch(hp, dest, pends, pcounts, n_rows=n_rows, tm=tm)
    hs = _expert_up(xs, meta, w_gate_up[0], b_gate_up[0], tm=tm)
    ys = _expert_down(hs, meta, w_down[0], b_down[0], tm=tm)
    out = _combine(x1, gates, dest, ys, final_norm_g)
    return out.reshape(batch, seq, d)
```

```python
import functools
import math

import jax
import jax.numpy as jnp
from jax import lax
from jax.experimental import pallas as pl
from jax.experimental.pallas import tpu as pltpu

F32 = jnp.float32
BF16 = jnp.bfloat16

TOP_K = 4
MAX_DISTANCE = 128
SWIGLU_LIMIT = 7.0
SWIGLU_ALPHA = 1.702
NORM_EPS = 1e-5
LAMBDA_INIT = 0.8 - 0.6 * math.exp(0.0)
MASK_VALUE = -1e30
LOG2E = math.log2(math.e)

LANES = 128
BF16_SUBLANES = 16
ROW_GROUP = 8
VMEM_LIMIT_BYTES = 56 * 1024 * 1024


def _tile(dim, want):
    t = min(dim, want)
    while dim % t:
        t -= LANES
    assert t > 0, (dim, want)
    return t


def _params(*semantics):
    return pltpu.CompilerParams(dimension_semantics=semantics,
                                vmem_limit_bytes=VMEM_LIMIT_BYTES)


def _rmsnorm_kernel(x_ref, g_ref, o_ref):
    x = x_ref[...]
    ms = jnp.mean(x * x, axis=-1, keepdims=True)
    o_ref[...] = (x * lax.rsqrt(ms + NORM_EPS) * g_ref[...]).astype(o_ref.dtype)


def _rmsnorm_rows(x, g, out_dtype):
    n, d = x.shape
    tr = _tile(n, 512)
    return pl.pallas_call(
        _rmsnorm_kernel,
        grid=(n // tr,),
        in_specs=[pl.BlockSpec((tr, d), lambda i: (i, 0)),
                  pl.BlockSpec((1, d), lambda i: (0, 0))],
        out_specs=pl.BlockSpec((tr, d), lambda i: (i, 0)),
        out_shape=jax.ShapeDtypeStruct((n, d), out_dtype),
        compiler_params=_params("parallel"),
        name="rmsnorm_rows",
    )(x, g.reshape(1, d))


def _in_projection_kernel(a_ref, b_ref, o_ref):
    acc = jnp.dot(a_ref[...], b_ref[...], preferred_element_type=F32)
    for c in range(o_ref.shape[0]):
        o_ref[c] = acc[:, c * LANES:(c + 1) * LANES].astype(o_ref.dtype)


def _in_projection(h, w):
    m, k = h.shape
    n = w.shape[1]
    tm, tn = _tile(m, 1024), _tile(n, 1024)
    return pl.pallas_call(
        _in_projection_kernel,
        grid=(m // tm, n // tn),
        in_specs=[pl.BlockSpec((tm, k), lambda i, j: (i, 0)),
                  pl.BlockSpec((k, tn), lambda i, j: (0, j))],
        out_specs=pl.BlockSpec((tn // LANES, tm, LANES), lambda i, j: (j, i, 0)),
        out_shape=jax.ShapeDtypeStruct((n // LANES, m, LANES), BF16),
        compiler_params=_params("parallel", "parallel"),
        name="in_projection",
    )(h, w)


def _out_projection_kernel(a_ref, s_ref, wa_ref, ws_ref, x_ref, o_ref):
    acc = jnp.dot(a_ref[...], wa_ref[...], preferred_element_type=F32)
    acc += jnp.dot(s_ref[...], ws_ref[...], preferred_element_type=F32)
    o_ref[...] = x_ref[...] + acc


def _out_projection(att, sgu, w, x):
    m, ka = att.shape
    ks = sgu.shape[1]
    n = w.shape[1]
    assert ka == ks and w.shape[0] == ka + ks
    tm, tn = _tile(m, 1024), _tile(n, 512)
    return pl.pallas_call(
        _out_projection_kernel,
        grid=(m // tm, n // tn),
        in_specs=[pl.BlockSpec((tm, ka), lambda i, j: (i, 0)),
                  pl.BlockSpec((tm, ks), lambda i, j: (i, 0)),
                  pl.BlockSpec((ka, tn), lambda i, j: (0, j)),
                  pl.BlockSpec((ks, tn), lambda i, j: (1, j)),
                  pl.BlockSpec((tm, tn), lambda i, j: (i, j))],
        out_specs=pl.BlockSpec((tm, tn), lambda i, j: (i, j)),
        out_shape=jax.ShapeDtypeStruct((m, n), F32),
        compiler_params=_params("parallel", "parallel"),
        name="out_projection",
    )(att, sgu, w, w, x)


def _bias_tiles_kernel(tbl_ref, o_ref, *, tq, n_buckets):
    h = pl.program_id(0)
    max_exact = n_buckets // 2
    nb = tq // LANES
    key = lax.broadcasted_iota(jnp.int32, (LANES, LANES), 0)
    qry = lax.broadcasted_iota(jnp.int32, (LANES, LANES), 1)

    def lookup(dist, m):
        d = jnp.maximum(dist, 1).astype(F32)
        large = max_exact + (jnp.log(d / max_exact) / math.log(MAX_DISTANCE / max_exact)
                             * (n_buckets - max_exact)).astype(jnp.int32)
        large = jnp.minimum(large, n_buckets - 1)
        bucket = jnp.where(dist < max_exact, dist, large)
        val = jnp.zeros(dist.shape, F32)
        for b in range(n_buckets):
            val = jnp.where(bucket == b, tbl_ref[b, 2 * h + m], val)
        return jnp.where(dist >= 0, val * LOG2E, MASK_VALUE)

    for w in range(2):
        for delta in range(-(nb - 1), nb):
            offset = delta * LANES + w * tq
            for m in range(2):
                if offset + LANES - 1 < 0:
                    block = jnp.full((LANES, LANES), MASK_VALUE, F32)
                elif offset - (LANES - 1) >= MAX_DISTANCE:
                    block = jnp.full((LANES, LANES), tbl_ref[n_buckets - 1, 2 * h + m], F32) * LOG2E
                else:
                    block = lookup(qry - key + offset, m)
                for kc in range(max(0, -delta), min(nb, nb - delta)):
                    qc = kc + delta
                    o_ref[0, w, kc * LANES:(kc + 1) * LANES,
                          m * tq + qc * LANES:m * tq + (qc + 1) * LANES] = block


def _bias_tiles(rel_bias, tq):
    n_buckets, n_maps = rel_bias.shape
    n_heads = n_maps // 2
    assert tq >= MAX_DISTANCE
    return pl.pallas_call(
        functools.partial(_bias_tiles_kernel, tq=tq, n_buckets=n_buckets),
        grid=(n_heads,),
        in_specs=[pl.BlockSpec(memory_space=pltpu.SMEM)],
        out_specs=pl.BlockSpec((1, 2, tq, 2 * tq), lambda h: (h, 0, 0, 0)),
        out_shape=jax.ShapeDtypeStruct((n_heads, 2, tq, 2 * tq), F32),
        compiler_params=_params("parallel"),
        name="bias_tiles",
    )(rel_bias.astype(F32))


def _attention_kernel(far_ref, lam_ref, q_ref, k_ref, v_ref, bias_ref, g_ref, o_ref,
                      vt_ref, sa_ref, sb_ref, acc_ref, *, tq, tk, hd):
    h = pl.program_id(1)
    i = pl.program_id(2)
    _, seq, vd = v_ref.shape
    sub = tq // tk
    assert sub == 2

    @pl.when(i == 0)
    def _():
        ones_row = lax.broadcasted_iota(jnp.int32, (BF16_SUBLANES, tk), 0) == 0
        for c in range(seq // tk):
            vt_ref[c, :vd, :] = v_ref[0, c * tk:(c + 1) * tk, :].astype(F32).T.astype(BF16)
            vt_ref[c, vd:, :] = jnp.where(ones_row, 1.0, 0.0).astype(BF16)

    qt = (q_ref[0].astype(F32) * (hd ** -0.5 * LOG2E)).T
    dim = lax.broadcasted_iota(jnp.int32, qt.shape, 0)
    qqt = jnp.concatenate([jnp.where(dim < hd, qt, 0.0),
                           jnp.where(dim >= hd, qt, 0.0)], axis=1).astype(BF16)
    col = lax.broadcasted_iota(jnp.int32, (1, 2 * tq), 1)
    far = jnp.where(col < tq, far_ref[2 * h], far_ref[2 * h + 1]) * LOG2E

    def scores(kb):
        kblk = k_ref[0, pl.ds(pl.multiple_of(kb * tk, tk), tk), :]
        return jnp.dot(kblk, qqt, preferred_element_type=F32)

    def online_update(s, kb, m_prev, shift):
        m_new = jnp.maximum(m_prev, jnp.max(s, axis=0, keepdims=True) + shift)
        alpha = jnp.exp2(m_prev - m_new)
        p = jnp.exp2((s - (m_new - shift)).astype(BF16))
        acc_ref[...] = alpha * acc_ref[...] + jnp.dot(vt_ref[kb], p,
                                                      preferred_element_type=F32)
        return m_new

    kb_prev = sub * jnp.maximum(i - 1, 0)
    no_prev = jnp.where(i >= 1, 0.0, MASK_VALUE)
    n_pairs = jnp.maximum(i - 1, 0)
    last = jnp.maximum(sub * n_pairs - 1, 0)
    acc_ref[...] = jnp.zeros_like(acc_ref)
    m = jnp.full((1, 2 * tq), MASK_VALUE, F32)

    sa_ref[...] = scores(sub * i) + bias_ref[0, 0, :tk, :]
    sb_ref[...] = scores(sub * i + 1) + bias_ref[0, 0, tk:, :]
    m = online_update(sa_ref[...], sub * i, m, 0.0)
    sa_ref[...] = scores(kb_prev) + (bias_ref[0, 1, :tk, :] + no_prev)
    m = online_update(sb_ref[...], sub * i + 1, m, 0.0)
    sb_ref[...] = scores(kb_prev + 1) + (bias_ref[0, 1, tk:, :] + no_prev)
    m = online_update(sa_ref[...], kb_prev, m, 0.0)
    sa_ref[...] = scores(0)
    m = online_update(sb_ref[...], kb_prev + 1, m, 0.0)

    def far_pair(n, m_run):
        kb = sub * n
        sb_ref[...] = scores(kb + 1)
        m_run = online_update(sa_ref[...], kb, m_run, far)
        sa_ref[...] = scores(jnp.minimum(kb + 2, last))
        return online_update(sb_ref[...], kb + 1, m_run, far)

    lax.fori_loop(0, n_pairs, far_pair, m)

    acc = acc_ref[...]
    o = acc[:vd] * (1.0 / acc[vd:vd + 1])
    a = o[:, :tq] - lam_ref[0] * o[:, tq:]
    ms = jnp.mean(a * a, axis=0, keepdims=True)
    y = a * lax.rsqrt(ms + NORM_EPS) * g_ref[...] * (1.0 - LAMBDA_INIT)
    o_ref[...] = y.T.astype(o_ref.dtype)


def _diff_attention(proj, bias, far_bias, lam, subln_g, *, batch, seq, n_heads, hd, tq, tk):
    vd = subln_g.shape[-1]
    assert 2 * hd == LANES and vd == LANES
    nq = seq // tq
    k_col0 = n_heads
    v_col0 = 2 * n_heads
    grid_spec = pltpu.PrefetchScalarGridSpec(
        num_scalar_prefetch=2,
        grid=(batch, n_heads, nq),
        in_specs=[
            pl.BlockSpec((1, tq, LANES), lambda b, h, i, *_: (h, b * nq + i, 0)),
            pl.BlockSpec((1, seq, LANES), lambda b, h, i, *_: (k_col0 + h, b, 0)),
            pl.BlockSpec((1, seq, LANES), lambda b, h, i, *_: (v_col0 + h, b, 0)),
            pl.BlockSpec((1, 2, tq, 2 * tq), lambda b, h, i, *_: (h, 0, 0, 0)),
            pl.BlockSpec((vd, 1), lambda b, h, i, *_: (0, 0)),
        ],
        out_specs=pl.BlockSpec((tq, vd), lambda b, h, i, *_: (b * nq + i, h)),
        scratch_shapes=[pltpu.VMEM((seq // tk, vd + BF16_SUBLANES, tk), BF16),
                        pltpu.VMEM((tk, 2 * tq), F32),
                        pltpu.VMEM((tk, 2 * tq), F32),
                        pltpu.VMEM((vd + BF16_SUBLANES, 2 * tq), F32)],
    )
    return pl.pallas_call(
        functools.partial(_attention_kernel, tq=tq, tk=tk, hd=hd),
        grid_spec=grid_spec,
        out_shape=jax.ShapeDtypeStruct((batch * seq, n_heads * vd), BF16),
        compiler_params=_params("parallel", "parallel", "arbitrary"),
        name="diff_attention",
    )(far_bias, lam, proj, proj, proj, bias, subln_g.reshape(vd, 1))


def _gelu(x):
    return 0.5 * x * (1.0 + lax.erf(x * math.sqrt(0.5)))


def _sgu_kernel(u_ref, v_ref, lng_ref, lnb_ref, w_ref, bt_ref, o_ref):
    n_heads, chunk, hdim = u_ref.shape
    width = n_heads * hdim
    u = _gelu(u_ref[...].astype(F32))
    v = _gelu(v_ref[...].astype(F32))
    mu = jnp.sum(jnp.sum(v, axis=0), axis=-1, keepdims=True) / width
    vc = v - mu
    var = jnp.sum(jnp.sum(vc * vc, axis=0), axis=-1, keepdims=True) / width
    vn = (vc * lax.rsqrt(var + NORM_EPS) * lng_ref[...] + lnb_ref[...]).astype(BF16)
    r = lax.broadcasted_iota(jnp.int32, (chunk, chunk), 0)
    c = lax.broadcasted_iota(jnp.int32, (chunk, chunk), 1)
    causal = r >= c
    for hh in range(n_heads):
        w = jnp.where(causal, w_ref[hh], 0.0).astype(BF16)
        y = jnp.dot(w, vn[hh], preferred_element_type=F32) + bt_ref[:, hh:hh + 1]
        o_ref[:, hh * hdim:(hh + 1) * hdim] = (u[hh] * y).astype(o_ref.dtype)


def _spatial_gating(proj, ln_g, ln_b, w_s, b_s, *, u_col0):
    _, n_tok, hdim = proj.shape
    n_heads, chunk, _ = w_s.shape
    width = ln_g.shape[-1]
    assert width == n_heads * hdim and u_col0 % width == 0
    ub = u_col0 // width
    return pl.pallas_call(
        _sgu_kernel,
        grid=(n_tok // chunk,),
        in_specs=[pl.BlockSpec((n_heads, chunk, hdim), lambda i: (ub, i, 0)),
                  pl.BlockSpec((n_heads, chunk, hdim), lambda i: (ub + 1, i, 0)),
                  pl.BlockSpec((n_heads, 1, hdim), lambda i: (0, 0, 0)),
                  pl.BlockSpec((n_heads, 1, hdim), lambda i: (0, 0, 0)),
                  pl.BlockSpec((n_heads, chunk, chunk), lambda i: (0, 0, 0)),
                  pl.BlockSpec((chunk, n_heads), lambda i: (0, 0))],
        out_specs=pl.BlockSpec((chunk, width), lambda i: (i, 0)),
        out_shape=jax.ShapeDtypeStruct((n_tok, width), BF16),
        compiler_params=_params("parallel"),
        name="spatial_gating",
    )(proj, proj, ln_g.reshape(n_heads, 1, hdim), ln_b.reshape(n_heads, 1, hdim), w_s, b_s.T)


def _split_bf16(x):
    hi = x.astype(BF16)
    return hi, (x - hi.astype(F32)).astype(BF16)


def _router_kernel(x_ref, g_ref, rw_ref, rb_ref, hp_ref, idx_ref, gate_ref, rank_ref, cnt_ref):
    @pl.when(pl.program_id(0) == 0)
    def _():
        cnt_ref[...] = jnp.zeros_like(cnt_ref)

    x = x_ref[...]
    ms = jnp.mean(x * x, axis=-1, keepdims=True)
    h = x * lax.rsqrt(ms + NORM_EPS) * g_ref[...]
    half = h.shape[-1] // 2
    hp_ref[...] = _pack_bf16_pairs(h[:, :half], h[:, half:])

    h_hi, h_lo = _split_bf16(h)
    w_hi, w_lo = _split_bf16(rw_ref[...])
    logits = (jnp.dot(h_hi, w_hi, preferred_element_type=F32)
              + jnp.dot(h_lo, w_hi, preferred_element_type=F32)
              + jnp.dot(h_hi, w_lo, preferred_element_type=F32)) + rb_ref[...]
    tr, n_exp = logits.shape
    lane = lax.broadcasted_iota(jnp.int32, logits.shape, 1)
    vals, idxs = [], []
    for _ in range(TOP_K):
        top = jnp.max(logits, axis=-1, keepdims=True)
        idx = jnp.min(jnp.where(logits == top, lane, n_exp), axis=-1, keepdims=True)
        vals.append(top)
        idxs.append(idx)
        logits = jnp.where(lane == idx, -jnp.inf, logits)
    exps = [jnp.exp(v - vals[0]) for v in vals]
    denom = exps[0]
    for e in exps[1:]:
        denom = denom + e

    earlier = (lax.broadcasted_iota(jnp.int32, (tr, tr), 1)
               < lax.broadcasted_iota(jnp.int32, (tr, tr), 0))
    earlier = jnp.where(earlier, 1.0, 0.0).astype(BF16)
    count = cnt_ref[...]
    for k in range(TOP_K):
        chosen = lane == idxs[k]
        onehot = jnp.where(chosen, 1.0, 0.0)
        before = jnp.dot(earlier, onehot.astype(BF16), preferred_element_type=F32) + count
        rank = jnp.sum(jnp.where(chosen, before, 0.0), axis=-1, keepdims=True)
        idx_ref[:, k:k + 1] = idxs[k]
        gate_ref[:, k:k + 1] = exps[k] / denom
        rank_ref[:, k:k + 1] = rank.astype(jnp.int32)
        count = count + jnp.sum(onehot, axis=0, keepdims=True)
    cnt_ref[...] = count


def _router(x, g, rw, rb):
    n, d = x.shape
    n_exp = rw.shape[1]
    tr = _tile(n, 512)
    assert n * TOP_K < 2 ** 24
    return pl.pallas_call(
        _router_kernel,
        grid=(n // tr,),
        in_specs=[pl.BlockSpec((tr, d), lambda i: (i, 0)),
                  pl.BlockSpec((1, d), lambda i: (0, 0)),
                  pl.BlockSpec((d, n_exp), lambda i: (0, 0)),
                  pl.BlockSpec((1, n_exp), lambda i: (0, 0))],
        out_specs=[pl.BlockSpec((tr, d // 2), lambda i: (i, 0)),
                   pl.BlockSpec((tr, TOP_K), lambda i: (i, 0)),
                   pl.BlockSpec((tr, TOP_K), lambda i: (i, 0)),
                   pl.BlockSpec((tr, TOP_K), lambda i: (i, 0)),
                   pl.BlockSpec((1, n_exp), lambda i: (0, 0))],
        out_shape=[jax.ShapeDtypeStruct((n, d // 2), jnp.uint32),
                   jax.ShapeDtypeStruct((n, TOP_K), jnp.int32),
                   jax.ShapeDtypeStruct((n, TOP_K), F32),
                   jax.ShapeDtypeStruct((n, TOP_K), jnp.int32),
                   jax.ShapeDtypeStruct((1, n_exp), F32)],
        compiler_params=_params("arbitrary"),
        name="router",
    )(x, g.reshape(1, d), rw, rb.reshape(1, n_exp))


def _dispatch_kernel(dest_ref, pends_ref, pcnt_ref, h_ref, xs_hbm, zbuf, zsem, sem,
                     *, tb, tm, n_exp):
    s = pl.program_id(0)

    n_rows = xs_hbm.shape[0]

    def zero_block(start):
        return pltpu.make_async_copy(
            zbuf, xs_hbm.at[pl.ds(pl.multiple_of(start, tm), tm), :], zsem)

    def zero_fill(action):
        for e in range(n_exp):
            @pl.when(pcnt_ref[e] > 0)
            def _():
                action(zero_block(pends_ref[e] - tm))
        for b in range(n_exp):
            start = pends_ref[n_exp - 1] + b * tm

            @pl.when(start < n_rows)
            def _():
                action(zero_block(start))

    @pl.when(s == 0)
    def _():
        zbuf[...] = jnp.zeros_like(zbuf)
        zero_fill(lambda copy: copy.start())
        zero_fill(lambda copy: copy.wait())

    def row_copy(t, slot):
        return pltpu.make_async_copy(h_ref.at[pl.ds(t, 1), :],
                                     xs_hbm.at[pl.ds(slot, 1), :], sem)

    def issue(group, carry):
        t0 = pl.multiple_of(group * ROW_GROUP, ROW_GROUP)
        for u in range(ROW_GROUP):
            tok = s * tb + t0 + u
            for k in range(TOP_K):
                row_copy(t0 + u, dest_ref[tok * TOP_K + k]).start(priority=k % 2)
        return carry

    lax.fori_loop(0, tb // ROW_GROUP, issue, 0)

    def drain(t, carry):
        for k in range(TOP_K):
            row_copy(0, 0).wait()
        return carry

    lax.fori_loop(0, tb, drain, 0)


def _dispatch(hp, dest, pends, pcounts, *, n_rows, tm):
    n_tok, dw = hp.shape
    n_exp = pends.shape[0]
    tb = _tile(n_tok, 256)
    grid_spec = pltpu.PrefetchScalarGridSpec(
        num_scalar_prefetch=3,
        grid=(n_tok // tb,),
        in_specs=[pl.BlockSpec((tb, dw), lambda i, *_: (i, 0))],
        out_specs=pl.BlockSpec(memory_space=pl.ANY),
        scratch_shapes=[pltpu.VMEM((tm, dw), hp.dtype),
                        pltpu.SemaphoreType.DMA(()),
                        pltpu.SemaphoreType.DMA(())],
    )
    return pl.pallas_call(
        functools.partial(_dispatch_kernel, tb=tb, tm=tm, n_exp=n_exp),
        grid_spec=grid_spec,
        out_shape=jax.ShapeDtypeStruct((n_rows, dw), hp.dtype),
        compiler_params=_params("arbitrary"),
        name="dispatch",
    )(dest, pends, pcounts, hp)


def _pack_bf16_pairs(hi, lo):
    hb = pltpu.bitcast(hi.astype(BF16).astype(F32), jnp.uint32)
    lb = pltpu.bitcast(lo.astype(BF16).astype(F32), jnp.uint32)
    return hb | (lb >> 16)


def _unpack_pairs_f32(p):
    return (pltpu.bitcast(p & jnp.uint32(0xFFFF0000), F32), pltpu.bitcast(p << 16, F32))


def _stream_expert_weights(be_ref, nv_ref, chg_ref, nxt_ref, w_hbm, stage, sems, w_bf,
                           *, half_stride, tn, nj):
    j = pl.program_id(0)
    i = pl.program_id(1)

    def copies(e, jj):
        return [pltpu.make_async_copy(
            w_hbm.at[e, :, pl.ds(pl.multiple_of(part * half_stride + jj * tn, tn), tn)],
            stage.at[part], sems.at[part]) for part in range(2)]

    change = chg_ref[i]

    @pl.when(change > 0)
    def _():
        @pl.when(jnp.logical_and(j == 0, change == 1))
        def _():
            for cp in copies(be_ref[i], j):
                cp.start()

        for part, cp in enumerate(copies(be_ref[i], j)):
            cp.wait()
            w_bf[part] = stage[part].astype(BF16)

        last = nxt_ref[i] < 0
        next_e = jnp.where(last, be_ref[0], nxt_ref[i])
        next_j = jnp.where(last, j + 1, j)

        @pl.when(next_j < nj)
        def _():
            for cp in copies(next_e, next_j):
                cp.start(priority=1)


def _expert_up_kernel(be_ref, nv_ref, chg_ref, nxt_ref, xs_ref, w_hbm, bg_ref, bu_ref, o_ref,
                      stage, w_bf, sems, *, ff, tn, nj):
    i = pl.program_id(1)
    _stream_expert_weights(be_ref, nv_ref, chg_ref, nxt_ref, w_hbm, stage, sems, w_bf,
                           half_stride=ff, tn=tn, nj=nj)

    @pl.when(i < nv_ref[0])
    def _():
        x_hi, x_lo = _unpack_pairs_f32(xs_ref[...])
        x_hi, x_lo = x_hi.astype(BF16), x_lo.astype(BF16)
        half = x_hi.shape[-1]
        gate = (jnp.dot(x_hi, w_bf[0, :half, :], preferred_element_type=F32)
                + jnp.dot(x_lo, w_bf[0, half:, :], preferred_element_type=F32) + bg_ref[0])
        up = (jnp.dot(x_hi, w_bf[1, :half, :], preferred_element_type=F32)
              + jnp.dot(x_lo, w_bf[1, half:, :], preferred_element_type=F32) + bu_ref[0])
        gate = jnp.minimum(gate, SWIGLU_LIMIT)
        up = jnp.clip(up, -SWIGLU_LIMIT, SWIGLU_LIMIT)
        act = (up + 1.0) * gate * (1.0 / (1.0 + jnp.exp(-SWIGLU_ALPHA * gate)))
        o_ref[...] = act.astype(o_ref.dtype)

    @pl.when(i >= nv_ref[0])
    def _():
        o_ref[...] = jnp.zeros_like(o_ref)


def _expert_scratch(k, tn):
    return [pltpu.VMEM((2, k, tn), F32),
            pltpu.VMEM((2, k, tn), BF16),
            pltpu.SemaphoreType.DMA((2,))]


def _expert_up(xs, meta, w_gu, b_gu, *, tm):
    n_rows, dw = xs.shape
    n_exp, d, ff2 = w_gu.shape
    ff = ff2 // 2
    assert d == 2 * dw
    tn = _tile(ff, 512)
    nj = ff // tn
    n_blocks = n_rows // tm
    b3 = b_gu.reshape(n_exp, 1, ff2)
    grid_spec = pltpu.PrefetchScalarGridSpec(
        num_scalar_prefetch=4,
        grid=(nj, n_blocks),
        in_specs=[
            pl.BlockSpec((tm, dw), lambda j, i, be, nv, *_: (jnp.minimum(i, nv[0] - 1), 0)),
            pl.BlockSpec(memory_space=pl.ANY),
            pl.BlockSpec((1, 1, tn), lambda j, i, be, *_: (be[i], 0, j)),
            pl.BlockSpec((1, 1, tn), lambda j, i, be, *_: (be[i], 0, nj + j)),
        ],
        out_specs=pl.BlockSpec((tm, tn), lambda j, i, *_: (i, j)),
        scratch_shapes=_expert_scratch(d, tn),
    )
    return pl.pallas_call(
        functools.partial(_expert_up_kernel, ff=ff, tn=tn, nj=nj),
        grid_spec=grid_spec,
        out_shape=jax.ShapeDtypeStruct((n_rows, ff), BF16),
        compiler_params=_params("arbitrary", "arbitrary"),
        name="expert_up",
    )(*meta, xs, w_gu, b3, b3)


def _expert_down_kernel(be_ref, nv_ref, chg_ref, nxt_ref, h_ref, w_hbm, ba_ref, bb_ref, o_ref,
                        stage, w_bf, sems, *, half, tn, nj):
    i = pl.program_id(1)
    _stream_expert_weights(be_ref, nv_ref, chg_ref, nxt_ref, w_hbm, stage, sems, w_bf,
                           half_stride=half, tn=tn, nj=nj)

    @pl.when(i < nv_ref[0])
    def _():
        h = h_ref[...]
        ya = jnp.dot(h, w_bf[0], preferred_element_type=F32) + ba_ref[0]
        yb = jnp.dot(h, w_bf[1], preferred_element_type=F32) + bb_ref[0]
        o_ref[...] = _pack_bf16_pairs(ya, yb)

    @pl.when(i >= nv_ref[0])
    def _():
        o_ref[...] = jnp.zeros_like(o_ref)


def _expert_down(hs, meta, w_dn, b_dn, *, tm):
    n_rows, ff = hs.shape
    n_exp, _, d = w_dn.shape
    half = d // 2
    tn = _tile(half, 1024)
    nj = half // tn
    n_blocks = n_rows // tm
    b3 = b_dn.reshape(n_exp, 1, d)
    grid_spec = pltpu.PrefetchScalarGridSpec(
        num_scalar_prefetch=4,
        grid=(nj, n_blocks),
        in_specs=[
            pl.BlockSpec((tm, ff), lambda j, i, be, nv, *_: (jnp.minimum(i, nv[0] - 1), 0)),
            pl.BlockSpec(memory_space=pl.ANY),
            pl.BlockSpec((1, 1, tn), lambda j, i, be, *_: (be[i], 0, j)),
            pl.BlockSpec((1, 1, tn), lambda j, i, be, *_: (be[i], 0, nj + j)),
        ],
        out_specs=pl.BlockSpec((tm, tn), lambda j, i, *_: (i, j)),
        scratch_shapes=_expert_scratch(ff, tn),
    )
    return pl.pallas_call(
        functools.partial(_expert_down_kernel, half=half, tn=tn, nj=nj),
        grid_spec=grid_spec,
        out_shape=jax.ShapeDtypeStruct((n_rows, half), jnp.uint32),
        compiler_params=_params("arbitrary", "arbitrary"),
        name="expert_down",
    )(*meta, hs, w_dn, b3, b3)


def _combine_kernel(dest_ref, x_ref, gate_ref, g_ref, ys_hbm, o_ref, buf, sems, *, tb, n_steps):
    s = pl.program_id(0)
    cur = s & 1

    def row_copy(row, buf_slot, k, t):
        return pltpu.make_async_copy(ys_hbm.at[pl.ds(row, 1), :],
                                     buf.at[buf_slot, k, pl.ds(t, 1), :], sems.at[buf_slot])

    def gather(step, buf_slot):
        def issue(group, carry):
            t0 = pl.multiple_of(group * ROW_GROUP, ROW_GROUP)
            for u in range(ROW_GROUP):
                tok = step * tb + t0 + u
                for k in range(TOP_K):
                    row_copy(dest_ref[tok * TOP_K + k], buf_slot, k, t0 + u).start(
                        priority=k % 2)
            return carry

        lax.fori_loop(0, tb // ROW_GROUP, issue, 0)

    @pl.when(s == 0)
    def _():
        gather(0, 0)

    @pl.when(s + 1 < n_steps)
    def _():
        gather(s + 1, 1 - cur)

    def drain(t, carry):
        for k in range(TOP_K):
            row_copy(0, cur, 0, 0).wait()
        return carry

    lax.fori_loop(0, tb, drain, 0)

    half = x_ref.shape[-1] // 2
    acc_hi = x_ref[:, :half]
    acc_lo = x_ref[:, half:]
    for k in range(TOP_K):
        y_hi, y_lo = _unpack_pairs_f32(buf[cur, k])
        gate = gate_ref[:, k:k + 1]
        acc_hi = acc_hi + gate * y_hi
        acc_lo = acc_lo + gate * y_lo
    ms = (jnp.sum(acc_hi * acc_hi, axis=-1, keepdims=True)
          + jnp.sum(acc_lo * acc_lo, axis=-1, keepdims=True)) / (2 * half)
    r = lax.rsqrt(ms + NORM_EPS)
    o_ref[:, :half] = acc_hi * r * g_ref[:, :half]
    o_ref[:, half:] = acc_lo * r * g_ref[:, half:]


def _combine(x, gates, dest, ys, g):
    n_tok, d = x.shape
    tb = _tile(n_tok, 256)
    n_steps = n_tok // tb
    grid_spec = pltpu.PrefetchScalarGridSpec(
        num_scalar_prefetch=1,
        grid=(n_steps,),
        in_specs=[pl.BlockSpec((tb, d), lambda i, dest: (i, 0)),
                  pl.BlockSpec((tb, TOP_K), lambda i, dest: (i, 0)),
                  pl.BlockSpec((1, d), lambda i, dest: (0, 0)),
                  pl.BlockSpec(memory_space=pl.ANY)],
        out_specs=pl.BlockSpec((tb, d), lambda i, dest: (i, 0)),
        scratch_shapes=[pltpu.VMEM((2, TOP_K, tb, d // 2), ys.dtype),
                        pltpu.SemaphoreType.DMA((2,))],
    )
    return pl.pallas_call(
        functools.partial(_combine_kernel, tb=tb, n_steps=n_steps),
        grid_spec=grid_spec,
        out_shape=jax.ShapeDtypeStruct((n_tok, d), F32),
        compiler_params=_params("arbitrary"),
        name="combine",
    )(dest, x, gates, g.reshape(1, d), ys)


def _routing_tables(top_idx, rank, counts, tm):
    n_exp = counts.shape[0]
    flat_e = top_idx.reshape(-1)
    n_pairs = flat_e.shape[0]
    pcounts = (counts + tm - 1) // tm * tm
    pends = jnp.cumsum(pcounts)
    pstarts = pends - pcounts
    onehot = flat_e[:, None] == jnp.arange(n_exp, dtype=jnp.int32)[None, :]
    dest = jnp.sum(jnp.where(onehot, pstarts[None, :], 0), axis=1) + rank.reshape(-1)
    assert n_pairs % tm == 0
    n_blocks = n_pairs // tm + n_exp
    block_id = jnp.arange(n_blocks, dtype=jnp.int32)
    block_e = jnp.minimum(jnp.sum(pends[None, :] <= (block_id * tm)[:, None], axis=1),
                          n_exp - 1).astype(jnp.int32)
    n_valid = (pends[-1] // tm).astype(jnp.int32)
    prev_e = jnp.concatenate([jnp.full((1,), -1, jnp.int32), block_e[:-1]])
    is_change = (block_id < n_valid) & (block_e != prev_e)
    change_ord = jnp.cumsum(is_change.astype(jnp.int32))
    change = jnp.where(is_change, change_ord, 0).astype(jnp.int32)
    later = is_change[None, :] & (block_id[None, :] > block_id[:, None])
    next_e = jnp.where(jnp.any(later, axis=1), block_e[jnp.argmax(later, axis=1)], -1)
    meta = (block_e, n_valid.reshape(1), change, next_e.astype(jnp.int32))
    return (dest.astype(jnp.int32), pends.astype(jnp.int32), pcounts.astype(jnp.int32),
            meta, n_blocks * tm)


def kernel(x, attn_norm_g, w_in, lambda_q1, lambda_k1, lambda_q2, lambda_k2, diff_subln_g,
           sgu_ln_g, sgu_ln_b, sgu_w, sgu_b, rel_bias, w_out, ffn_norm_g, router_w, router_b,
           w_gate_up, b_gate_up, w_down, b_down, final_norm_g):
    batch, seq, d = x.shape
    n_tok = batch * seq
    hd = lambda_q1.shape[-1]
    n_maps = rel_bias.shape[1]
    n_heads = n_maps // 2
    q_cols = n_maps * hd
    attn_width = n_heads * diff_subln_g.shape[-1]
    n_exp = w_gate_up.shape[1]
    tq = _tile(seq, 512)
    tk = tq // 2
    tm = 512

    xt = x.reshape(n_tok, d)
    h = _rmsnorm_rows(xt, attn_norm_g[0], BF16)
    proj = _in_projection(h, w_in[0].astype(BF16))

    lam = (jnp.exp(jnp.sum(lambda_q1[0].astype(F32) * lambda_k1[0].astype(F32)))
           - jnp.exp(jnp.sum(lambda_q2[0].astype(F32) * lambda_k2[0].astype(F32)))
           + LAMBDA_INIT).reshape(1)
    bias = _bias_tiles(rel_bias, tq)
    far_bias = rel_bias[-1].astype(F32)
    att = _diff_attention(proj, bias, far_bias, lam, diff_subln_g[0],
                          batch=batch, seq=seq, n_heads=n_heads, hd=hd, tq=tq, tk=tk)
    sgu = _spatial_gating(proj, sgu_ln_g[0], sgu_ln_b[0], sgu_w[0], sgu_b[0],
                          u_col0=2 * q_cols + attn_width)
    x1 = _out_projection(att, sgu, w_out[0].astype(BF16), xt)

    hp, top_idx, gates, rank, counts = _router(x1, ffn_norm_g[0], router_w[0], router_b[0])
    dest, pends, pcounts, meta, n_rows = _routing_tables(
        top_idx, rank, counts[0].astype(jnp.int32), tm)
    xs = _dispatch(hp, dest, pends, pcounts, n_rows=n_rows, tm=tm)
    hs = _expert_up(xs, meta, w_gate_up[0], b_gate_up[0], tm=tm)
    ys = _expert_down(hs, meta, w_down[0], b_down[0], tm=tm)
    out = _combine(x1, gates, dest, ys, final_norm_g)
    return out.reshape(batch, seq, d)
```

```python
import functools
import math

import jax
import jax.numpy as jnp
from jax import lax
from jax.experimental import pallas as pl
from jax.experimental.pallas import tpu as pltpu

F32 = jnp.float32
BF16 = jnp.bfloat16

TOP_K = 4
MAX_DISTANCE = 128
SWIGLU_LIMIT = 7.0
SWIGLU_ALPHA = 1.702
NORM_EPS = 1e-5
LAMBDA_INIT = 0.8 - 0.6 * math.exp(0.0)
MASK_VALUE = -1e30
LOG2E = math.log2(math.e)

LANES = 128
BF16_SUBLANES = 16
ROW_GROUP = 8
VMEM_LIMIT_BYTES = 56 * 1024 * 1024


def _tile(dim, want):
    t = min(dim, want)
    while dim % t:
        t -= LANES
    assert t > 0, (dim, want)
    return t


def _params(*semantics, flags=None):
    return pltpu.CompilerParams(dimension_semantics=semantics,
                                vmem_limit_bytes=VMEM_LIMIT_BYTES, flags=flags)


def _rmsnorm_kernel(x_ref, g_ref, o_ref):
    x = x_ref[...]
    ms = jnp.mean(x * x, axis=-1, keepdims=True)
    o_ref[...] = (x * lax.rsqrt(ms + NORM_EPS) * g_ref[...]).astype(o_ref.dtype)


def _rmsnorm_rows(x, g, out_dtype):
    n, d = x.shape
    tr = _tile(n, 512)
    return pl.pallas_call(
        _rmsnorm_kernel,
        grid=(n // tr,),
        in_specs=[pl.BlockSpec((tr, d), lambda i: (i, 0)),
                  pl.BlockSpec((1, d), lambda i: (0, 0))],
        out_specs=pl.BlockSpec((tr, d), lambda i: (i, 0)),
        out_shape=jax.ShapeDtypeStruct((n, d), out_dtype),
        compiler_params=_params("parallel"),
        name="rmsnorm_rows",
    )(x, g.reshape(1, d))


def _in_projection_kernel(a_ref, b_ref, o_ref):
    acc = jnp.dot(a_ref[...], b_ref[...], preferred_element_type=F32)
    for c in range(o_ref.shape[0]):
        o_ref[c] = acc[:, c * LANES:(c + 1) * LANES].astype(o_ref.dtype)


def _in_projection(h, w):
    m, k = h.shape
    n = w.shape[1]
    tm, tn = _tile(m, 1024), _tile(n, 1024)
    return pl.pallas_call(
        _in_projection_kernel,
        grid=(m // tm, n // tn),
        in_specs=[pl.BlockSpec((tm, k), lambda i, j: (i, 0)),
                  pl.BlockSpec((k, tn), lambda i, j: (0, j))],
        out_specs=pl.BlockSpec((tn // LANES, tm, LANES), lambda i, j: (j, i, 0)),
        out_shape=jax.ShapeDtypeStruct((n // LANES, m, LANES), BF16),
        compiler_params=_params("parallel", "parallel"),
        name="in_projection",
    )(h, w)


def _out_projection_kernel(a_ref, s_ref, wa_ref, ws_ref, x_ref, o_ref):
    acc = jnp.dot(a_ref[...], wa_ref[...], preferred_element_type=F32)
    acc += jnp.dot(s_ref[...], ws_ref[...], preferred_element_type=F32)
    o_ref[...] = x_ref[...] + acc


def _out_projection(att, sgu, w, x):
    m, ka = att.shape
    ks = sgu.shape[1]
    n = w.shape[1]
    assert ka == ks and w.shape[0] == ka + ks
    tm, tn = _tile(m, 1024), _tile(n, 512)
    return pl.pallas_call(
        _out_projection_kernel,
        grid=(m // tm, n // tn),
        in_specs=[pl.BlockSpec((tm, ka), lambda i, j: (i, 0)),
                  pl.BlockSpec((tm, ks), lambda i, j: (i, 0)),
                  pl.BlockSpec((ka, tn), lambda i, j: (0, j)),
                  pl.BlockSpec((ks, tn), lambda i, j: (1, j)),
                  pl.BlockSpec((tm, tn), lambda i, j: (i, j))],
        out_specs=pl.BlockSpec((tm, tn), lambda i, j: (i, j)),
        out_shape=jax.ShapeDtypeStruct((m, n), F32),
        compiler_params=_params("parallel", "parallel"),
        name="out_projection",
    )(att, sgu, w, w, x)


def _bias_tiles_kernel(tbl_ref, o_ref, *, tq, n_buckets):
    h = pl.program_id(0)
    max_exact = n_buckets // 2
    nb = tq // LANES
    key = lax.broadcasted_iota(jnp.int32, (LANES, LANES), 0)
    qry = lax.broadcasted_iota(jnp.int32, (LANES, LANES), 1)

    def lookup(dist, m):
        d = jnp.maximum(dist, 1).astype(F32)
        large = max_exact + (jnp.log(d / max_exact) / math.log(MAX_DISTANCE / max_exact)
                             * (n_buckets - max_exact)).astype(jnp.int32)
        large = jnp.minimum(large, n_buckets - 1)
        bucket = jnp.where(dist < max_exact, dist, large)
        val = jnp.zeros(dist.shape, F32)
        for b in range(n_buckets):
            val = jnp.where(bucket == b, tbl_ref[b, 2 * h + m], val)
        return jnp.where(dist >= 0, val * LOG2E, MASK_VALUE)

    for w in range(2):
        for delta in range(-(nb - 1), nb):
            offset = delta * LANES + w * tq
            for m in range(2):
                if offset + LANES - 1 < 0:
                    block = jnp.full((LANES, LANES), MASK_VALUE, F32)
                elif offset - (LANES - 1) >= MAX_DISTANCE:
                    block = jnp.full((LANES, LANES), tbl_ref[n_buckets - 1, 2 * h + m], F32) * LOG2E
                else:
                    block = lookup(qry - key + offset, m)
                for kc in range(max(0, -delta), min(nb, nb - delta)):
                    qc = kc + delta
                    o_ref[0, w, kc * LANES:(kc + 1) * LANES,
                          m * tq + qc * LANES:m * tq + (qc + 1) * LANES] = block


def _bias_tiles(rel_bias, tq):
    n_buckets, n_maps = rel_bias.shape
    n_heads = n_maps // 2
    assert tq >= MAX_DISTANCE
    return pl.pallas_call(
        functools.partial(_bias_tiles_kernel, tq=tq, n_buckets=n_buckets),
        grid=(n_heads,),
        in_specs=[pl.BlockSpec(memory_space=pltpu.SMEM)],
        out_specs=pl.BlockSpec((1, 2, tq, 2 * tq), lambda h: (h, 0, 0, 0)),
        out_shape=jax.ShapeDtypeStruct((n_heads, 2, tq, 2 * tq), F32),
        compiler_params=_params("parallel"),
        name="bias_tiles",
    )(rel_bias.astype(F32))


def _attention_kernel(far_ref, lam_ref, q_ref, k_ref, v_ref, bias_ref, g_ref, o_ref,
                      vt_ref, sa_ref, sb_ref, acc_ref, *, tq, tk, hd):
    h = pl.program_id(1)
    i = pl.program_id(2)
    _, seq, vd = v_ref.shape
    sub = tq // tk
    assert sub == 2

    @pl.when(i == 0)
    def _():
        ones_row = lax.broadcasted_iota(jnp.int32, (BF16_SUBLANES, tk), 0) == 0
        for c in range(seq // tk):
            vt_ref[c, :vd, :] = v_ref[0, c * tk:(c + 1) * tk, :].astype(F32).T.astype(BF16)
            vt_ref[c, vd:, :] = jnp.where(ones_row, 1.0, 0.0).astype(BF16)

    qt = (q_ref[0].astype(F32) * (hd ** -0.5 * LOG2E)).T
    dim = lax.broadcasted_iota(jnp.int32, qt.shape, 0)
    qqt = jnp.concatenate([jnp.where(dim < hd, qt, 0.0),
                           jnp.where(dim >= hd, qt, 0.0)], axis=1).astype(BF16)
    col = lax.broadcasted_iota(jnp.int32, (1, 2 * tq), 1)
    far = jnp.where(col < tq, far_ref[2 * h], far_ref[2 * h + 1]) * LOG2E

    def scores(kb):
        kblk = k_ref[0, pl.ds(pl.multiple_of(kb * tk, tk), tk), :]
        return jnp.dot(kblk, qqt, preferred_element_type=F32)

    def online_update(s, kb, m_prev, shift):
        m_new = jnp.maximum(m_prev, jnp.max(s, axis=0, keepdims=True) + shift)
        alpha = jnp.exp2(m_prev - m_new)
        p = jnp.exp2((s - (m_new - shift)).astype(BF16))
        acc_ref[...] = alpha * acc_ref[...] + jnp.dot(vt_ref[kb], p,
                                                      preferred_element_type=F32)
        return m_new

    kb_prev = sub * jnp.maximum(i - 1, 0)
    no_prev = jnp.where(i >= 1, 0.0, MASK_VALUE)
    n_pairs = jnp.maximum(i - 1, 0)
    last = jnp.maximum(sub * n_pairs - 1, 0)
    acc_ref[...] = jnp.zeros_like(acc_ref)
    m = jnp.full((1, 2 * tq), MASK_VALUE, F32)

    sa_ref[...] = scores(sub * i) + bias_ref[0, 0, :tk, :]
    sb_ref[...] = scores(sub * i + 1) + bias_ref[0, 0, tk:, :]
    m = online_update(sa_ref[...], sub * i, m, 0.0)
    sa_ref[...] = scores(kb_prev) + (bias_ref[0, 1, :tk, :] + no_prev)
    m = online_update(sb_ref[...], sub * i + 1, m, 0.0)
    sb_ref[...] = scores(kb_prev + 1) + (bias_ref[0, 1, tk:, :] + no_prev)
    m = online_update(sa_ref[...], kb_prev, m, 0.0)
    sa_ref[...] = scores(0)
    m = online_update(sb_ref[...], kb_prev + 1, m, 0.0)

    def far_pair(n, m_run):
        kb = sub * n
        sb_ref[...] = scores(kb + 1)
        m_run = online_update(sa_ref[...], kb, m_run, far)
        sa_ref[...] = scores(jnp.minimum(kb + 2, last))
        return online_update(sb_ref[...], kb + 1, m_run, far)

    lax.fori_loop(0, n_pairs, far_pair, m)

    acc = acc_ref[...]
    o = acc[:vd] * (1.0 / acc[vd:vd + 1])
    a = o[:, :tq] - lam_ref[0] * o[:, tq:]
    ms = jnp.mean(a * a, axis=0, keepdims=True)
    y = a * lax.rsqrt(ms + NORM_EPS) * g_ref[...] * (1.0 - LAMBDA_INIT)
    o_ref[...] = y.T.astype(o_ref.dtype)


def _diff_attention(proj, bias, far_bias, lam, subln_g, *, batch, seq, n_heads, hd, tq, tk):
    vd = subln_g.shape[-1]
    assert 2 * hd == LANES and vd == LANES
    nq = seq // tq
    k_col0 = n_heads
    v_col0 = 2 * n_heads
    grid_spec = pltpu.PrefetchScalarGridSpec(
        num_scalar_prefetch=2,
        grid=(batch, n_heads, nq),
        in_specs=[
            pl.BlockSpec((1, tq, LANES), lambda b, h, i, *_: (h, b * nq + i, 0)),
            pl.BlockSpec((1, seq, LANES), lambda b, h, i, *_: (k_col0 + h, b, 0)),
            pl.BlockSpec((1, seq, LANES), lambda b, h, i, *_: (v_col0 + h, b, 0)),
            pl.BlockSpec((1, 2, tq, 2 * tq), lambda b, h, i, *_: (h, 0, 0, 0)),
            pl.BlockSpec((vd, 1), lambda b, h, i, *_: (0, 0)),
        ],
        out_specs=pl.BlockSpec((tq, vd), lambda b, h, i, *_: (b * nq + i, h)),
        scratch_shapes=[pltpu.VMEM((seq // tk, vd + BF16_SUBLANES, tk), BF16),
                        pltpu.VMEM((tk, 2 * tq), F32),
                        pltpu.VMEM((tk, 2 * tq), F32),
                        pltpu.VMEM((vd + BF16_SUBLANES, 2 * tq), F32)],
    )
    return pl.pallas_call(
        functools.partial(_attention_kernel, tq=tq, tk=tk, hd=hd),
        grid_spec=grid_spec,
        out_shape=jax.ShapeDtypeStruct((batch * seq, n_heads * vd), BF16),
        compiler_params=_params("parallel", "parallel", "arbitrary"),
        name="diff_attention",
    )(far_bias, lam, proj, proj, proj, bias, subln_g.reshape(vd, 1))


def _gelu(x):
    return 0.5 * x * (1.0 + lax.erf(x * math.sqrt(0.5)))


def _sgu_kernel(u_ref, v_ref, lng_ref, lnb_ref, w_ref, bt_ref, o_ref):
    n_heads, chunk, hdim = u_ref.shape
    width = n_heads * hdim
    u = _gelu(u_ref[...].astype(F32))
    v = _gelu(v_ref[...].astype(F32))
    mu = jnp.sum(jnp.sum(v, axis=0), axis=-1, keepdims=True) / width
    vc = v - mu
    var = jnp.sum(jnp.sum(vc * vc, axis=0), axis=-1, keepdims=True) / width
    vn = (vc * lax.rsqrt(var + NORM_EPS) * lng_ref[...] + lnb_ref[...]).astype(BF16)
    r = lax.broadcasted_iota(jnp.int32, (chunk, chunk), 0)
    c = lax.broadcasted_iota(jnp.int32, (chunk, chunk), 1)
    causal = r >= c
    for hh in range(n_heads):
        w = jnp.where(causal, w_ref[hh], 0.0).astype(BF16)
        y = jnp.dot(w, vn[hh], preferred_element_type=F32) + bt_ref[:, hh:hh + 1]
        o_ref[:, hh * hdim:(hh + 1) * hdim] = (u[hh] * y).astype(o_ref.dtype)


def _spatial_gating(proj, ln_g, ln_b, w_s, b_s, *, u_col0):
    _, n_tok, hdim = proj.shape
    n_heads, chunk, _ = w_s.shape
    width = ln_g.shape[-1]
    assert width == n_heads * hdim and u_col0 % width == 0
    ub = u_col0 // width
    return pl.pallas_call(
        _sgu_kernel,
        grid=(n_tok // chunk,),
        in_specs=[pl.BlockSpec((n_heads, chunk, hdim), lambda i: (ub, i, 0)),
                  pl.BlockSpec((n_heads, chunk, hdim), lambda i: (ub + 1, i, 0)),
                  pl.BlockSpec((n_heads, 1, hdim), lambda i: (0, 0, 0)),
                  pl.BlockSpec((n_heads, 1, hdim), lambda i: (0, 0, 0)),
                  pl.BlockSpec((n_heads, chunk, chunk), lambda i: (0, 0, 0)),
                  pl.BlockSpec((chunk, n_heads), lambda i: (0, 0))],
        out_specs=pl.BlockSpec((chunk, width), lambda i: (i, 0)),
        out_shape=jax.ShapeDtypeStruct((n_tok, width), BF16),
        compiler_params=_params("parallel"),
        name="spatial_gating",
    )(proj, proj, ln_g.reshape(n_heads, 1, hdim), ln_b.reshape(n_heads, 1, hdim), w_s, b_s.T)


def _split_bf16(x):
    hi = x.astype(BF16)
    return hi, (x - hi.astype(F32)).astype(BF16)


def _router_kernel(x_ref, g_ref, rw_ref, rb_ref, hp_ref, idx_ref, gate_ref, rank_ref, cnt_ref):
    @pl.when(pl.program_id(0) == 0)
    def _():
        cnt_ref[...] = jnp.zeros_like(cnt_ref)

    x = x_ref[...]
    ms = jnp.mean(x * x, axis=-1, keepdims=True)
    h = x * lax.rsqrt(ms + NORM_EPS) * g_ref[...]
    half = h.shape[-1] // 2
    hp_ref[...] = _pack_bf16_pairs(h[:, :half], h[:, half:])

    h_hi, h_lo = _split_bf16(h)
    w_hi, w_lo = _split_bf16(rw_ref[...])
    logits = (jnp.dot(h_hi, w_hi, preferred_element_type=F32)
              + jnp.dot(h_lo, w_hi, preferred_element_type=F32)
              + jnp.dot(h_hi, w_lo, preferred_element_type=F32)) + rb_ref[...]
    tr, n_exp = logits.shape
    lane = lax.broadcasted_iota(jnp.int32, logits.shape, 1)
    vals, idxs = [], []
    for _ in range(TOP_K):
        top = jnp.max(logits, axis=-1, keepdims=True)
        idx = jnp.min(jnp.where(logits == top, lane, n_exp), axis=-1, keepdims=True)
        vals.append(top)
        idxs.append(idx)
        logits = jnp.where(lane == idx, -jnp.inf, logits)
    exps = [jnp.exp(v - vals[0]) for v in vals]
    denom = exps[0]
    for e in exps[1:]:
        denom = denom + e

    earlier = (lax.broadcasted_iota(jnp.int32, (tr, tr), 1)
               < lax.broadcasted_iota(jnp.int32, (tr, tr), 0))
    earlier = jnp.where(earlier, 1.0, 0.0).astype(BF16)
    count = cnt_ref[...]
    for k in range(TOP_K):
        chosen = lane == idxs[k]
        onehot = jnp.where(chosen, 1.0, 0.0)
        before = jnp.dot(earlier, onehot.astype(BF16), preferred_element_type=F32) + count
        rank = jnp.sum(jnp.where(chosen, before, 0.0), axis=-1, keepdims=True)
        idx_ref[:, k:k + 1] = idxs[k]
        gate_ref[:, k:k + 1] = exps[k] / denom
        rank_ref[:, k:k + 1] = rank.astype(jnp.int32)
        count = count + jnp.sum(onehot, axis=0, keepdims=True)
    cnt_ref[...] = count


def _router(x, g, rw, rb):
    n, d = x.shape
    n_exp = rw.shape[1]
    tr = _tile(n, 512)
    assert n * TOP_K < 2 ** 24
    return pl.pallas_call(
        _router_kernel,
        grid=(n // tr,),
        in_specs=[pl.BlockSpec((tr, d), lambda i: (i, 0)),
                  pl.BlockSpec((1, d), lambda i: (0, 0)),
                  pl.BlockSpec((d, n_exp), lambda i: (0, 0)),
                  pl.BlockSpec((1, n_exp), lambda i: (0, 0))],
        out_specs=[pl.BlockSpec((tr, d // 2), lambda i: (i, 0)),
                   pl.BlockSpec((tr, TOP_K), lambda i: (i, 0)),
                   pl.BlockSpec((tr, TOP_K), lambda i: (i, 0)),
                   pl.BlockSpec((tr, TOP_K), lambda i: (i, 0)),
                   pl.BlockSpec((1, n_exp), lambda i: (0, 0))],
        out_shape=[jax.ShapeDtypeStruct((n, d // 2), jnp.uint32),
                   jax.ShapeDtypeStruct((n, TOP_K), jnp.int32),
                   jax.ShapeDtypeStruct((n, TOP_K), F32),
                   jax.ShapeDtypeStruct((n, TOP_K), jnp.int32),
                   jax.ShapeDtypeStruct((1, n_exp), F32)],
        compiler_params=_params("arbitrary"),
        name="router",
    )(x, g.reshape(1, d), rw, rb.reshape(1, n_exp))


def _dispatch_kernel(dest_ref, pends_ref, pcnt_ref, h_ref, xs_hbm, zbuf, zsem, sem,
                     *, tb, tm, n_exp):
    s = pl.program_id(0)

    n_rows = xs_hbm.shape[0]

    def zero_block(start):
        return pltpu.make_async_copy(
            zbuf, xs_hbm.at[pl.ds(pl.multiple_of(start, tm), tm), :], zsem)

    def zero_fill(action):
        for e in range(n_exp):
            @pl.when(pcnt_ref[e] > 0)
            def _():
                action(zero_block(pends_ref[e] - tm))
        for b in range(n_exp):
            start = pends_ref[n_exp - 1] + b * tm

            @pl.when(start < n_rows)
            def _():
                action(zero_block(start))

    @pl.when(s == 0)
    def _():
        zbuf[...] = jnp.zeros_like(zbuf)
        zero_fill(lambda copy: copy.start())
        zero_fill(lambda copy: copy.wait())

    def row_copy(t, slot):
        return pltpu.make_async_copy(h_ref.at[pl.ds(t, 1), :],
                                     xs_hbm.at[pl.ds(slot, 1), :], sem)

    def issue(group, carry):
        t0 = pl.multiple_of(group * ROW_GROUP, ROW_GROUP)
        for u in range(ROW_GROUP):
            tok = s * tb + t0 + u
            for k in range(TOP_K):
                row_copy(t0 + u, dest_ref[tok * TOP_K + k]).start(priority=k % 2)
        return carry

    lax.fori_loop(0, tb // ROW_GROUP, issue, 0)

    def drain(t, carry):
        for k in range(TOP_K):
            row_copy(0, 0).wait()
        return carry

    lax.fori_loop(0, tb, drain, 0)


def _dispatch(hp, dest, pends, pcounts, *, n_rows, tm):
    n_tok, dw = hp.shape
    n_exp = pends.shape[0]
    tb = _tile(n_tok, 256)
    grid_spec = pltpu.PrefetchScalarGridSpec(
        num_scalar_prefetch=3,
        grid=(n_tok // tb,),
        in_specs=[pl.BlockSpec((tb, dw), lambda i, *_: (i, 0))],
        out_specs=pl.BlockSpec(memory_space=pl.ANY),
        scratch_shapes=[pltpu.VMEM((tm, dw), hp.dtype),
                        pltpu.SemaphoreType.DMA(()),
                        pltpu.SemaphoreType.DMA(())],
    )
    return pl.pallas_call(
        functools.partial(_dispatch_kernel, tb=tb, tm=tm, n_exp=n_exp),
        grid_spec=grid_spec,
        out_shape=jax.ShapeDtypeStruct((n_rows, dw), hp.dtype),
        compiler_params=_params("arbitrary"),
        name="dispatch",
    )(dest, pends, pcounts, hp)


def _pack_bf16_pairs(hi, lo):
    hb = pltpu.bitcast(hi.astype(BF16).astype(F32), jnp.uint32)
    lb = pltpu.bitcast(lo.astype(BF16).astype(F32), jnp.uint32)
    return hb | (lb >> 16)


def _unpack_pairs_f32(p):
    return (pltpu.bitcast(p & jnp.uint32(0xFFFF0000), F32), pltpu.bitcast(p << 16, F32))


def _expert_block(be_ref, nv_ref, chg_ref, nxt_ref, used_ref, w_hbm, stage, sems, w_bf, o_ref,
                  compute, *, half_stride, tn, nj):
    j = pl.program_id(0)
    i = pl.program_id(1)
    change = chg_ref[i]
    valid = i < nv_ref[0]
    tm = o_ref.shape[0]
    steady = jnp.logical_and(valid, change == 0)
    small = used_ref[i] <= tm // 2

    def copies(e, jj):
        return [pltpu.make_async_copy(
            w_hbm.at[e, :, pl.ds(pl.multiple_of(part * half_stride + jj * tn, tn), tn)],
            stage.at[part], sems.at[part]) for part in range(2)]

    @pl.when(change > 0)
    def _():
        @pl.when(jnp.logical_and(j == 0, change == 1))
        def _():
            for cp in copies(be_ref[i], j):
                cp.start()

        for cp in copies(be_ref[i], j):
            cp.wait()

        def convert(part, rows):
            w = stage[part, rows, :].astype(BF16)
            w_bf[part, rows, :] = w
            return w

        compute(convert, tm)

        last = nxt_ref[i] < 0
        next_e = jnp.where(last, be_ref[0], nxt_ref[i])
        next_j = jnp.where(last, j + 1, j)

        @pl.when(next_j < nj)
        def _():
            for cp in copies(next_e, next_j):
                cp.start(priority=1)

    def current(part, rows):
        return w_bf[part, rows, :]

    @pl.when(jnp.logical_and(steady, jnp.logical_not(small)))
    def _():
        compute(current, tm)

    @pl.when(jnp.logical_and(steady, small))
    def _():
        compute(current, tm // 2)
        o_ref[tm // 2:, :] = jnp.zeros((tm - tm // 2, o_ref.shape[1]), o_ref.dtype)

    @pl.when(jnp.logical_not(valid))
    def _():
        o_ref[...] = jnp.zeros_like(o_ref)


def _expert_up_kernel(be_ref, nv_ref, chg_ref, nxt_ref, used_ref, xs_ref, w_hbm, bg_ref, bu_ref,
                      o_ref, stage, w_bf, sems, *, ff, tn, nj):
    def compute(get_w, n_rows):
        x_hi, x_lo = _unpack_pairs_f32(xs_ref[:n_rows, :])
        x_hi, x_lo = x_hi.astype(BF16), x_lo.astype(BF16)
        half = x_hi.shape[-1]
        top, bottom = slice(0, half), slice(half, 2 * half)
        gate = (jnp.dot(x_hi, get_w(0, top), preferred_element_type=F32)
                + jnp.dot(x_lo, get_w(0, bottom), preferred_element_type=F32) + bg_ref[0])
        up = (jnp.dot(x_hi, get_w(1, top), preferred_element_type=F32)
              + jnp.dot(x_lo, get_w(1, bottom), preferred_element_type=F32) + bu_ref[0])
        gate = jnp.minimum(gate, SWIGLU_LIMIT)
        up = jnp.clip(up, -SWIGLU_LIMIT, SWIGLU_LIMIT)
        act = (up + 1.0) * gate * (1.0 / (1.0 + jnp.exp(-SWIGLU_ALPHA * gate)))
        o_ref[:n_rows, :] = act.astype(o_ref.dtype)

    _expert_block(be_ref, nv_ref, chg_ref, nxt_ref, used_ref, w_hbm, stage, sems, w_bf, o_ref,
                  compute, half_stride=ff, tn=tn, nj=nj)


def _expert_scratch(k, tn):
    return [pltpu.VMEM((2, k, tn), F32),
            pltpu.VMEM((2, k, tn), BF16),
            pltpu.SemaphoreType.DMA((2,))]


def _expert_up(xs, meta, w_gu, b_gu, *, tm):
    n_rows, dw = xs.shape
    n_exp, d, ff2 = w_gu.shape
    ff = ff2 // 2
    assert d == 2 * dw
    tn = _tile(ff, 512)
    nj = ff // tn
    n_blocks = n_rows // tm
    b3 = b_gu.reshape(n_exp, 1, ff2)
    grid_spec = pltpu.PrefetchScalarGridSpec(
        num_scalar_prefetch=5,
        grid=(nj, n_blocks),
        in_specs=[
            pl.BlockSpec((tm, dw), lambda j, i, be, nv, *_: (jnp.minimum(i, nv[0] - 1), 0)),
            pl.BlockSpec(memory_space=pl.ANY),
            pl.BlockSpec((1, 1, tn), lambda j, i, be, *_: (be[i], 0, j)),
            pl.BlockSpec((1, 1, tn), lambda j, i, be, *_: (be[i], 0, nj + j)),
        ],
        out_specs=pl.BlockSpec((tm, tn), lambda j, i, *_: (i, j)),
        scratch_shapes=_expert_scratch(d, tn),
    )
    return pl.pallas_call(
        functools.partial(_expert_up_kernel, ff=ff, tn=tn, nj=nj),
        grid_spec=grid_spec,
        out_shape=jax.ShapeDtypeStruct((n_rows, ff), BF16),
        compiler_params=_params("arbitrary", "arbitrary"),
        name="expert_up",
    )(*meta, xs, w_gu, b3, b3)


def _expert_down_kernel(be_ref, nv_ref, chg_ref, nxt_ref, used_ref, h_ref, w_hbm, ba_ref, bb_ref,
                        o_ref, stage, w_bf, sems, *, half, tn, nj):
    def compute(get_w, n_rows):
        k_half = h_ref.shape[-1] // 2
        top, bottom = slice(0, k_half), slice(k_half, 2 * k_half)
        h_top, h_bottom = h_ref[:n_rows, top], h_ref[:n_rows, bottom]
        ya = (jnp.dot(h_top, get_w(0, top), preferred_element_type=F32)
              + jnp.dot(h_bottom, get_w(0, bottom), preferred_element_type=F32) + ba_ref[0])
        yb = (jnp.dot(h_top, get_w(1, top), preferred_element_type=F32)
              + jnp.dot(h_bottom, get_w(1, bottom), preferred_element_type=F32) + bb_ref[0])
        o_ref[:n_rows, :] = _pack_bf16_pairs(ya, yb)

    _expert_block(be_ref, nv_ref, chg_ref, nxt_ref, used_ref, w_hbm, stage, sems, w_bf, o_ref,
                  compute, half_stride=half, tn=tn, nj=nj)


def _expert_down(hs, meta, w_dn, b_dn, *, tm):
    n_rows, ff = hs.shape
    n_exp, _, d = w_dn.shape
    half = d // 2
    tn = _tile(half, 1024)
    nj = half // tn
    n_blocks = n_rows // tm
    b3 = b_dn.reshape(n_exp, 1, d)
    grid_spec = pltpu.PrefetchScalarGridSpec(
        num_scalar_prefetch=5,
        grid=(nj, n_blocks),
        in_specs=[
            pl.BlockSpec((tm, ff), lambda j, i, be, nv, *_: (jnp.minimum(i, nv[0] - 1), 0)),
            pl.BlockSpec(memory_space=pl.ANY),
            pl.BlockSpec((1, 1, tn), lambda j, i, be, *_: (be[i], 0, j)),
            pl.BlockSpec((1, 1, tn), lambda j, i, be, *_: (be[i], 0, nj + j)),
        ],
        out_specs=pl.BlockSpec((tm, tn), lambda j, i, *_: (i, j)),
        scratch_shapes=_expert_scratch(ff, tn),
    )
    return pl.pallas_call(
        functools.partial(_expert_down_kernel, half=half, tn=tn, nj=nj),
        grid_spec=grid_spec,
        out_shape=jax.ShapeDtypeStruct((n_rows, half), jnp.uint32),
        compiler_params=_params("arbitrary", "arbitrary"),
        name="expert_down",
    )(*meta, hs, w_dn, b3, b3)


def _combine_kernel(dest_ref, x_ref, gate_ref, g_ref, ys_hbm, o_ref, buf, sems, *, tb, n_steps):
    s = pl.program_id(0)
    cur = s & 1

    def row_copy(row, buf_slot, k, t):
        return pltpu.make_async_copy(ys_hbm.at[pl.ds(row, 1), :],
                                     buf.at[buf_slot, k, pl.ds(t, 1), :], sems.at[buf_slot])

    def gather(step, buf_slot):
        def issue(group, carry):
            t0 = pl.multiple_of(group * ROW_GROUP, ROW_GROUP)
            for u in range(ROW_GROUP):
                tok = step * tb + t0 + u
                for k in range(TOP_K):
                    row_copy(dest_ref[tok * TOP_K + k], buf_slot, k, t0 + u).start(
                        priority=k % 2)
            return carry

        lax.fori_loop(0, tb // ROW_GROUP, issue, 0)

    @pl.when(s == 0)
    def _():
        gather(0, 0)

    @pl.when(s + 1 < n_steps)
    def _():
        gather(s + 1, 1 - cur)

    def drain(t, carry):
        for k in range(TOP_K):
            row_copy(0, cur, 0, 0).wait()
        return carry

    lax.fori_loop(0, tb, drain, 0)

    half = x_ref.shape[-1] // 2
    acc_hi = x_ref[:, :half]
    acc_lo = x_ref[:, half:]
    for k in range(TOP_K):
        y_hi, y_lo = _unpack_pairs_f32(buf[cur, k])
        gate = gate_ref[:, k:k + 1]
        acc_hi = acc_hi + gate * y_hi
        acc_lo = acc_lo + gate * y_lo
    ms = (jnp.sum(acc_hi * acc_hi, axis=-1, keepdims=True)
          + jnp.sum(acc_lo * acc_lo, axis=-1, keepdims=True)) / (2 * half)
    r = lax.rsqrt(ms + NORM_EPS)
    o_ref[:, :half] = acc_hi * r * g_ref[:, :half]
    o_ref[:, half:] = acc_lo * r * g_ref[:, half:]


def _combine(x, gates, dest, ys, g):
    n_tok, d = x.shape
    tb = _tile(n_tok, 256)
    n_steps = n_tok // tb
    grid_spec = pltpu.PrefetchScalarGridSpec(
        num_scalar_prefetch=1,
        grid=(n_steps,),
        in_specs=[pl.BlockSpec((tb, d), lambda i, dest: (i, 0)),
                  pl.BlockSpec((tb, TOP_K), lambda i, dest: (i, 0)),
                  pl.BlockSpec((1, d), lambda i, dest: (0, 0)),
                  pl.BlockSpec(memory_space=pl.ANY)],
        out_specs=pl.BlockSpec((tb, d), lambda i, dest: (i, 0)),
        scratch_shapes=[pltpu.VMEM((2, TOP_K, tb, d // 2), ys.dtype),
                        pltpu.SemaphoreType.DMA((2,))],
    )
    return pl.pallas_call(
        functools.partial(_combine_kernel, tb=tb, n_steps=n_steps),
        grid_spec=grid_spec,
        out_shape=jax.ShapeDtypeStruct((n_tok, d), F32),
        compiler_params=_params("arbitrary"),
        name="combine",
    )(dest, x, gates, g.reshape(1, d), ys)


def _routing_tables(top_idx, rank, counts, tm):
    n_exp = counts.shape[0]
    flat_e = top_idx.reshape(-1)
    n_pairs = flat_e.shape[0]
    pcounts = (counts + tm - 1) // tm * tm
    pends = jnp.cumsum(pcounts)
    pstarts = pends - pcounts
    onehot = flat_e[:, None] == jnp.arange(n_exp, dtype=jnp.int32)[None, :]
    dest = jnp.sum(jnp.where(onehot, pstarts[None, :], 0), axis=1) + rank.reshape(-1)
    assert n_pairs % tm == 0
    n_blocks = n_pairs // tm + n_exp
    block_id = jnp.arange(n_blocks, dtype=jnp.int32)
    block_e = jnp.minimum(jnp.sum(pends[None, :] <= (block_id * tm)[:, None], axis=1),
                          n_exp - 1).astype(jnp.int32)
    n_valid = (pends[-1] // tm).astype(jnp.int32)
    prev_e = jnp.concatenate([jnp.full((1,), -1, jnp.int32), block_e[:-1]])
    is_change = (block_id < n_valid) & (block_e != prev_e)
    change_ord = jnp.cumsum(is_change.astype(jnp.int32))
    change = jnp.where(is_change, change_ord, 0).astype(jnp.int32)
    later = is_change[None, :] & (block_id[None, :] > block_id[:, None])
    next_e = jnp.where(jnp.any(later, axis=1), block_e[jnp.argmax(later, axis=1)], -1)
    real_end = pstarts + counts
    used = jnp.clip(real_end[block_e] - block_id * tm, 0, tm).astype(jnp.int32)
    meta = (block_e, n_valid.reshape(1), change, next_e.astype(jnp.int32), used)
    return (dest.astype(jnp.int32), pends.astype(jnp.int32), pcounts.astype(jnp.int32),
            meta, n_blocks * tm)


def kernel(x, attn_norm_g, w_in, lambda_q1, lambda_k1, lambda_q2, lambda_k2, diff_subln_g,
           sgu_ln_g, sgu_ln_b, sgu_w, sgu_b, rel_bias, w_out, ffn_norm_g, router_w, router_b,
           w_gate_up, b_gate_up, w_down, b_down, final_norm_g):
    batch, seq, d = x.shape
    n_tok = batch * seq
    hd = lambda_q1.shape[-1]
    n_maps = rel_bias.shape[1]
    n_heads = n_maps // 2
    q_cols = n_maps * hd
    attn_width = n_heads * diff_subln_g.shape[-1]
    n_exp = w_gate_up.shape[1]
    tq = _tile(seq, 512)
    tk = tq // 2
    tm = 512

    xt = x.reshape(n_tok, d)
    h = _rmsnorm_rows(xt, attn_norm_g[0], BF16)
    proj = _in_projection(h, w_in[0].astype(BF16))

    lam = (jnp.exp(jnp.sum(lambda_q1[0].astype(F32) * lambda_k1[0].astype(F32)))
           - jnp.exp(jnp.sum(lambda_q2[0].astype(F32) * lambda_k2[0].astype(F32)))
           + LAMBDA_INIT).reshape(1)
    bias = _bias_tiles(rel_bias, tq)
    far_bias = rel_bias[-1].astype(F32)
    att = _diff_attention(proj, bias, far_bias, lam, diff_subln_g[0],
                          batch=batch, seq=seq, n_heads=n_heads, hd=hd, tq=tq, tk=tk)
    sgu = _spatial_gating(proj, sgu_ln_g[0], sgu_ln_b[0], sgu_w[0], sgu_b[0],
                          u_col0=2 * q_cols + attn_width)
    x1 = _out_projection(att, sgu, w_out[0].astype(BF16), xt)

    hp, top_idx, gates, rank, counts = _router(x1, ffn_norm_g[0], router_w[0], router_b[0])
    dest, pends, pcounts, meta, n_rows = _routing_tables(
        top_idx, rank, counts[0].astype(jnp.int32), tm)
    xs = _dispatch(hp, dest, pends, pcounts, n_rows=n_rows, tm=tm)
    hs = _expert_up(xs, meta, w_gate_up[0], b_gate_up[0], tm=tm)
    ys = _expert_down(hs, meta, w_down[0], b_down[0], tm=tm)
    out = _combine(x1, gates, dest, ys, final_norm_g)
    return out.reshape(batch, seq, d)
```

```python
import functools
import math

import jax
import jax.numpy as jnp
from jax import lax
from jax.experimental import pallas as pl
from jax.experimental.pallas import tpu as pltpu

F32 = jnp.float32
BF16 = jnp.bfloat16

TOP_K = 4
MAX_DISTANCE = 128
SWIGLU_LIMIT = 7.0
SWIGLU_ALPHA = 1.702
NORM_EPS = 1e-5
LAMBDA_INIT = 0.8 - 0.6 * math.exp(0.0)
MASK_VALUE = -1e30
LOG2E = math.log2(math.e)

LANES = 128
BF16_SUBLANES = 16
ROW_GROUP = 8
TAIL_LEVELS = 4
VMEM_LIMIT_BYTES = 56 * 1024 * 1024


def _tile(dim, want):
    t = min(dim, want)
    while dim % t:
        t -= LANES
    assert t > 0, (dim, want)
    return t


def _params(*semantics, flags=None):
    return pltpu.CompilerParams(dimension_semantics=semantics,
                                vmem_limit_bytes=VMEM_LIMIT_BYTES, flags=flags)


def _rmsnorm_kernel(x_ref, g_ref, o_ref):
    x = x_ref[...]
    ms = jnp.mean(x * x, axis=-1, keepdims=True)
    o_ref[...] = (x * lax.rsqrt(ms + NORM_EPS) * g_ref[...]).astype(o_ref.dtype)


def _rmsnorm_rows(x, g, out_dtype):
    n, d = x.shape
    tr = _tile(n, 512)
    return pl.pallas_call(
        _rmsnorm_kernel,
        grid=(n // tr,),
        in_specs=[pl.BlockSpec((tr, d), lambda i: (i, 0)),
                  pl.BlockSpec((1, d), lambda i: (0, 0))],
        out_specs=pl.BlockSpec((tr, d), lambda i: (i, 0)),
        out_shape=jax.ShapeDtypeStruct((n, d), out_dtype),
        compiler_params=_params("parallel"),
        name="rmsnorm_rows",
    )(x, g.reshape(1, d))


def _in_projection_kernel(a_ref, b_ref, o_ref):
    acc = jnp.dot(a_ref[...], b_ref[...], preferred_element_type=F32)
    for c in range(o_ref.shape[0]):
        o_ref[c] = acc[:, c * LANES:(c + 1) * LANES].astype(o_ref.dtype)


def _in_projection(h, w):
    m, k = h.shape
    n = w.shape[1]
    tm, tn = _tile(m, 1024), _tile(n, 1024)
    return pl.pallas_call(
        _in_projection_kernel,
        grid=(m // tm, n // tn),
        in_specs=[pl.BlockSpec((tm, k), lambda i, j: (i, 0)),
                  pl.BlockSpec((k, tn), lambda i, j: (0, j))],
        out_specs=pl.BlockSpec((tn // LANES, tm, LANES), lambda i, j: (j, i, 0)),
        out_shape=jax.ShapeDtypeStruct((n // LANES, m, LANES), BF16),
        compiler_params=_params("parallel", "parallel"),
        name="in_projection",
    )(h, w)


def _out_projection_kernel(a_ref, s_ref, wa_ref, ws_ref, x_ref, o_ref):
    acc = jnp.dot(a_ref[...], wa_ref[...], preferred_element_type=F32)
    acc += jnp.dot(s_ref[...], ws_ref[...], preferred_element_type=F32)
    o_ref[...] = x_ref[...] + acc


def _out_projection(att, sgu, w, x):
    m, ka = att.shape
    ks = sgu.shape[1]
    n = w.shape[1]
    assert ka == ks and w.shape[0] == ka + ks
    tm, tn = _tile(m, 1024), _tile(n, 512)
    return pl.pallas_call(
        _out_projection_kernel,
        grid=(m // tm, n // tn),
        in_specs=[pl.BlockSpec((tm, ka), lambda i, j: (i, 0)),
                  pl.BlockSpec((tm, ks), lambda i, j: (i, 0)),
                  pl.BlockSpec((ka, tn), lambda i, j: (0, j)),
                  pl.BlockSpec((ks, tn), lambda i, j: (1, j)),
                  pl.BlockSpec((tm, tn), lambda i, j: (i, j))],
        out_specs=pl.BlockSpec((tm, tn), lambda i, j: (i, j)),
        out_shape=jax.ShapeDtypeStruct((m, n), F32),
        compiler_params=_params("parallel", "parallel"),
        name="out_projection",
    )(att, sgu, w, w, x)


def _bias_tiles_kernel(tbl_ref, o_ref, *, tq, n_buckets):
    h = pl.program_id(0)
    max_exact = n_buckets // 2
    nb = tq // LANES
    key = lax.broadcasted_iota(jnp.int32, (LANES, LANES), 0)
    qry = lax.broadcasted_iota(jnp.int32, (LANES, LANES), 1)

    def lookup(dist, m):
        d = jnp.maximum(dist, 1).astype(F32)
        large = max_exact + (jnp.log(d / max_exact) / math.log(MAX_DISTANCE / max_exact)
                             * (n_buckets - max_exact)).astype(jnp.int32)
        large = jnp.minimum(large, n_buckets - 1)
        bucket = jnp.where(dist < max_exact, dist, large)
        val = jnp.zeros(dist.shape, F32)
        for b in range(n_buckets):
            val = jnp.where(bucket == b, tbl_ref[b, 2 * h + m], val)
        return jnp.where(dist >= 0, val * LOG2E, MASK_VALUE)

    for w in range(2):
        for delta in range(-(nb - 1), nb):
            offset = delta * LANES + w * tq
            for m in range(2):
                if offset + LANES - 1 < 0:
                    block = jnp.full((LANES, LANES), MASK_VALUE, F32)
                elif offset - (LANES - 1) >= MAX_DISTANCE:
                    block = jnp.full((LANES, LANES), tbl_ref[n_buckets - 1, 2 * h + m], F32) * LOG2E
                else:
                    block = lookup(qry - key + offset, m)
                for kc in range(max(0, -delta), min(nb, nb - delta)):
                    qc = kc + delta
                    o_ref[0, w, kc * LANES:(kc + 1) * LANES,
                          m * tq + qc * LANES:m * tq + (qc + 1) * LANES] = block


def _bias_tiles(rel_bias, tq):
    n_buckets, n_maps = rel_bias.shape
    n_heads = n_maps // 2
    assert tq >= MAX_DISTANCE
    return pl.pallas_call(
        functools.partial(_bias_tiles_kernel, tq=tq, n_buckets=n_buckets),
        grid=(n_heads,),
        in_specs=[pl.BlockSpec(memory_space=pltpu.SMEM)],
        out_specs=pl.BlockSpec((1, 2, tq, 2 * tq), lambda h: (h, 0, 0, 0)),
        out_shape=jax.ShapeDtypeStruct((n_heads, 2, tq, 2 * tq), F32),
        compiler_params=_params("parallel"),
        name="bias_tiles",
    )(rel_bias.astype(F32))


def _attention_kernel(far_ref, lam_ref, q_ref, k_ref, v_ref, bias_ref, g_ref, o_ref,
                      vt_ref, sa_ref, sb_ref, acc_ref, *, tq, tk, hd):
    i = pl.program_id(2)
    n_group, seq, vd = v_ref.shape
    head0 = pl.program_id(1) * n_group
    group = range(n_group)
    sub = tq // tk
    assert sub == 2

    @pl.when(i == 0)
    def _():
        ones_row = lax.broadcasted_iota(jnp.int32, (BF16_SUBLANES, tk), 0) == 0
        for g in group:
            for c in range(seq // tk):
                vt_ref[g, c, :vd, :] = (
                    v_ref[g, c * tk:(c + 1) * tk, :].astype(F32).T.astype(BF16))
                vt_ref[g, c, vd:, :] = jnp.where(ones_row, 1.0, 0.0).astype(BF16)

    col = lax.broadcasted_iota(jnp.int32, (1, 2 * tq), 1)
    qqt, far = [], []
    for g in group:
        qt = (q_ref[g].astype(F32) * (hd ** -0.5 * LOG2E)).T
        dim = lax.broadcasted_iota(jnp.int32, qt.shape, 0)
        qqt.append(jnp.concatenate([jnp.where(dim < hd, qt, 0.0),
                                    jnp.where(dim >= hd, qt, 0.0)], axis=1).astype(BF16))
        head = head0 + g
        far.append(jnp.where(col < tq, far_ref[2 * head], far_ref[2 * head + 1]) * LOG2E)

    def scores(g, kb):
        kblk = k_ref[g, pl.ds(pl.multiple_of(kb * tk, tk), tk), :]
        return jnp.dot(kblk, qqt[g], preferred_element_type=F32)

    def online_update(g, s, kb, m_prev, shift):
        m_new = jnp.maximum(m_prev, jnp.max(s, axis=0, keepdims=True) + shift)
        alpha = jnp.exp2(m_prev - m_new)
        p = jnp.exp2((s - (m_new - shift)).astype(BF16))
        acc_ref[g] = alpha * acc_ref[g] + jnp.dot(vt_ref[g, kb], p, preferred_element_type=F32)
        return m_new

    kb_prev = sub * jnp.maximum(i - 1, 0)
    no_prev = jnp.where(i >= 1, 0.0, MASK_VALUE)
    n_pairs = jnp.maximum(i - 1, 0)
    last = jnp.maximum(sub * n_pairs - 1, 0)
    acc_ref[...] = jnp.zeros_like(acc_ref)
    m = [jnp.full((1, 2 * tq), MASK_VALUE, F32) for _ in group]

    for g in group:
        sa_ref[g] = scores(g, sub * i) + bias_ref[g, 0, :tk, :]
    for g in group:
        sb_ref[g] = scores(g, sub * i + 1) + bias_ref[g, 0, tk:, :]
        m[g] = online_update(g, sa_ref[g], sub * i, m[g], 0.0)
    for g in group:
        sa_ref[g] = scores(g, kb_prev) + (bias_ref[g, 1, :tk, :] + no_prev)
        m[g] = online_update(g, sb_ref[g], sub * i + 1, m[g], 0.0)
    for g in group:
        sb_ref[g] = scores(g, kb_prev + 1) + (bias_ref[g, 1, tk:, :] + no_prev)
        m[g] = online_update(g, sa_ref[g], kb_prev, m[g], 0.0)
    for g in group:
        sa_ref[g] = scores(g, 0)
        m[g] = online_update(g, sb_ref[g], kb_prev + 1, m[g], 0.0)

    def far_pair(n, m_run):
        m_run = list(m_run)
        kb = sub * n
        for g in group:
            sb_ref[g] = scores(g, kb + 1)
            m_run[g] = online_update(g, sa_ref[g], kb, m_run[g], far[g])
        for g in group:
            sa_ref[g] = scores(g, jnp.minimum(kb + 2, last))
            m_run[g] = online_update(g, sb_ref[g], kb + 1, m_run[g], far[g])
        return tuple(m_run)

    lax.fori_loop(0, n_pairs, far_pair, tuple(m))

    for g in group:
        acc = acc_ref[g]
        o = acc[:vd] * (1.0 / acc[vd:vd + 1])
        a = o[:, :tq] - lam_ref[0] * o[:, tq:]
        ms = jnp.mean(a * a, axis=0, keepdims=True)
        y = a * lax.rsqrt(ms + NORM_EPS) * g_ref[...] * (1.0 - LAMBDA_INIT)
        o_ref[:, g * vd:(g + 1) * vd] = y.T.astype(o_ref.dtype)


def _diff_attention(proj, bias, far_bias, lam, subln_g, *, batch, seq, n_heads, hd, tq, tk):
    vd = subln_g.shape[-1]
    assert 2 * hd == LANES and vd == LANES
    nq = seq // tq
    n_group = 2
    assert n_heads % n_group == 0
    k_blk0 = n_heads // n_group
    v_blk0 = 2 * n_heads // n_group
    grid_spec = pltpu.PrefetchScalarGridSpec(
        num_scalar_prefetch=2,
        grid=(batch, n_heads // n_group, nq),
        in_specs=[
            pl.BlockSpec((n_group, tq, LANES), lambda b, h, i, *_: (h, b * nq + i, 0)),
            pl.BlockSpec((n_group, seq, LANES), lambda b, h, i, *_: (k_blk0 + h, b, 0)),
            pl.BlockSpec((n_group, seq, LANES), lambda b, h, i, *_: (v_blk0 + h, b, 0)),
            pl.BlockSpec((n_group, 2, tq, 2 * tq), lambda b, h, i, *_: (h, 0, 0, 0)),
            pl.BlockSpec((vd, 1), lambda b, h, i, *_: (0, 0)),
        ],
        out_specs=pl.BlockSpec((tq, n_group * vd), lambda b, h, i, *_: (b * nq + i, h)),
        scratch_shapes=[pltpu.VMEM((n_group, seq // tk, vd + BF16_SUBLANES, tk), BF16),
                        pltpu.VMEM((n_group, tk, 2 * tq), F32),
                        pltpu.VMEM((n_group, tk, 2 * tq), F32),
                        pltpu.VMEM((n_group, vd + BF16_SUBLANES, 2 * tq), F32)],
    )
    return pl.pallas_call(
        functools.partial(_attention_kernel, tq=tq, tk=tk, hd=hd),
        grid_spec=grid_spec,
        out_shape=jax.ShapeDtypeStruct((batch * seq, n_heads * vd), BF16),
        compiler_params=_params("parallel", "parallel", "arbitrary"),
        name="diff_attention",
    )(far_bias, lam, proj, proj, proj, bias, subln_g.reshape(vd, 1))


def _gelu(x):
    return 0.5 * x * (1.0 + lax.erf(x * math.sqrt(0.5)))


def _sgu_kernel(u_ref, v_ref, lng_ref, lnb_ref, w_ref, bt_ref, o_ref):
    n_heads, rows, hdim = u_ref.shape
    chunk = w_ref.shape[-1]
    width = n_heads * hdim
    u = _gelu(u_ref[...].astype(F32))
    v = _gelu(v_ref[...].astype(F32))
    mu = jnp.sum(jnp.sum(v, axis=0), axis=-1, keepdims=True) / width
    vc = v - mu
    var = jnp.sum(jnp.sum(vc * vc, axis=0), axis=-1, keepdims=True) / width
    vn = (vc * lax.rsqrt(var + NORM_EPS) * lng_ref[...] + lnb_ref[...]).astype(BF16)
    r = lax.broadcasted_iota(jnp.int32, (chunk, chunk), 0)
    c = lax.broadcasted_iota(jnp.int32, (chunk, chunk), 1)
    causal = r >= c
    for hh in range(n_heads):
        w = jnp.where(causal, w_ref[hh], 0.0).astype(BF16)
        for c0 in range(0, rows, chunk):
            sl = slice(c0, c0 + chunk)
            y = jnp.dot(w, vn[hh, sl], preferred_element_type=F32) + bt_ref[:, hh:hh + 1]
            o_ref[sl, hh * hdim:(hh + 1) * hdim] = (u[hh, sl] * y).astype(o_ref.dtype)


def _spatial_gating(proj, ln_g, ln_b, w_s, b_s, *, u_col0):
    _, n_tok, hdim = proj.shape
    n_heads, chunk, _ = w_s.shape
    width = ln_g.shape[-1]
    assert width == n_heads * hdim and u_col0 % width == 0
    ub = u_col0 // width
    rows = chunk * min(4, n_tok // chunk)
    assert n_tok % rows == 0
    return pl.pallas_call(
        _sgu_kernel,
        grid=(n_tok // rows,),
        in_specs=[pl.BlockSpec((n_heads, rows, hdim), lambda i: (ub, i, 0)),
                  pl.BlockSpec((n_heads, rows, hdim), lambda i: (ub + 1, i, 0)),
                  pl.BlockSpec((n_heads, 1, hdim), lambda i: (0, 0, 0)),
                  pl.BlockSpec((n_heads, 1, hdim), lambda i: (0, 0, 0)),
                  pl.BlockSpec((n_heads, chunk, chunk), lambda i: (0, 0, 0)),
                  pl.BlockSpec((chunk, n_heads), lambda i: (0, 0))],
        out_specs=pl.BlockSpec((rows, width), lambda i: (i, 0)),
        out_shape=jax.ShapeDtypeStruct((n_tok, width), BF16),
        compiler_params=_params("parallel"),
        name="spatial_gating",
    )(proj, proj, ln_g.reshape(n_heads, 1, hdim), ln_b.reshape(n_heads, 1, hdim), w_s, b_s.T)


def _split_bf16(x):
    hi = x.astype(BF16)
    return hi, (x - hi.astype(F32)).astype(BF16)


def _router_kernel(x_ref, g_ref, rw_ref, rb_ref, hp_ref, idx_ref, gate_ref, rank_ref, cnt_ref):
    @pl.when(pl.program_id(0) == 0)
    def _():
        cnt_ref[...] = jnp.zeros_like(cnt_ref)

    x = x_ref[...]
    ms = jnp.mean(x * x, axis=-1, keepdims=True)
    h = x * lax.rsqrt(ms + NORM_EPS) * g_ref[...]
    half = h.shape[-1] // 2
    hp_ref[...] = _pack_bf16_pairs(h[:, :half], h[:, half:])

    h_hi, h_lo = _split_bf16(h)
    w_hi, w_lo = _split_bf16(rw_ref[...])
    logits = (jnp.dot(h_hi, w_hi, preferred_element_type=F32)
              + jnp.dot(h_lo, w_hi, preferred_element_type=F32)
              + jnp.dot(h_hi, w_lo, preferred_element_type=F32)) + rb_ref[...]
    tr, n_exp = logits.shape
    lane = lax.broadcasted_iota(jnp.int32, logits.shape, 1)
    vals, idxs = [], []
    for _ in range(TOP_K):
        top = jnp.max(logits, axis=-1, keepdims=True)
        idx = jnp.min(jnp.where(logits == top, lane, n_exp), axis=-1, keepdims=True)
        vals.append(top)
        idxs.append(idx)
        logits = jnp.where(lane == idx, -jnp.inf, logits)
    exps = [jnp.exp(v - vals[0]) for v in vals]
    denom = exps[0]
    for e in exps[1:]:
        denom = denom + e

    earlier = (lax.broadcasted_iota(jnp.int32, (tr, tr), 1)
               < lax.broadcasted_iota(jnp.int32, (tr, tr), 0))
    earlier = jnp.where(earlier, 1.0, 0.0).astype(BF16)
    count = cnt_ref[...]
    for k in range(TOP_K):
        chosen = lane == idxs[k]
        onehot = jnp.where(chosen, 1.0, 0.0)
        before = jnp.dot(earlier, onehot.astype(BF16), preferred_element_type=F32) + count
        rank = jnp.sum(jnp.where(chosen, before, 0.0), axis=-1, keepdims=True)
        idx_ref[:, k:k + 1] = idxs[k]
        gate_ref[:, k:k + 1] = exps[k] / denom
        rank_ref[:, k:k + 1] = rank.astype(jnp.int32)
        count = count + jnp.sum(onehot, axis=0, keepdims=True)
    cnt_ref[...] = count


def _router(x, g, rw, rb):
    n, d = x.shape
    n_exp = rw.shape[1]
    tr = _tile(n, 512)
    assert n * TOP_K < 2 ** 24
    return pl.pallas_call(
        _router_kernel,
        grid=(n // tr,),
        in_specs=[pl.BlockSpec((tr, d), lambda i: (i, 0)),
                  pl.BlockSpec((1, d), lambda i: (0, 0)),
                  pl.BlockSpec((d, n_exp), lambda i: (0, 0)),
                  pl.BlockSpec((1, n_exp), lambda i: (0, 0))],
        out_specs=[pl.BlockSpec((tr, d // 2), lambda i: (i, 0)),
                   pl.BlockSpec((tr, TOP_K), lambda i: (i, 0)),
                   pl.BlockSpec((tr, TOP_K), lambda i: (i, 0)),
                   pl.BlockSpec((tr, TOP_K), lambda i: (i, 0)),
                   pl.BlockSpec((1, n_exp), lambda i: (0, 0))],
        out_shape=[jax.ShapeDtypeStruct((n, d // 2), jnp.uint32),
                   jax.ShapeDtypeStruct((n, TOP_K), jnp.int32),
                   jax.ShapeDtypeStruct((n, TOP_K), F32),
                   jax.ShapeDtypeStruct((n, TOP_K), jnp.int32),
                   jax.ShapeDtypeStruct((1, n_exp), F32)],
        compiler_params=_params("arbitrary"),
        name="router",
    )(x, g.reshape(1, d), rw, rb.reshape(1, n_exp))


def _dispatch_kernel(dest_ref, pends_ref, pcnt_ref, h_ref, xs_hbm, zbuf, zsem, sem,
                     *, tb, tm, n_exp):
    s = pl.program_id(0)

    n_rows = xs_hbm.shape[0]

    def zero_block(start):
        return pltpu.make_async_copy(
            zbuf, xs_hbm.at[pl.ds(pl.multiple_of(start, tm), tm), :], zsem)

    def zero_fill(action):
        for e in range(n_exp):
            @pl.when(pcnt_ref[e] > 0)
            def _():
                action(zero_block(pends_ref[e] - tm))
        for b in range(n_exp):
            start = pends_ref[n_exp - 1] + b * tm

            @pl.when(start < n_rows)
            def _():
                action(zero_block(start))

    @pl.when(s == 0)
    def _():
        zbuf[...] = jnp.zeros_like(zbuf)
        zero_fill(lambda copy: copy.start())
        zero_fill(lambda copy: copy.wait())

    def row_copy(t, slot):
        return pltpu.make_async_copy(h_ref.at[pl.ds(t, 1), :],
                                     xs_hbm.at[pl.ds(slot, 1), :], sem)

    def issue(group, carry):
        t0 = pl.multiple_of(group * ROW_GROUP, ROW_GROUP)
        for u in range(ROW_GROUP):
            tok = s * tb + t0 + u
            for k in range(TOP_K):
                row_copy(t0 + u, dest_ref[tok * TOP_K + k]).start(priority=k % 2)
        return carry

    lax.fori_loop(0, tb // ROW_GROUP, issue, 0)

    def drain(t, carry):
        for k in range(TOP_K):
            row_copy(0, 0).wait()
        return carry

    lax.fori_loop(0, tb, drain, 0)


def _dispatch(hp, dest, pends, pcounts, *, n_rows, tm):
    n_tok, dw = hp.shape
    n_exp = pends.shape[0]
    tb = _tile(n_tok, 256)
    grid_spec = pltpu.PrefetchScalarGridSpec(
        num_scalar_prefetch=3,
        grid=(n_tok // tb,),
        in_specs=[pl.BlockSpec((tb, dw), lambda i, *_: (i, 0))],
        out_specs=pl.BlockSpec(memory_space=pl.ANY),
        scratch_shapes=[pltpu.VMEM((tm, dw), hp.dtype),
                        pltpu.SemaphoreType.DMA(()),
                        pltpu.SemaphoreType.DMA(())],
    )
    return pl.pallas_call(
        functools.partial(_dispatch_kernel, tb=tb, tm=tm, n_exp=n_exp),
        grid_spec=grid_spec,
        out_shape=jax.ShapeDtypeStruct((n_rows, dw), hp.dtype),
        compiler_params=_params("arbitrary"),
        name="dispatch",
    )(dest, pends, pcounts, hp)


def _pack_bf16_pairs(hi, lo):
    hb = pltpu.bitcast(hi.astype(BF16).astype(F32), jnp.uint32)
    lb = pltpu.bitcast(lo.astype(BF16).astype(F32), jnp.uint32)
    return hb | (lb >> 16)


def _unpack_pairs_f32(p):
    return (pltpu.bitcast(p & jnp.uint32(0xFFFF0000), F32), pltpu.bitcast(p << 16, F32))


def _expert_block(be_ref, nv_ref, chg_ref, nxt_ref, used_ref, w_hbm, stage, sems, w_bf, o_ref,
                  compute, *, half_stride, tn, nj):
    j = pl.program_id(0)
    i = pl.program_id(1)
    change = chg_ref[i]
    valid = i < nv_ref[0]
    tm = o_ref.shape[0]
    steady = jnp.logical_and(valid, change == 0)
    step = tm // TAIL_LEVELS
    quarters = jnp.clip((used_ref[i] + step - 1) // step, 1, TAIL_LEVELS)

    def copies(e, jj):
        return [pltpu.make_async_copy(
            w_hbm.at[e, :, pl.ds(pl.multiple_of(part * half_stride + jj * tn, tn), tn)],
            stage.at[part], sems.at[part]) for part in range(2)]

    @pl.when(change > 0)
    def _():
        @pl.when(jnp.logical_and(j == 0, change == 1))
        def _():
            for cp in copies(be_ref[i], j):
                cp.start()

        for cp in copies(be_ref[i], j):
            cp.wait()

        def convert(part, rows):
            w = stage[part, rows, :].astype(BF16)
            w_bf[part, rows, :] = w
            return w

        compute(convert, tm)

        last = nxt_ref[i] < 0
        next_e = jnp.where(last, be_ref[0], nxt_ref[i])
        next_j = jnp.where(last, j + 1, j)

        @pl.when(next_j < nj)
        def _():
            for cp in copies(next_e, next_j):
                cp.start(priority=1)

    def current(part, rows):
        return w_bf[part, rows, :]

    for q in range(1, TAIL_LEVELS + 1):
        n_rows = tm * q // TAIL_LEVELS

        @pl.when(jnp.logical_and(steady, quarters == q))
        def _():
            compute(current, n_rows)
            if n_rows < tm:
                o_ref[n_rows:, :] = jnp.zeros((tm - n_rows, o_ref.shape[1]), o_ref.dtype)

    @pl.when(jnp.logical_not(valid))
    def _():
        o_ref[...] = jnp.zeros_like(o_ref)


def _expert_up_kernel(be_ref, nv_ref, chg_ref, nxt_ref, used_ref, xs_ref, w_hbm, bg_ref, bu_ref,
                      o_ref, stage, w_bf, sems, *, ff, tn, nj):
    def compute(get_w, n_rows):
        x_hi, x_lo = _unpack_pairs_f32(xs_ref[:n_rows, :])
        x_hi, x_lo = x_hi.astype(BF16), x_lo.astype(BF16)
        half = x_hi.shape[-1]
        top, bottom = slice(0, half), slice(half, 2 * half)
        gate = (jnp.dot(x_hi, get_w(0, top), preferred_element_type=F32)
                + jnp.dot(x_lo, get_w(0, bottom), preferred_element_type=F32) + bg_ref[0])
        up = (jnp.dot(x_hi, get_w(1, top), preferred_element_type=F32)
              + jnp.dot(x_lo, get_w(1, bottom), preferred_element_type=F32) + bu_ref[0])
        gate = jnp.minimum(gate, SWIGLU_LIMIT)
        up = jnp.clip(up, -SWIGLU_LIMIT, SWIGLU_LIMIT)
        act = (up + 1.0) * gate * (1.0 / (1.0 + jnp.exp(-SWIGLU_ALPHA * gate)))
        o_ref[:n_rows, :] = act.astype(o_ref.dtype)

    _expert_block(be_ref, nv_ref, chg_ref, nxt_ref, used_ref, w_hbm, stage, sems, w_bf, o_ref,
                  compute, half_stride=ff, tn=tn, nj=nj)


def _expert_scratch(k, tn):
    return [pltpu.VMEM((2, k, tn), F32),
            pltpu.VMEM((2, k, tn), BF16),
            pltpu.SemaphoreType.DMA((2,))]


def _expert_up(xs, meta, w_gu, b_gu, *, tm):
    n_rows, dw = xs.shape
    n_exp, d, ff2 = w_gu.shape
    ff = ff2 // 2
    assert d == 2 * dw
    tn = _tile(ff, 512)
    nj = ff // tn
    n_blocks = n_rows // tm
    b3 = b_gu.reshape(n_exp, 1, ff2)
    grid_spec = pltpu.PrefetchScalarGridSpec(
        num_scalar_prefetch=5,
        grid=(nj, n_blocks),
        in_specs=[
            pl.BlockSpec((tm, dw), lambda j, i, be, nv, *_: (jnp.minimum(i, nv[0] - 1), 0)),
            pl.BlockSpec(memory_space=pl.ANY),
            pl.BlockSpec((1, 1, tn), lambda j, i, be, *_: (be[i], 0, j)),
            pl.BlockSpec((1, 1, tn), lambda j, i, be, *_: (be[i], 0, nj + j)),
        ],
        out_specs=pl.BlockSpec((tm, tn), lambda j, i, *_: (i, j)),
        scratch_shapes=_expert_scratch(d, tn),
    )
    return pl.pallas_call(
        functools.partial(_expert_up_kernel, ff=ff, tn=tn, nj=nj),
        grid_spec=grid_spec,
        out_shape=jax.ShapeDtypeStruct((n_rows, ff), BF16),
        compiler_params=_params("arbitrary", "arbitrary"),
        name="expert_up",
    )(*meta, xs, w_gu, b3, b3)


def _expert_down_kernel(be_ref, nv_ref, chg_ref, nxt_ref, used_ref, h_ref, w_hbm, ba_ref, bb_ref,
                        o_ref, stage, w_bf, sems, *, half, tn, nj):
    def compute(get_w, n_rows):
        k_half = h_ref.shape[-1] // 2
        top, bottom = slice(0, k_half), slice(k_half, 2 * k_half)
        h_top, h_bottom = h_ref[:n_rows, top], h_ref[:n_rows, bottom]
        ya = (jnp.dot(h_top, get_w(0, top), preferred_element_type=F32)
              + jnp.dot(h_bottom, get_w(0, bottom), preferred_element_type=F32) + ba_ref[0])
        yb = (jnp.dot(h_top, get_w(1, top), preferred_element_type=F32)
              + jnp.dot(h_bottom, get_w(1, bottom), preferred_element_type=F32) + bb_ref[0])
        o_ref[:n_rows, :] = _pack_bf16_pairs(ya, yb)

    _expert_block(be_ref, nv_ref, chg_ref, nxt_ref, used_ref, w_hbm, stage, sems, w_bf, o_ref,
                  compute, half_stride=half, tn=tn, nj=nj)


def _expert_down(hs, meta, w_dn, b_dn, *, tm):
    n_rows, ff = hs.shape
    n_exp, _, d = w_dn.shape
    half = d // 2
    tn = _tile(half, 1024)
    nj = half // tn
    n_blocks = n_rows // tm
    b3 = b_dn.reshape(n_exp, 1, d)
    grid_spec = pltpu.PrefetchScalarGridSpec(
        num_scalar_prefetch=5,
        grid=(nj, n_blocks),
        in_specs=[
            pl.BlockSpec((tm, ff), lambda j, i, be, nv, *_: (jnp.minimum(i, nv[0] - 1), 0)),
            pl.BlockSpec(memory_space=pl.ANY),
            pl.BlockSpec((1, 1, tn), lambda j, i, be, *_: (be[i], 0, j)),
            pl.BlockSpec((1, 1, tn), lambda j, i, be, *_: (be[i], 0, nj + j)),
        ],
        out_specs=pl.BlockSpec((tm, tn), lambda j, i, *_: (i, j)),
        scratch_shapes=_expert_scratch(ff, tn),
    )
    return pl.pallas_call(
        functools.partial(_expert_down_kernel, half=half, tn=tn, nj=nj),
        grid_spec=grid_spec,
        out_shape=jax.ShapeDtypeStruct((n_rows, half), jnp.uint32),
        compiler_params=_params("arbitrary", "arbitrary"),
        name="expert_down",
    )(*meta, hs, w_dn, b3, b3)


def _combine_kernel(dest_ref, x_ref, gate_ref, g_ref, ys_hbm, o_ref, buf, sems, *, tb, n_steps):
    s = pl.program_id(0)
    cur = s & 1

    def row_copy(row, buf_slot, k, t):
        return pltpu.make_async_copy(ys_hbm.at[pl.ds(row, 1), :],
                                     buf.at[buf_slot, k, pl.ds(t, 1), :], sems.at[buf_slot])

    def gather(step, buf_slot):
        def issue(group, carry):
            t0 = pl.multiple_of(group * ROW_GROUP, ROW_GROUP)
            for u in range(ROW_GROUP):
                tok = step * tb + t0 + u
                for k in range(TOP_K):
                    row_copy(dest_ref[tok * TOP_K + k], buf_slot, k, t0 + u).start(
                        priority=k % 2)
            return carry

        lax.fori_loop(0, tb // ROW_GROUP, issue, 0)

    @pl.when(s == 0)
    def _():
        gather(0, 0)

    @pl.when(s + 1 < n_steps)
    def _():
        gather(s + 1, 1 - cur)

    def drain(t, carry):
        for k in range(TOP_K):
            row_copy(0, cur, 0, 0).wait()
        return carry

    lax.fori_loop(0, tb, drain, 0)

    half = x_ref.shape[-1] // 2
    acc_hi = x_ref[:, :half]
    acc_lo = x_ref[:, half:]
    for k in range(TOP_K):
        y_hi, y_lo = _unpack_pairs_f32(buf[cur, k])
        gate = gate_ref[:, k:k + 1]
        acc_hi = acc_hi + gate * y_hi
        acc_lo = acc_lo + gate * y_lo
    ms = (jnp.sum(acc_hi * acc_hi, axis=-1, keepdims=True)
          + jnp.sum(acc_lo * acc_lo, axis=-1, keepdims=True)) / (2 * half)
    r = lax.rsqrt(ms + NORM_EPS)
    o_ref[:, :half] = acc_hi * r * g_ref[:, :half]
    o_ref[:, half:] = acc_lo * r * g_ref[:, half:]


def _combine(x, gates, dest, ys, g):
    n_tok, d = x.shape
    tb = _tile(n_tok, 256)
    n_steps = n_tok // tb
    grid_spec = pltpu.PrefetchScalarGridSpec(
        num_scalar_prefetch=1,
        grid=(n_steps,),
        in_specs=[pl.BlockSpec((tb, d), lambda i, dest: (i, 0)),
                  pl.BlockSpec((tb, TOP_K), lambda i, dest: (i, 0)),
                  pl.BlockSpec((1, d), lambda i, dest: (0, 0)),
                  pl.BlockSpec(memory_space=pl.ANY)],
        out_specs=pl.BlockSpec((tb, d), lambda i, dest: (i, 0)),
        scratch_shapes=[pltpu.VMEM((2, TOP_K, tb, d // 2), ys.dtype),
                        pltpu.SemaphoreType.DMA((2,))],
    )
    return pl.pallas_call(
        functools.partial(_combine_kernel, tb=tb, n_steps=n_steps),
        grid_spec=grid_spec,
        out_shape=jax.ShapeDtypeStruct((n_tok, d), F32),
        compiler_params=_params("arbitrary"),
        name="combine",
    )(dest, x, gates, g.reshape(1, d), ys)


def _routing_tables(top_idx, rank, counts, tm):
    n_exp = counts.shape[0]
    flat_e = top_idx.reshape(-1)
    n_pairs = flat_e.shape[0]
    pcounts = (counts + tm - 1) // tm * tm
    pends = jnp.cumsum(pcounts)
    pstarts = pends - pcounts
    onehot = flat_e[:, None] == jnp.arange(n_exp, dtype=jnp.int32)[None, :]
    dest = jnp.sum(jnp.where(onehot, pstarts[None, :], 0), axis=1) + rank.reshape(-1)
    assert n_pairs % tm == 0
    n_blocks = n_pairs // tm + n_exp
    block_id = jnp.arange(n_blocks, dtype=jnp.int32)
    block_e = jnp.minimum(jnp.sum(pends[None, :] <= (block_id * tm)[:, None], axis=1),
                          n_exp - 1).astype(jnp.int32)
    n_valid = (pends[-1] // tm).astype(jnp.int32)
    prev_e = jnp.concatenate([jnp.full((1,), -1, jnp.int32), block_e[:-1]])
    is_change = (block_id < n_valid) & (block_e != prev_e)
    change_ord = jnp.cumsum(is_change.astype(jnp.int32))
    change = jnp.where(is_change, change_ord, 0).astype(jnp.int32)
    later = is_change[None, :] & (block_id[None, :] > block_id[:, None])
    next_e = jnp.where(jnp.any(later, axis=1), block_e[jnp.argmax(later, axis=1)], -1)
    real_end = pstarts + counts
    used = jnp.clip(real_end[block_e] - block_id * tm, 0, tm).astype(jnp.int32)
    meta = (block_e, n_valid.reshape(1), change, next_e.astype(jnp.int32), used)
    return (dest.astype(jnp.int32), pends.astype(jnp.int32), pcounts.astype(jnp.int32),
            meta, n_blocks * tm)


def kernel(x, attn_norm_g, w_in, lambda_q1, lambda_k1, lambda_q2, lambda_k2, diff_subln_g,
           sgu_ln_g, sgu_ln_b, sgu_w, sgu_b, rel_bias, w_out, ffn_norm_g, router_w, router_b,
           w_gate_up, b_gate_up, w_down, b_down, final_norm_g):
    batch, seq, d = x.shape
    n_tok = batch * seq
    hd = lambda_q1.shape[-1]
    n_maps = rel_bias.shape[1]
    n_heads = n_maps // 2
    q_cols = n_maps * hd
    attn_width = n_heads * diff_subln_g.shape[-1]
    n_exp = w_gate_up.shape[1]
    tq = _tile(seq, 512)
    tk = tq // 2
    tm = 512

    xt = x.reshape(n_tok, d)
    h = _rmsnorm_rows(xt, attn_norm_g[0], BF16)
    proj = _in_projection(h, w_in[0].astype(BF16))

    lam = (jnp.exp(jnp.sum(lambda_q1[0].astype(F32) * lambda_k1[0].astype(F32)))
           - jnp.exp(jnp.sum(lambda_q2[0].astype(F32) * lambda_k2[0].astype(F32)))
           + LAMBDA_INIT).reshape(1)
    bias = _bias_tiles(rel_bias, tq)
    far_bias = rel_bias[-1].astype(F32)
    att = _diff_attention(proj, bias, far_bias, lam, diff_subln_g[0],
                          batch=batch, seq=seq, n_heads=n_heads, hd=hd, tq=tq, tk=tk)
    sgu = _spatial_gating(proj, sgu_ln_g[0], sgu_ln_b[0], sgu_w[0], sgu_b[0],
                          u_col0=2 * q_cols + attn_width)
    x1 = _out_projection(att, sgu, w_out[0].astype(BF16), xt)

    hp, top_idx, gates, rank, counts = _router(x1, ffn_norm_g[0], router_w[0], router_b[0])
    dest, pends, pcounts, meta, n_rows = _routing_tables(
        top_idx, rank, counts[0].astype(jnp.int32), tm)
    xs = _dispatch(hp, dest, pends, pcounts, n_rows=n_rows, tm=tm)
    hs = _expert_up(xs, meta, w_gate_up[0], b_gate_up[0], tm=tm)
    ys = _expert_down(hs, meta, w_down[0], b_down[0], tm=tm)
    out = _combine(x1, gates, dest, ys, final_norm_g)
    return out.reshape(batch, seq, d)
```

```python
import functools
import math

import jax
import jax.numpy as jnp
from jax import lax
from jax.experimental import pallas as pl
from jax.experimental.pallas import tpu as pltpu

F32 = jnp.float32
BF16 = jnp.bfloat16

TOP_K = 4
MAX_DISTANCE = 128
SWIGLU_LIMIT = 7.0
SWIGLU_ALPHA = 1.702
NORM_EPS = 1e-5
LAMBDA_INIT = 0.8 - 0.6 * math.exp(0.0)
MASK_VALUE = -1e30
LOG2E = math.log2(math.e)

LANES = 128
BF16_SUBLANES = 16
TAIL_LEVELS = 4
VMEM_LIMIT_BYTES = 56 * 1024 * 1024


def _tile(dim, want):
    t = min(dim, want)
    while dim % t:
        t -= LANES
    assert t > 0, (dim, want)
    return t


def _params(*semantics, flags=None):
    return pltpu.CompilerParams(dimension_semantics=semantics,
                                vmem_limit_bytes=VMEM_LIMIT_BYTES, flags=flags)


def _rmsnorm_kernel(x_ref, g_ref, o_ref):
    x = x_ref[...]
    ms = jnp.mean(x * x, axis=-1, keepdims=True)
    o_ref[...] = (x * lax.rsqrt(ms + NORM_EPS) * g_ref[...]).astype(o_ref.dtype)


def _rmsnorm_rows(x, g, out_dtype):
    n, d = x.shape
    tr = _tile(n, 512)
    return pl.pallas_call(
        _rmsnorm_kernel,
        grid=(n // tr,),
        in_specs=[pl.BlockSpec((tr, d), lambda i: (i, 0)),
                  pl.BlockSpec((1, d), lambda i: (0, 0))],
        out_specs=pl.BlockSpec((tr, d), lambda i: (i, 0)),
        out_shape=jax.ShapeDtypeStruct((n, d), out_dtype),
        compiler_params=_params("parallel"),
        name="rmsnorm_rows",
    )(x, g.reshape(1, d))


def _in_projection_kernel(a_ref, b_ref, o_ref):
    acc = jnp.dot(a_ref[...], b_ref[...], preferred_element_type=F32)
    for c in range(o_ref.shape[0]):
        o_ref[c] = acc[:, c * LANES:(c + 1) * LANES].astype(o_ref.dtype)


def _in_projection(h, w):
    m, k = h.shape
    n = w.shape[1]
    tm, tn = _tile(m, 1024), _tile(n, 1024)
    return pl.pallas_call(
        _in_projection_kernel,
        grid=(m // tm, n // tn),
        in_specs=[pl.BlockSpec((tm, k), lambda i, j: (i, 0)),
                  pl.BlockSpec((k, tn), lambda i, j: (0, j))],
        out_specs=pl.BlockSpec((tn // LANES, tm, LANES), lambda i, j: (j, i, 0)),
        out_shape=jax.ShapeDtypeStruct((n // LANES, m, LANES), BF16),
        compiler_params=_params("parallel", "parallel"),
        name="in_projection",
    )(h, w)


def _out_projection_kernel(a_ref, s_ref, wa_ref, ws_ref, x_ref, o_ref):
    acc = jnp.dot(a_ref[...], wa_ref[...], preferred_element_type=F32)
    acc += jnp.dot(s_ref[...], ws_ref[...], preferred_element_type=F32)
    o_ref[...] = x_ref[...] + acc


def _out_projection(att, sgu, w, x):
    m, ka = att.shape
    ks = sgu.shape[1]
    n = w.shape[1]
    assert ka == ks and w.shape[0] == ka + ks
    tm, tn = _tile(m, 1024), _tile(n, 512)
    return pl.pallas_call(
        _out_projection_kernel,
        grid=(m // tm, n // tn),
        in_specs=[pl.BlockSpec((tm, ka), lambda i, j: (i, 0)),
                  pl.BlockSpec((tm, ks), lambda i, j: (i, 0)),
                  pl.BlockSpec((ka, tn), lambda i, j: (0, j)),
                  pl.BlockSpec((ks, tn), lambda i, j: (1, j)),
                  pl.BlockSpec((tm, tn), lambda i, j: (i, j))],
        out_specs=pl.BlockSpec((tm, tn), lambda i, j: (i, j)),
        out_shape=jax.ShapeDtypeStruct((m, n), F32),
        compiler_params=_params("parallel", "parallel"),
        name="out_projection",
    )(att, sgu, w, w, x)


def _bias_tiles_kernel(tbl_ref, o_ref, *, tq, n_buckets):
    h = pl.program_id(0)
    max_exact = n_buckets // 2
    nb = tq // LANES
    key = lax.broadcasted_iota(jnp.int32, (LANES, LANES), 0)
    qry = lax.broadcasted_iota(jnp.int32, (LANES, LANES), 1)

    def lookup(dist, m):
        d = jnp.maximum(dist, 1).astype(F32)
        large = max_exact + (jnp.log(d / max_exact) / math.log(MAX_DISTANCE / max_exact)
                             * (n_buckets - max_exact)).astype(jnp.int32)
        large = jnp.minimum(large, n_buckets - 1)
        bucket = jnp.where(dist < max_exact, dist, large)
        val = jnp.zeros(dist.shape, F32)
        for b in range(n_buckets):
            val = jnp.where(bucket == b, tbl_ref[b, 2 * h + m], val)
        return jnp.where(dist >= 0, val * LOG2E, MASK_VALUE)

    for w in range(2):
        for delta in range(-(nb - 1), nb):
            offset = delta * LANES + w * tq
            for m in range(2):
                if offset + LANES - 1 < 0:
                    block = jnp.full((LANES, LANES), MASK_VALUE, F32)
                elif offset - (LANES - 1) >= MAX_DISTANCE:
                    block = jnp.full((LANES, LANES), tbl_ref[n_buckets - 1, 2 * h + m], F32) * LOG2E
                else:
                    block = lookup(qry - key + offset, m)
                for kc in range(max(0, -delta), min(nb, nb - delta)):
                    qc = kc + delta
                    o_ref[0, w, kc * LANES:(kc + 1) * LANES,
                          m * tq + qc * LANES:m * tq + (qc + 1) * LANES] = block


def _bias_tiles(rel_bias, tq):
    n_buckets, n_maps = rel_bias.shape
    n_heads = n_maps // 2
    assert tq >= MAX_DISTANCE
    return pl.pallas_call(
        functools.partial(_bias_tiles_kernel, tq=tq, n_buckets=n_buckets),
        grid=(n_heads,),
        in_specs=[pl.BlockSpec(memory_space=pltpu.SMEM)],
        out_specs=pl.BlockSpec((1, 2, tq, 2 * tq), lambda h: (h, 0, 0, 0)),
        out_shape=jax.ShapeDtypeStruct((n_heads, 2, tq, 2 * tq), F32),
        compiler_params=_params("parallel"),
        name="bias_tiles",
    )(rel_bias.astype(F32))


def _attention_kernel(far_ref, lam_ref, q_ref, k_ref, v_ref, bias_ref, g_ref, o_ref,
                      vt_ref, sa_ref, sb_ref, acc_ref, *, tq, tk, hd):
    i = pl.program_id(2)
    n_group, seq, vd = v_ref.shape
    head0 = pl.program_id(1) * n_group
    group = range(n_group)
    sub = tq // tk
    assert sub == 2

    @pl.when(i == 0)
    def _():
        ones_row = lax.broadcasted_iota(jnp.int32, (BF16_SUBLANES, tk), 0) == 0
        for g in group:
            for c in range(seq // tk):
                vt_ref[g, c, :vd, :] = (
                    v_ref[g, c * tk:(c + 1) * tk, :].astype(F32).T.astype(BF16))
                vt_ref[g, c, vd:, :] = jnp.where(ones_row, 1.0, 0.0).astype(BF16)

    col = lax.broadcasted_iota(jnp.int32, (1, 2 * tq), 1)
    qqt, far = [], []
    for g in group:
        qt = (q_ref[g].astype(F32) * (hd ** -0.5 * LOG2E)).T
        dim = lax.broadcasted_iota(jnp.int32, qt.shape, 0)
        qqt.append(jnp.concatenate([jnp.where(dim < hd, qt, 0.0),
                                    jnp.where(dim >= hd, qt, 0.0)], axis=1).astype(BF16))
        head = head0 + g
        far.append(jnp.where(col < tq, far_ref[2 * head], far_ref[2 * head + 1]) * LOG2E)

    def scores(g, kb):
        kblk = k_ref[g, pl.ds(pl.multiple_of(kb * tk, tk), tk), :]
        return jnp.dot(kblk, qqt[g], preferred_element_type=F32)

    def online_update(g, s, kb, m_prev, shift):
        m_new = jnp.maximum(m_prev, jnp.max(s, axis=0, keepdims=True) + shift)
        alpha = jnp.exp2(m_prev - m_new)
        p = jnp.exp2((s - (m_new - shift)).astype(BF16))
        acc_ref[g] = alpha * acc_ref[g] + jnp.dot(vt_ref[g, kb], p, preferred_element_type=F32)
        return m_new

    kb_prev = sub * jnp.maximum(i - 1, 0)
    no_prev = jnp.where(i >= 1, 0.0, MASK_VALUE)
    n_pairs = jnp.maximum(i - 1, 0)
    last = jnp.maximum(sub * n_pairs - 1, 0)
    acc_ref[...] = jnp.zeros_like(acc_ref)
    m = [jnp.full((1, 2 * tq), MASK_VALUE, F32) for _ in group]

    for g in group:
        sa_ref[g] = scores(g, sub * i) + bias_ref[g, 0, :tk, :]
    for g in group:
        sb_ref[g] = scores(g, sub * i + 1) + bias_ref[g, 0, tk:, :]
        m[g] = online_update(g, sa_ref[g], sub * i, m[g], 0.0)
    for g in group:
        sa_ref[g] = scores(g, kb_prev) + (bias_ref[g, 1, :tk, :] + no_prev)
        m[g] = online_update(g, sb_ref[g], sub * i + 1, m[g], 0.0)
    for g in group:
        sb_ref[g] = scores(g, kb_prev + 1) + (bias_ref[g, 1, tk:, :] + no_prev)
        m[g] = online_update(g, sa_ref[g], kb_prev, m[g], 0.0)
    for g in group:
        sa_ref[g] = scores(g, 0)
        m[g] = online_update(g, sb_ref[g], kb_prev + 1, m[g], 0.0)

    def far_pair(n, m_run):
        m_run = list(m_run)
        kb = sub * n
        for g in group:
            sb_ref[g] = scores(g, kb + 1)
            m_run[g] = online_update(g, sa_ref[g], kb, m_run[g], far[g])
        for g in group:
            sa_ref[g] = scores(g, jnp.minimum(kb + 2, last))
            m_run[g] = online_update(g, sb_ref[g], kb + 1, m_run[g], far[g])
        return tuple(m_run)

    lax.fori_loop(0, n_pairs, far_pair, tuple(m))

    for g in group:
        acc = acc_ref[g]
        o = acc[:vd] * (1.0 / acc[vd:vd + 1])
        a = o[:, :tq] - lam_ref[0] * o[:, tq:]
        ms = jnp.mean(a * a, axis=0, keepdims=True)
        y = a * lax.rsqrt(ms + NORM_EPS) * g_ref[...] * (1.0 - LAMBDA_INIT)
        o_ref[:, g * vd:(g + 1) * vd] = y.T.astype(o_ref.dtype)


def _diff_attention(proj, bias, far_bias, lam, subln_g, *, batch, seq, n_heads, hd, tq, tk):
    vd = subln_g.shape[-1]
    assert 2 * hd == LANES and vd == LANES
    nq = seq // tq
    n_group = 2
    assert n_heads % n_group == 0
    k_blk0 = n_heads // n_group
    v_blk0 = 2 * n_heads // n_group
    grid_spec = pltpu.PrefetchScalarGridSpec(
        num_scalar_prefetch=2,
        grid=(batch, n_heads // n_group, nq),
        in_specs=[
            pl.BlockSpec((n_group, tq, LANES), lambda b, h, i, *_: (h, b * nq + i, 0)),
            pl.BlockSpec((n_group, seq, LANES), lambda b, h, i, *_: (k_blk0 + h, b, 0)),
            pl.BlockSpec((n_group, seq, LANES), lambda b, h, i, *_: (v_blk0 + h, b, 0)),
            pl.BlockSpec((n_group, 2, tq, 2 * tq), lambda b, h, i, *_: (h, 0, 0, 0)),
            pl.BlockSpec((vd, 1), lambda b, h, i, *_: (0, 0)),
        ],
        out_specs=pl.BlockSpec((tq, n_group * vd), lambda b, h, i, *_: (b * nq + i, h)),
        scratch_shapes=[pltpu.VMEM((n_group, seq // tk, vd + BF16_SUBLANES, tk), BF16),
                        pltpu.VMEM((n_group, tk, 2 * tq), F32),
                        pltpu.VMEM((n_group, tk, 2 * tq), F32),
                        pltpu.VMEM((n_group, vd + BF16_SUBLANES, 2 * tq), F32)],
    )
    return pl.pallas_call(
        functools.partial(_attention_kernel, tq=tq, tk=tk, hd=hd),
        grid_spec=grid_spec,
        out_shape=jax.ShapeDtypeStruct((batch * seq, n_heads * vd), BF16),
        compiler_params=_params("parallel", "parallel", "arbitrary"),
        name="diff_attention",
    )(far_bias, lam, proj, proj, proj, bias, subln_g.reshape(vd, 1))


def _gelu(x):
    return 0.5 * x * (1.0 + lax.erf(x * math.sqrt(0.5)))


def _sgu_kernel(u_ref, v_ref, lng_ref, lnb_ref, w_ref, bt_ref, o_ref):
    n_heads, rows, hdim = u_ref.shape
    chunk = w_ref.shape[-1]
    width = n_heads * hdim
    u = _gelu(u_ref[...].astype(F32))
    v = _gelu(v_ref[...].astype(F32))
    mu = jnp.sum(jnp.sum(v, axis=0), axis=-1, keepdims=True) / width
    vc = v - mu
    var = jnp.sum(jnp.sum(vc * vc, axis=0), axis=-1, keepdims=True) / width
    vn = (vc * lax.rsqrt(var + NORM_EPS) * lng_ref[...] + lnb_ref[...]).astype(BF16)
    r = lax.broadcasted_iota(jnp.int32, (chunk, chunk), 0)
    c = lax.broadcasted_iota(jnp.int32, (chunk, chunk), 1)
    causal = r >= c
    for hh in range(n_heads):
        w = jnp.where(causal, w_ref[hh], 0.0).astype(BF16)
        for c0 in range(0, rows, chunk):
            sl = slice(c0, c0 + chunk)
            y = jnp.dot(w, vn[hh, sl], preferred_element_type=F32) + bt_ref[:, hh:hh + 1]
            o_ref[sl, hh * hdim:(hh + 1) * hdim] = (u[hh, sl] * y).astype(o_ref.dtype)


def _spatial_gating(proj, ln_g, ln_b, w_s, b_s, *, u_col0):
    _, n_tok, hdim = proj.shape
    n_heads, chunk, _ = w_s.shape
    width = ln_g.shape[-1]
    assert width == n_heads * hdim and u_col0 % width == 0
    ub = u_col0 // width
    rows = chunk * min(4, n_tok // chunk)
    assert n_tok % rows == 0
    return pl.pallas_call(
        _sgu_kernel,
        grid=(n_tok // rows,),
        in_specs=[pl.BlockSpec((n_heads, rows, hdim), lambda i: (ub, i, 0)),
                  pl.BlockSpec((n_heads, rows, hdim), lambda i: (ub + 1, i, 0)),
                  pl.BlockSpec((n_heads, 1, hdim), lambda i: (0, 0, 0)),
                  pl.BlockSpec((n_heads, 1, hdim), lambda i: (0, 0, 0)),
                  pl.BlockSpec((n_heads, chunk, chunk), lambda i: (0, 0, 0)),
                  pl.BlockSpec((chunk, n_heads), lambda i: (0, 0))],
        out_specs=pl.BlockSpec((rows, width), lambda i: (i, 0)),
        out_shape=jax.ShapeDtypeStruct((n_tok, width), BF16),
        compiler_params=_params("parallel"),
        name="spatial_gating",
    )(proj, proj, ln_g.reshape(n_heads, 1, hdim), ln_b.reshape(n_heads, 1, hdim), w_s, b_s.T)


def _split_bf16(x):
    hi = x.astype(BF16)
    return hi, (x - hi.astype(F32)).astype(BF16)


def _router_kernel(x_ref, g_ref, rw_ref, rb_ref, hp_ref, idx_ref, gate_ref, rank_ref, cnt_ref):
    @pl.when(pl.program_id(0) == 0)
    def _():
        cnt_ref[...] = jnp.zeros_like(cnt_ref)

    x = x_ref[...]
    ms = jnp.mean(x * x, axis=-1, keepdims=True)
    h = x * lax.rsqrt(ms + NORM_EPS) * g_ref[...]
    half = h.shape[-1] // 2
    hp_ref[...] = _pack_bf16_pairs(h[:, :half], h[:, half:])

    h_hi, h_lo = _split_bf16(h)
    w_hi, w_lo = _split_bf16(rw_ref[...])
    logits = (jnp.dot(h_hi, w_hi, preferred_element_type=F32)
              + jnp.dot(h_lo, w_hi, preferred_element_type=F32)
              + jnp.dot(h_hi, w_lo, preferred_element_type=F32)) + rb_ref[...]
    tr, n_exp = logits.shape
    lane = lax.broadcasted_iota(jnp.int32, logits.shape, 1)
    vals, idxs = [], []
    for _ in range(TOP_K):
        top = jnp.max(logits, axis=-1, keepdims=True)
        idx = jnp.min(jnp.where(logits == top, lane, n_exp), axis=-1, keepdims=True)
        vals.append(top)
        idxs.append(idx)
        logits = jnp.where(lane == idx, -jnp.inf, logits)
    exps = [jnp.exp(v - vals[0]) for v in vals]
    denom = exps[0]
    for e in exps[1:]:
        denom = denom + e

    earlier = (lax.broadcasted_iota(jnp.int32, (tr, tr), 1)
               < lax.broadcasted_iota(jnp.int32, (tr, tr), 0))
    earlier = jnp.where(earlier, 1.0, 0.0).astype(BF16)
    count = cnt_ref[...]
    for k in range(TOP_K):
        chosen = lane == idxs[k]
        onehot = jnp.where(chosen, 1.0, 0.0)
        before = jnp.dot(earlier, onehot.astype(BF16), preferred_element_type=F32) + count
        rank = jnp.sum(jnp.where(chosen, before, 0.0), axis=-1, keepdims=True)
        idx_ref[:, k:k + 1] = idxs[k]
        gate_ref[:, k:k + 1] = exps[k] / denom
        rank_ref[:, k:k + 1] = rank.astype(jnp.int32)
        count = count + jnp.sum(onehot, axis=0, keepdims=True)
    cnt_ref[...] = count


def _router(x, g, rw, rb):
    n, d = x.shape
    n_exp = rw.shape[1]
    tr = _tile(n, 512)
    assert n * TOP_K < 2 ** 24
    return pl.pallas_call(
        _router_kernel,
        grid=(n // tr,),
        in_specs=[pl.BlockSpec((tr, d), lambda i: (i, 0)),
                  pl.BlockSpec((1, d), lambda i: (0, 0)),
                  pl.BlockSpec((d, n_exp), lambda i: (0, 0)),
                  pl.BlockSpec((1, n_exp), lambda i: (0, 0))],
        out_specs=[pl.BlockSpec((tr, d // 2), lambda i: (i, 0)),
                   pl.BlockSpec((tr, TOP_K), lambda i: (i, 0)),
                   pl.BlockSpec((tr, TOP_K), lambda i: (i, 0)),
                   pl.BlockSpec((tr, TOP_K), lambda i: (i, 0)),
                   pl.BlockSpec((1, n_exp), lambda i: (0, 0))],
        out_shape=[jax.ShapeDtypeStruct((n, d // 2), jnp.uint32),
                   jax.ShapeDtypeStruct((n, TOP_K), jnp.int32),
                   jax.ShapeDtypeStruct((n, TOP_K), F32),
                   jax.ShapeDtypeStruct((n, TOP_K), jnp.int32),
                   jax.ShapeDtypeStruct((1, n_exp), F32)],
        compiler_params=_params("arbitrary"),
        name="router",
    )(x, g.reshape(1, d), rw, rb.reshape(1, n_exp))


def _dispatch_kernel(dest_ref, pends_ref, pcnt_ref, h_ref, xs_hbm, zbuf, zsem, sem,
                     *, tb, tm, n_exp):
    s = pl.program_id(0)

    n_rows = xs_hbm.shape[0]

    def zero_block(start):
        return pltpu.make_async_copy(
            zbuf, xs_hbm.at[pl.ds(pl.multiple_of(start, tm), tm), :], zsem)

    def zero_fill(action):
        for e in range(n_exp):
            @pl.when(pcnt_ref[e] > 0)
            def _():
                action(zero_block(pends_ref[e] - tm))
        for b in range(n_exp):
            start = pends_ref[n_exp - 1] + b * tm

            @pl.when(start < n_rows)
            def _():
                action(zero_block(start))

    @pl.when(s == 0)
    def _():
        zbuf[...] = jnp.zeros_like(zbuf)
        zero_fill(lambda copy: copy.start())
        zero_fill(lambda copy: copy.wait())

    def row_copy(t, slot):
        return pltpu.make_async_copy(h_ref.at[pl.ds(t, 1), :],
                                     xs_hbm.at[pl.ds(slot, 1), :], sem)

    base = s * (tb * TOP_K)
    for t in range(tb):
        for k in range(TOP_K):
            row_copy(t, dest_ref[base + t * TOP_K + k]).start(priority=k % 2)

    def drain(t, carry):
        for k in range(TOP_K):
            row_copy(0, 0).wait()
        return carry

    lax.fori_loop(0, tb, drain, 0)


def _dispatch(hp, dest, pends, pcounts, *, n_rows, tm):
    n_tok, dw = hp.shape
    n_exp = pends.shape[0]
    tb = _tile(n_tok, 128)
    grid_spec = pltpu.PrefetchScalarGridSpec(
        num_scalar_prefetch=3,
        grid=(n_tok // tb,),
        in_specs=[pl.BlockSpec((tb, dw), lambda i, *_: (i, 0))],
        out_specs=pl.BlockSpec(memory_space=pl.ANY),
        scratch_shapes=[pltpu.VMEM((tm, dw), hp.dtype),
                        pltpu.SemaphoreType.DMA(()),
                        pltpu.SemaphoreType.DMA(())],
    )
    return pl.pallas_call(
        functools.partial(_dispatch_kernel, tb=tb, tm=tm, n_exp=n_exp),
        grid_spec=grid_spec,
        out_shape=jax.ShapeDtypeStruct((n_rows, dw), hp.dtype),
        compiler_params=_params("arbitrary"),
        name="dispatch",
    )(dest, pends, pcounts, hp)


def _pack_bf16_pairs(hi, lo):
    hb = pltpu.bitcast(hi.astype(BF16).astype(F32), jnp.uint32)
    lb = pltpu.bitcast(lo.astype(BF16).astype(F32), jnp.uint32)
    return hb | (lb >> 16)


def _unpack_pairs_f32(p):
    return (pltpu.bitcast(p & jnp.uint32(0xFFFF0000), F32), pltpu.bitcast(p << 16, F32))


def _expert_block(be_ref, nv_ref, chg_ref, nxt_ref, used_ref, w_hbm, stage, sems, w_bf, o_ref,
                  compute, *, half_stride, tn, nj):
    j = pl.program_id(0)
    i = pl.program_id(1)
    change = chg_ref[i]
    valid = i < nv_ref[0]
    tm = o_ref.shape[0]
    steady = jnp.logical_and(valid, change == 0)
    step = tm // TAIL_LEVELS
    quarters = jnp.clip((used_ref[i] + step - 1) // step, 1, TAIL_LEVELS)

    def copies(e, jj):
        return [pltpu.make_async_copy(
            w_hbm.at[e, :, pl.ds(pl.multiple_of(part * half_stride + jj * tn, tn), tn)],
            stage.at[part], sems.at[part]) for part in range(2)]

    @pl.when(change > 0)
    def _():
        @pl.when(jnp.logical_and(j == 0, change == 1))
        def _():
            for cp in copies(be_ref[i], j):
                cp.start()

        for cp in copies(be_ref[i], j):
            cp.wait()

        def convert(part, rows):
            w = stage[part, rows, :].astype(BF16)
            w_bf[part, rows, :] = w
            return w

        compute(convert, tm)

        last = nxt_ref[i] < 0
        next_e = jnp.where(last, be_ref[0], nxt_ref[i])
        next_j = jnp.where(last, j + 1, j)

        @pl.when(next_j < nj)
        def _():
            for cp in copies(next_e, next_j):
                cp.start(priority=1)

    def current(part, rows):
        return w_bf[part, rows, :]

    for q in range(1, TAIL_LEVELS + 1):
        n_rows = tm * q // TAIL_LEVELS

        @pl.when(jnp.logical_and(steady, quarters == q))
        def _():
            compute(current, n_rows)
            if n_rows < tm:
                o_ref[n_rows:, :] = jnp.zeros((tm - n_rows, o_ref.shape[1]), o_ref.dtype)

    @pl.when(jnp.logical_not(valid))
    def _():
        o_ref[...] = jnp.zeros_like(o_ref)


def _expert_up_kernel(be_ref, nv_ref, chg_ref, nxt_ref, used_ref, xs_ref, w_hbm, bg_ref, bu_ref,
                      o_ref, stage, w_bf, sems, *, ff, tn, nj):
    def compute(get_w, n_rows):
        x_hi, x_lo = _unpack_pairs_f32(xs_ref[:n_rows, :])
        x_hi, x_lo = x_hi.astype(BF16), x_lo.astype(BF16)
        half = x_hi.shape[-1]
        top, bottom = slice(0, half), slice(half, 2 * half)
        gate = (jnp.dot(x_hi, get_w(0, top), preferred_element_type=F32)
                + jnp.dot(x_lo, get_w(0, bottom), preferred_element_type=F32) + bg_ref[0])
        up = (jnp.dot(x_hi, get_w(1, top), preferred_element_type=F32)
              + jnp.dot(x_lo, get_w(1, bottom), preferred_element_type=F32) + bu_ref[0])
        gate = jnp.minimum(gate, SWIGLU_LIMIT)
        up = jnp.clip(up, -SWIGLU_LIMIT, SWIGLU_LIMIT)
        act = (up + 1.0) * gate * (1.0 / (1.0 + jnp.exp(-SWIGLU_ALPHA * gate)))
        o_ref[:n_rows, :] = act.astype(o_ref.dtype)

    _expert_block(be_ref, nv_ref, chg_ref, nxt_ref, used_ref, w_hbm, stage, sems, w_bf, o_ref,
                  compute, half_stride=ff, tn=tn, nj=nj)


def _expert_scratch(k, tn):
    return [pltpu.VMEM((2, k, tn), F32),
            pltpu.VMEM((2, k, tn), BF16),
            pltpu.SemaphoreType.DMA((2,))]


def _expert_up(xs, meta, w_gu, b_gu, *, tm):
    n_rows, dw = xs.shape
    n_exp, d, ff2 = w_gu.shape
    ff = ff2 // 2
    assert d == 2 * dw
    tn = _tile(ff, 512)
    nj = ff // tn
    n_blocks = n_rows // tm
    b3 = b_gu.reshape(n_exp, 1, ff2)
    grid_spec = pltpu.PrefetchScalarGridSpec(
        num_scalar_prefetch=5,
        grid=(nj, n_blocks),
        in_specs=[
            pl.BlockSpec((tm, dw), lambda j, i, be, nv, *_: (jnp.minimum(i, nv[0] - 1), 0)),
            pl.BlockSpec(memory_space=pl.ANY),
            pl.BlockSpec((1, 1, tn), lambda j, i, be, *_: (be[i], 0, j)),
            pl.BlockSpec((1, 1, tn), lambda j, i, be, *_: (be[i], 0, nj + j)),
        ],
        out_specs=pl.BlockSpec((tm, tn), lambda j, i, *_: (i, j)),
        scratch_shapes=_expert_scratch(d, tn),
    )
    return pl.pallas_call(
        functools.partial(_expert_up_kernel, ff=ff, tn=tn, nj=nj),
        grid_spec=grid_spec,
        out_shape=jax.ShapeDtypeStruct((n_rows, ff), BF16),
        compiler_params=_params("arbitrary", "arbitrary"),
        name="expert_up",
    )(*meta, xs, w_gu, b3, b3)


def _expert_down_kernel(be_ref, nv_ref, chg_ref, nxt_ref, used_ref, h_ref, w_hbm, ba_ref, bb_ref,
                        o_ref, stage, w_bf, sems, *, half, tn, nj):
    def compute(get_w, n_rows):
        k_half = h_ref.shape[-1] // 2
        top, bottom = slice(0, k_half), slice(k_half, 2 * k_half)
        h_top, h_bottom = h_ref[:n_rows, top], h_ref[:n_rows, bottom]
        ya = (jnp.dot(h_top, get_w(0, top), preferred_element_type=F32)
              + jnp.dot(h_bottom, get_w(0, bottom), preferred_element_type=F32) + ba_ref[0])
        yb = (jnp.dot(h_top, get_w(1, top), preferred_element_type=F32)
              + jnp.dot(h_bottom, get_w(1, bottom), preferred_element_type=F32) + bb_ref[0])
        o_ref[:n_rows, :] = _pack_bf16_pairs(ya, yb)

    _expert_block(be_ref, nv_ref, chg_ref, nxt_ref, used_ref, w_hbm, stage, sems, w_bf, o_ref,
                  compute, half_stride=half, tn=tn, nj=nj)


def _expert_down(hs, meta, w_dn, b_dn, *, tm):
    n_rows, ff = hs.shape
    n_exp, _, d = w_dn.shape
    half = d // 2
    tn = _tile(half, 1024)
    nj = half // tn
    n_blocks = n_rows // tm
    b3 = b_dn.reshape(n_exp, 1, d)
    grid_spec = pltpu.PrefetchScalarGridSpec(
        num_scalar_prefetch=5,
        grid=(nj, n_blocks),
        in_specs=[
            pl.BlockSpec((tm, ff), lambda j, i, be, nv, *_: (jnp.minimum(i, nv[0] - 1), 0)),
            pl.BlockSpec(memory_space=pl.ANY),
            pl.BlockSpec((1, 1, tn), lambda j, i, be, *_: (be[i], 0, j)),
            pl.BlockSpec((1, 1, tn), lambda j, i, be, *_: (be[i], 0, nj + j)),
        ],
        out_specs=pl.BlockSpec((tm, tn), lambda j, i, *_: (i, j)),
        scratch_shapes=_expert_scratch(ff, tn),
    )
    return pl.pallas_call(
        functools.partial(_expert_down_kernel, half=half, tn=tn, nj=nj),
        grid_spec=grid_spec,
        out_shape=jax.ShapeDtypeStruct((n_rows, half), jnp.uint32),
        compiler_params=_params("arbitrary", "arbitrary"),
        name="expert_down",
    )(*meta, hs, w_dn, b3, b3)


def _combine_kernel(dest_ref, x_ref, gate_ref, g_ref, ys_hbm, o_ref, buf, sems, *, tb, n_steps):
    s = pl.program_id(0)
    cur = s & 1

    def row_copy(row, buf_slot, k, t):
        return pltpu.make_async_copy(ys_hbm.at[pl.ds(row, 1), :],
                                     buf.at[buf_slot, k, pl.ds(t, 1), :], sems.at[buf_slot])

    def gather(step, buf_slot):
        base = step * (tb * TOP_K)
        for t in range(tb):
            for k in range(TOP_K):
                row_copy(dest_ref[base + t * TOP_K + k], buf_slot, k, t).start(priority=k % 2)

    @pl.when(s == 0)
    def _():
        gather(0, 0)

    @pl.when(s + 1 < n_steps)
    def _():
        gather(s + 1, 1 - cur)

    def drain(t, carry):
        for k in range(TOP_K):
            row_copy(0, cur, 0, 0).wait()
        return carry

    lax.fori_loop(0, tb, drain, 0)

    half = x_ref.shape[-1] // 2
    acc_hi = x_ref[:, :half]
    acc_lo = x_ref[:, half:]
    for k in range(TOP_K):
        y_hi, y_lo = _unpack_pairs_f32(buf[cur, k])
        gate = gate_ref[:, k:k + 1]
        acc_hi = acc_hi + gate * y_hi
        acc_lo = acc_lo + gate * y_lo
    ms = (jnp.sum(acc_hi * acc_hi, axis=-1, keepdims=True)
          + jnp.sum(acc_lo * acc_lo, axis=-1, keepdims=True)) / (2 * half)
    r = lax.rsqrt(ms + NORM_EPS)
    o_ref[:, :half] = acc_hi * r * g_ref[:, :half]
    o_ref[:, half:] = acc_lo * r * g_ref[:, half:]


def _combine(x, gates, dest, ys, g):
    n_tok, d = x.shape
    tb = _tile(n_tok, 128)
    n_steps = n_tok // tb
    grid_spec = pltpu.PrefetchScalarGridSpec(
        num_scalar_prefetch=1,
        grid=(n_steps,),
        in_specs=[pl.BlockSpec((tb, d), lambda i, dest: (i, 0)),
                  pl.BlockSpec((tb, TOP_K), lambda i, dest: (i, 0)),
                  pl.BlockSpec((1, d), lambda i, dest: (0, 0)),
                  pl.BlockSpec(memory_space=pl.ANY)],
        out_specs=pl.BlockSpec((tb, d), lambda i, dest: (i, 0)),
        scratch_shapes=[pltpu.VMEM((2, TOP_K, tb, d // 2), ys.dtype),
                        pltpu.SemaphoreType.DMA((2,))],
    )
    return pl.pallas_call(
        functools.partial(_combine_kernel, tb=tb, n_steps=n_steps),
        grid_spec=grid_spec,
        out_shape=jax.ShapeDtypeStruct((n_tok, d), F32),
        compiler_params=_params("arbitrary"),
        name="combine",
    )(dest, x, gates, g.reshape(1, d), ys)


def _routing_tables(top_idx, rank, counts, tm):
    n_exp = counts.shape[0]
    flat_e = top_idx.reshape(-1)
    n_pairs = flat_e.shape[0]
    pcounts = (counts + tm - 1) // tm * tm
    pends = jnp.cumsum(pcounts)
    pstarts = pends - pcounts
    onehot = flat_e[:, None] == jnp.arange(n_exp, dtype=jnp.int32)[None, :]
    dest = jnp.sum(jnp.where(onehot, pstarts[None, :], 0), axis=1) + rank.reshape(-1)
    assert n_pairs % tm == 0
    n_blocks = n_pairs // tm + n_exp
    block_id = jnp.arange(n_blocks, dtype=jnp.int32)
    block_e = jnp.minimum(jnp.sum(pends[None, :] <= (block_id * tm)[:, None], axis=1),
                          n_exp - 1).astype(jnp.int32)
    n_valid = (pends[-1] // tm).astype(jnp.int32)
    prev_e = jnp.concatenate([jnp.full((1,), -1, jnp.int32), block_e[:-1]])
    is_change = (block_id < n_valid) & (block_e != prev_e)
    change_ord = jnp.cumsum(is_change.astype(jnp.int32))
    change = jnp.where(is_change, change_ord, 0).astype(jnp.int32)
    later = is_change[None, :] & (block_id[None, :] > block_id[:, None])
    next_e = jnp.where(jnp.any(later, axis=1), block_e[jnp.argmax(later, axis=1)], -1)
    real_end = pstarts + counts
    used = jnp.clip(real_end[block_e] - block_id * tm, 0, tm).astype(jnp.int32)
    meta = (block_e, n_valid.reshape(1), change, next_e.astype(jnp.int32), used)
    return (dest.astype(jnp.int32), pends.astype(jnp.int32), pcounts.astype(jnp.int32),
            meta, n_blocks * tm)


def kernel(x, attn_norm_g, w_in, lambda_q1, lambda_k1, lambda_q2, lambda_k2, diff_subln_g,
           sgu_ln_g, sgu_ln_b, sgu_w, sgu_b, rel_bias, w_out, ffn_norm_g, router_w, router_b,
           w_gate_up, b_gate_up, w_down, b_down, final_norm_g):
    batch, seq, d = x.shape
    n_tok = batch * seq
    hd = lambda_q1.shape[-1]
    n_maps = rel_bias.shape[1]
    n_heads = n_maps // 2
    q_cols = n_maps * hd
    attn_width = n_heads * diff_subln_g.shape[-1]
    n_exp = w_gate_up.shape[1]
    tq = _tile(seq, 512)
    tk = tq // 2
    tm = 512

    xt = x.reshape(n_tok, d)
    h = _rmsnorm_rows(xt, attn_norm_g[0], BF16)
    proj = _in_projection(h, w_in[0].astype(BF16))

    lam = (jnp.exp(jnp.sum(lambda_q1[0].astype(F32) * lambda_k1[0].astype(F32)))
           - jnp.exp(jnp.sum(lambda_q2[0].astype(F32) * lambda_k2[0].astype(F32)))
           + LAMBDA_INIT).reshape(1)
    bias = _bias_tiles(rel_bias, tq)
    far_bias = rel_bias[-1].astype(F32)
    att = _diff_attention(proj, bias, far_bias, lam, diff_subln_g[0],
                          batch=batch, seq=seq, n_heads=n_heads, hd=hd, tq=tq, tk=tk)
    sgu = _spatial_gating(proj, sgu_ln_g[0], sgu_ln_b[0], sgu_w[0], sgu_b[0],
                          u_col0=2 * q_cols + attn_width)
    x1 = _out_projection(att, sgu, w_out[0].astype(BF16), xt)

    hp, top_idx, gates, rank, counts = _router(x1, ffn_norm_g[0], router_w[0], router_b[0])
    dest, pends, pcounts, meta, n_rows = _routing_tables(
        top_idx, rank, counts[0].astype(jnp.int32), tm)
    xs = _dispatch(hp, dest, pends, pcounts, n_rows=n_rows, tm=tm)
    hs = _expert_up(xs, meta, w_gate_up[0], b_gate_up[0], tm=tm)
    ys = _expert_down(hs, meta, w_down[0], b_down[0], tm=tm)
    out = _combine(x1, gates, dest, ys, final_norm_g)
    return out.reshape(batch, seq, d)
```

```python
import functools
import math

import jax
import jax.numpy as jnp
from jax import lax
from jax.experimental import pallas as pl
from jax.experimental.pallas import tpu as pltpu

F32 = jnp.float32
BF16 = jnp.bfloat16

TOP_K = 4
MAX_DISTANCE = 128
SWIGLU_LIMIT = 7.0
SWIGLU_ALPHA = 1.702
NORM_EPS = 1e-5
LAMBDA_INIT = 0.8 - 0.6 * math.exp(0.0)
MASK_VALUE = -1e30
LOG2E = math.log2(math.e)

LANES = 128
BF16_SUBLANES = 16
TAIL_LEVELS = 4
VMEM_LIMIT_BYTES = 56 * 1024 * 1024


def _tile(dim, want):
    t = min(dim, want)
    while dim % t:
        t -= LANES
    assert t > 0, (dim, want)
    return t


def _params(*semantics, flags=None):
    return pltpu.CompilerParams(dimension_semantics=semantics,
                                vmem_limit_bytes=VMEM_LIMIT_BYTES, flags=flags)


def _rmsnorm_kernel(x_ref, g_ref, o_ref):
    x = x_ref[...]
    ms = jnp.mean(x * x, axis=-1, keepdims=True)
    o_ref[...] = (x * lax.rsqrt(ms + NORM_EPS) * g_ref[...]).astype(o_ref.dtype)


def _rmsnorm_rows(x, g, out_dtype):
    n, d = x.shape
    tr = _tile(n, 512)
    return pl.pallas_call(
        _rmsnorm_kernel,
        grid=(n // tr,),
        in_specs=[pl.BlockSpec((tr, d), lambda i: (i, 0)),
                  pl.BlockSpec((1, d), lambda i: (0, 0))],
        out_specs=pl.BlockSpec((tr, d), lambda i: (i, 0)),
        out_shape=jax.ShapeDtypeStruct((n, d), out_dtype),
        compiler_params=_params("parallel"),
        name="rmsnorm_rows",
    )(x, g.reshape(1, d))


def _in_projection_kernel(a_ref, b_ref, o_ref):
    acc = jnp.dot(a_ref[...], b_ref[...], preferred_element_type=F32)
    for c in range(o_ref.shape[0]):
        o_ref[c] = acc[:, c * LANES:(c + 1) * LANES].astype(o_ref.dtype)


def _in_projection(h, w):
    m, k = h.shape
    n = w.shape[1]
    tm, tn = _tile(m, 1024), _tile(n, 1024)
    return pl.pallas_call(
        _in_projection_kernel,
        grid=(m // tm, n // tn),
        in_specs=[pl.BlockSpec((tm, k), lambda i, j: (i, 0)),
                  pl.BlockSpec((k, tn), lambda i, j: (0, j))],
        out_specs=pl.BlockSpec((tn // LANES, tm, LANES), lambda i, j: (j, i, 0)),
        out_shape=jax.ShapeDtypeStruct((n // LANES, m, LANES), BF16),
        compiler_params=_params("parallel", "parallel"),
        name="in_projection",
    )(h, w)


def _out_projection_kernel(a_ref, s_ref, wa_ref, ws_ref, x_ref, o_ref):
    acc = jnp.dot(a_ref[...], wa_ref[...], preferred_element_type=F32)
    acc += jnp.dot(s_ref[...], ws_ref[...], preferred_element_type=F32)
    o_ref[...] = x_ref[...] + acc


def _out_projection(att, sgu, w, x):
    m, ka = att.shape
    ks = sgu.shape[1]
    n = w.shape[1]
    assert ka == ks and w.shape[0] == ka + ks
    tm, tn = _tile(m, 1024), _tile(n, 512)
    return pl.pallas_call(
        _out_projection_kernel,
        grid=(m // tm, n // tn),
        in_specs=[pl.BlockSpec((tm, ka), lambda i, j: (i, 0)),
                  pl.BlockSpec((tm, ks), lambda i, j: (i, 0)),
                  pl.BlockSpec((ka, tn), lambda i, j: (0, j)),
                  pl.BlockSpec((ks, tn), lambda i, j: (1, j)),
                  pl.BlockSpec((tm, tn), lambda i, j: (i, j))],
        out_specs=pl.BlockSpec((tm, tn), lambda i, j: (i, j)),
        out_shape=jax.ShapeDtypeStruct((m, n), F32),
        compiler_params=_params("parallel", "parallel"),
        name="out_projection",
    )(att, sgu, w, w, x)


def _bias_tiles_kernel(tbl_ref, o_ref, *, tq, n_buckets):
    h = pl.program_id(0)
    max_exact = n_buckets // 2
    nb = tq // LANES
    key = lax.broadcasted_iota(jnp.int32, (LANES, LANES), 0)
    qry = lax.broadcasted_iota(jnp.int32, (LANES, LANES), 1)

    def lookup(dist, m):
        d = jnp.maximum(dist, 1).astype(F32)
        large = max_exact + (jnp.log(d / max_exact) / math.log(MAX_DISTANCE / max_exact)
                             * (n_buckets - max_exact)).astype(jnp.int32)
        large = jnp.minimum(large, n_buckets - 1)
        bucket = jnp.where(dist < max_exact, dist, large)
        val = jnp.zeros(dist.shape, F32)
        for b in range(n_buckets):
            val = jnp.where(bucket == b, tbl_ref[b, 2 * h + m], val)
        return jnp.where(dist >= 0, val * LOG2E, MASK_VALUE)

    for w in range(2):
        for delta in range(-(nb - 1), nb):
            offset = delta * LANES + w * tq
            for m in range(2):
                if offset + LANES - 1 < 0:
                    block = jnp.full((LANES, LANES), MASK_VALUE, F32)
                elif offset - (LANES - 1) >= MAX_DISTANCE:
                    block = jnp.full((LANES, LANES), tbl_ref[n_buckets - 1, 2 * h + m], F32) * LOG2E
                else:
                    block = lookup(qry - key + offset, m)
                for kc in range(max(0, -delta), min(nb, nb - delta)):
                    qc = kc + delta
                    o_ref[0, w, kc * LANES:(kc + 1) * LANES,
                          m * tq + qc * LANES:m * tq + (qc + 1) * LANES] = block


def _bias_tiles(rel_bias, tq):
    n_buckets, n_maps = rel_bias.shape
    n_heads = n_maps // 2
    assert tq >= MAX_DISTANCE
    return pl.pallas_call(
        functools.partial(_bias_tiles_kernel, tq=tq, n_buckets=n_buckets),
        grid=(n_heads,),
        in_specs=[pl.BlockSpec(memory_space=pltpu.SMEM)],
        out_specs=pl.BlockSpec((1, 2, tq, 2 * tq), lambda h: (h, 0, 0, 0)),
        out_shape=jax.ShapeDtypeStruct((n_heads, 2, tq, 2 * tq), F32),
        compiler_params=_params("parallel"),
        name="bias_tiles",
    )(rel_bias.astype(F32))


def _attention_kernel(far_ref, lam_ref, q_ref, k_ref, v_ref, bias_ref, g_ref, o_ref,
                      vt_ref, sa_ref, sb_ref, acc_ref, *, tq, tk, hd):
    i = pl.program_id(2)
    n_group, seq, vd = v_ref.shape
    head0 = pl.program_id(1) * n_group
    group = range(n_group)
    sub = tq // tk
    assert sub == 2

    @pl.when(i == 0)
    def _():
        ones_row = lax.broadcasted_iota(jnp.int32, (BF16_SUBLANES, tk), 0) == 0
        for g in group:
            for c in range(seq // tk):
                vt_ref[g, c, :vd, :] = (
                    v_ref[g, c * tk:(c + 1) * tk, :].astype(F32).T.astype(BF16))
                vt_ref[g, c, vd:, :] = jnp.where(ones_row, 1.0, 0.0).astype(BF16)

    col = lax.broadcasted_iota(jnp.int32, (1, 2 * tq), 1)
    qqt, far = [], []
    for g in group:
        qt = (q_ref[g].astype(F32) * (hd ** -0.5 * LOG2E)).T
        dim = lax.broadcasted_iota(jnp.int32, qt.shape, 0)
        qqt.append(jnp.concatenate([jnp.where(dim < hd, qt, 0.0),
                                    jnp.where(dim >= hd, qt, 0.0)], axis=1).astype(BF16))
        head = head0 + g
        far.append(jnp.where(col < tq, far_ref[2 * head], far_ref[2 * head + 1]) * LOG2E)

    def scores(g, kb):
        kblk = k_ref[g, pl.ds(pl.multiple_of(kb * tk, tk), tk), :]
        return jnp.dot(kblk, qqt[g], preferred_element_type=F32)

    def online_update(g, s, kb, m_prev, shift):
        m_new = jnp.maximum(m_prev, jnp.max(s, axis=0, keepdims=True) + shift)
        alpha = jnp.exp2(m_prev - m_new)
        p = jnp.exp2((s - (m_new - shift)).astype(BF16))
        acc_ref[g] = alpha * acc_ref[g] + jnp.dot(vt_ref[g, kb], p, preferred_element_type=F32)
        return m_new

    kb_prev = sub * jnp.maximum(i - 1, 0)
    no_prev = jnp.where(i >= 1, 0.0, MASK_VALUE)
    n_pairs = jnp.maximum(i - 1, 0)
    last = jnp.maximum(sub * n_pairs - 1, 0)
    acc_ref[...] = jnp.zeros_like(acc_ref)
    m = [jnp.full((1, 2 * tq), MASK_VALUE, F32) for _ in group]

    for g in group:
        sa_ref[g] = scores(g, sub * i) + bias_ref[g, 0, :tk, :]
    for g in group:
        sb_ref[g] = scores(g, sub * i + 1) + bias_ref[g, 0, tk:, :]
        m[g] = online_update(g, sa_ref[g], sub * i, m[g], 0.0)
    for g in group:
        sa_ref[g] = scores(g, kb_prev) + (bias_ref[g, 1, :tk, :] + no_prev)
        m[g] = online_update(g, sb_ref[g], sub * i + 1, m[g], 0.0)
    for g in group:
        sb_ref[g] = scores(g, kb_prev + 1) + (bias_ref[g, 1, tk:, :] + no_prev)
        m[g] = online_update(g, sa_ref[g], kb_prev, m[g], 0.0)
    for g in group:
        sa_ref[g] = scores(g, 0)
        m[g] = online_update(g, sb_ref[g], kb_prev + 1, m[g], 0.0)

    def far_pair(n, m_run):
        m_run = list(m_run)
        kb = sub * n
        for g in group:
            sb_ref[g] = scores(g, kb + 1)
            m_run[g] = online_update(g, sa_ref[g], kb, m_run[g], far[g])
        for g in group:
            sa_ref[g] = scores(g, jnp.minimum(kb + 2, last))
            m_run[g] = online_update(g, sb_ref[g], kb + 1, m_run[g], far[g])
        return tuple(m_run)

    lax.fori_loop(0, n_pairs, far_pair, tuple(m))

    for g in group:
        acc = acc_ref[g]
        o = acc[:vd] * (1.0 / acc[vd:vd + 1])
        a = o[:, :tq] - lam_ref[0] * o[:, tq:]
        ms = jnp.mean(a * a, axis=0, keepdims=True)
        y = a * lax.rsqrt(ms + NORM_EPS) * g_ref[...] * (1.0 - LAMBDA_INIT)
        o_ref[:, g * vd:(g + 1) * vd] = y.T.astype(o_ref.dtype)


def _diff_attention(proj, bias, far_bias, lam, subln_g, *, batch, seq, n_heads, hd, tq, tk):
    vd = subln_g.shape[-1]
    assert 2 * hd == LANES and vd == LANES
    nq = seq // tq
    n_group = 2
    assert n_heads % n_group == 0
    k_blk0 = n_heads // n_group
    v_blk0 = 2 * n_heads // n_group
    grid_spec = pltpu.PrefetchScalarGridSpec(
        num_scalar_prefetch=2,
        grid=(batch, n_heads // n_group, nq),
        in_specs=[
            pl.BlockSpec((n_group, tq, LANES), lambda b, h, i, *_: (h, b * nq + i, 0)),
            pl.BlockSpec((n_group, seq, LANES), lambda b, h, i, *_: (k_blk0 + h, b, 0)),
            pl.BlockSpec((n_group, seq, LANES), lambda b, h, i, *_: (v_blk0 + h, b, 0)),
            pl.BlockSpec((n_group, 2, tq, 2 * tq), lambda b, h, i, *_: (h, 0, 0, 0)),
            pl.BlockSpec((vd, 1), lambda b, h, i, *_: (0, 0)),
        ],
        out_specs=pl.BlockSpec((tq, n_group * vd), lambda b, h, i, *_: (b * nq + i, h)),
        scratch_shapes=[pltpu.VMEM((n_group, seq // tk, vd + BF16_SUBLANES, tk), BF16),
                        pltpu.VMEM((n_group, tk, 2 * tq), F32),
                        pltpu.VMEM((n_group, tk, 2 * tq), F32),
                        pltpu.VMEM((n_group, vd + BF16_SUBLANES, 2 * tq), F32)],
    )
    return pl.pallas_call(
        functools.partial(_attention_kernel, tq=tq, tk=tk, hd=hd),
        grid_spec=grid_spec,
        out_shape=jax.ShapeDtypeStruct((batch * seq, n_heads * vd), BF16),
        compiler_params=_params("parallel", "parallel", "arbitrary"),
        name="diff_attention",
    )(far_bias, lam, proj, proj, proj, bias, subln_g.reshape(vd, 1))


def _gelu(x):
    return 0.5 * x * (1.0 + lax.erf(x * math.sqrt(0.5)))


def _sgu_kernel(u_ref, v_ref, lng_ref, lnb_ref, w_ref, bt_ref, o_ref):
    n_heads, rows, hdim = u_ref.shape
    chunk = w_ref.shape[-1]
    width = n_heads * hdim
    u = _gelu(u_ref[...].astype(F32))
    v = _gelu(v_ref[...].astype(F32))
    mu = jnp.sum(jnp.sum(v, axis=0), axis=-1, keepdims=True) / width
    vc = v - mu
    var = jnp.sum(jnp.sum(vc * vc, axis=0), axis=-1, keepdims=True) / width
    vn = (vc * lax.rsqrt(var + NORM_EPS) * lng_ref[...] + lnb_ref[...]).astype(BF16)
    r = lax.broadcasted_iota(jnp.int32, (chunk, chunk), 0)
    c = lax.broadcasted_iota(jnp.int32, (chunk, chunk), 1)
    causal = r >= c
    for hh in range(n_heads):
        w = jnp.where(causal, w_ref[hh], 0.0).astype(BF16)
        for c0 in range(0, rows, chunk):
            sl = slice(c0, c0 + chunk)
            y = jnp.dot(w, vn[hh, sl], preferred_element_type=F32) + bt_ref[:, hh:hh + 1]
            o_ref[sl, hh * hdim:(hh + 1) * hdim] = (u[hh, sl] * y).astype(o_ref.dtype)


def _spatial_gating(proj, ln_g, ln_b, w_s, b_s, *, u_col0):
    _, n_tok, hdim = proj.shape
    n_heads, chunk, _ = w_s.shape
    width = ln_g.shape[-1]
    assert width == n_heads * hdim and u_col0 % width == 0
    ub = u_col0 // width
    rows = chunk * min(4, n_tok // chunk)
    assert n_tok % rows == 0
    return pl.pallas_call(
        _sgu_kernel,
        grid=(n_tok // rows,),
        in_specs=[pl.BlockSpec((n_heads, rows, hdim), lambda i: (ub, i, 0)),
                  pl.BlockSpec((n_heads, rows, hdim), lambda i: (ub + 1, i, 0)),
                  pl.BlockSpec((n_heads, 1, hdim), lambda i: (0, 0, 0)),
                  pl.BlockSpec((n_heads, 1, hdim), lambda i: (0, 0, 0)),
                  pl.BlockSpec((n_heads, chunk, chunk), lambda i: (0, 0, 0)),
                  pl.BlockSpec((chunk, n_heads), lambda i: (0, 0))],
        out_specs=pl.BlockSpec((rows, width), lambda i: (i, 0)),
        out_shape=jax.ShapeDtypeStruct((n_tok, width), BF16),
        compiler_params=_params("parallel"),
        name="spatial_gating",
    )(proj, proj, ln_g.reshape(n_heads, 1, hdim), ln_b.reshape(n_heads, 1, hdim), w_s, b_s.T)


def _split_bf16(x):
    hi = x.astype(BF16)
    return hi, (x - hi.astype(F32)).astype(BF16)


def _router_kernel(x_ref, g_ref, rw_ref, rb_ref, hp_ref, idx_ref, gate_ref, rank_ref, cnt_ref):
    @pl.when(pl.program_id(0) == 0)
    def _():
        cnt_ref[...] = jnp.zeros_like(cnt_ref)

    x = x_ref[...]
    ms = jnp.mean(x * x, axis=-1, keepdims=True)
    h = x * lax.rsqrt(ms + NORM_EPS) * g_ref[...]
    half = h.shape[-1] // 2
    hp_ref[...] = _pack_bf16_pairs(h[:, :half], h[:, half:])

    h_hi, h_lo = _split_bf16(h)
    w_hi, w_lo = _split_bf16(rw_ref[...])
    logits = (jnp.dot(h_hi, w_hi, preferred_element_type=F32)
              + jnp.dot(h_lo, w_hi, preferred_element_type=F32)
              + jnp.dot(h_hi, w_lo, preferred_element_type=F32)) + rb_ref[...]
    tr, n_exp = logits.shape
    lane = lax.broadcasted_iota(jnp.int32, logits.shape, 1)
    vals, idxs = [], []
    for _ in range(TOP_K):
        top = jnp.max(logits, axis=-1, keepdims=True)
        idx = jnp.min(jnp.where(logits == top, lane, n_exp), axis=-1, keepdims=True)
        vals.append(top)
        idxs.append(idx)
        logits = jnp.where(lane == idx, -jnp.inf, logits)
    exps = [jnp.exp(v - vals[0]) for v in vals]
    denom = exps[0]
    for e in exps[1:]:
        denom = denom + e

    earlier = (lax.broadcasted_iota(jnp.int32, (tr, tr), 1)
               < lax.broadcasted_iota(jnp.int32, (tr, tr), 0))
    earlier = jnp.where(earlier, 1.0, 0.0).astype(BF16)
    count = cnt_ref[...]
    for k in range(TOP_K):
        chosen = lane == idxs[k]
        onehot = jnp.where(chosen, 1.0, 0.0)
        before = jnp.dot(earlier, onehot.astype(BF16), preferred_element_type=F32) + count
        rank = jnp.sum(jnp.where(chosen, before, 0.0), axis=-1, keepdims=True)
        idx_ref[:, k:k + 1] = idxs[k]
        gate_ref[:, k:k + 1] = exps[k] / denom
        rank_ref[:, k:k + 1] = rank.astype(jnp.int32)
        count = count + jnp.sum(onehot, axis=0, keepdims=True)
    cnt_ref[...] = count


def _router(x, g, rw, rb):
    n, d = x.shape
    n_exp = rw.shape[1]
    tr = _tile(n, 512)
    assert n * TOP_K < 2 ** 24
    return pl.pallas_call(
        _router_kernel,
        grid=(n // tr,),
        in_specs=[pl.BlockSpec((tr, d), lambda i: (i, 0)),
                  pl.BlockSpec((1, d), lambda i: (0, 0)),
                  pl.BlockSpec((d, n_exp), lambda i: (0, 0)),
                  pl.BlockSpec((1, n_exp), lambda i: (0, 0))],
        out_specs=[pl.BlockSpec((tr, d // 2), lambda i: (i, 0)),
                   pl.BlockSpec((tr, TOP_K), lambda i: (i, 0)),
                   pl.BlockSpec((tr, TOP_K), lambda i: (i, 0)),
                   pl.BlockSpec((tr, TOP_K), lambda i: (i, 0)),
                   pl.BlockSpec((1, n_exp), lambda i: (0, 0))],
        out_shape=[jax.ShapeDtypeStruct((n, d // 2), jnp.uint32),
                   jax.ShapeDtypeStruct((n, TOP_K), jnp.int32),
                   jax.ShapeDtypeStruct((n, TOP_K), F32),
                   jax.ShapeDtypeStruct((n, TOP_K), jnp.int32),
                   jax.ShapeDtypeStruct((1, n_exp), F32)],
        compiler_params=_params("arbitrary"),
        name="router",
    )(x, g.reshape(1, d), rw, rb.reshape(1, n_exp))


def _dispatch_kernel(dest_ref, pends_ref, pcnt_ref, h_ref, xs_hbm, zbuf, zsem, sem,
                     *, tb, tm, n_exp):
    s = pl.program_id(0)

    n_rows = xs_hbm.shape[0]

    def zero_block(start):
        return pltpu.make_async_copy(
            zbuf, xs_hbm.at[pl.ds(pl.multiple_of(start, tm), tm), :], zsem)

    def zero_fill(action):
        for e in range(n_exp):
            @pl.when(pcnt_ref[e] > 0)
            def _():
                action(zero_block(pends_ref[e] - tm))
        for b in range(n_exp):
            start = pends_ref[n_exp - 1] + b * tm

            @pl.when(start < n_rows)
            def _():
                action(zero_block(start))

    @pl.when(s == 0)
    def _():
        zbuf[...] = jnp.zeros_like(zbuf)
        zero_fill(lambda copy: copy.start())
        zero_fill(lambda copy: copy.wait())

    def row_copy(t, slot):
        return pltpu.make_async_copy(h_ref.at[pl.ds(t, 1), :],
                                     xs_hbm.at[pl.ds(slot, 1), :], sem)

    base = s * (tb * TOP_K)
    for t in range(tb):
        for k in range(TOP_K):
            row_copy(t, dest_ref[base + t * TOP_K + k]).start(priority=k % 2)

    def drain(t, carry):
        for k in range(TOP_K):
            row_copy(0, 0).wait()
        return carry

    lax.fori_loop(0, tb, drain, 0)


def _dispatch(hp, dest, pends, pcounts, *, n_rows, tm):
    n_tok, dw = hp.shape
    n_exp = pends.shape[0]
    tb = _tile(n_tok, 128)
    grid_spec = pltpu.PrefetchScalarGridSpec(
        num_scalar_prefetch=3,
        grid=(n_tok // tb,),
        in_specs=[pl.BlockSpec((tb, dw), lambda i, *_: (i, 0))],
        out_specs=pl.BlockSpec(memory_space=pl.ANY),
        scratch_shapes=[pltpu.VMEM((tm, dw), hp.dtype),
                        pltpu.SemaphoreType.DMA(()),
                        pltpu.SemaphoreType.DMA(())],
    )
    return pl.pallas_call(
        functools.partial(_dispatch_kernel, tb=tb, tm=tm, n_exp=n_exp),
        grid_spec=grid_spec,
        out_shape=jax.ShapeDtypeStruct((n_rows, dw), hp.dtype),
        compiler_params=_params("arbitrary"),
        name="dispatch",
    )(dest, pends, pcounts, hp)


def _pack_bf16_pairs(hi, lo):
    hb = pltpu.bitcast(hi.astype(BF16).astype(F32), jnp.uint32)
    lb = pltpu.bitcast(lo.astype(BF16).astype(F32), jnp.uint32)
    return hb | (lb >> 16)


def _unpack_pairs_f32(p):
    return (pltpu.bitcast(p & jnp.uint32(0xFFFF0000), F32), pltpu.bitcast(p << 16, F32))


def _expert_block(be_ref, nv_ref, chg_ref, nxt_ref, used_ref, w_hbm, stage, sems, w_bf, o_ref,
                  compute, *, half_stride, tn, nj):
    j = pl.program_id(0)
    i = pl.program_id(1)
    change = chg_ref[i]
    valid = i < nv_ref[0]
    tm = o_ref.shape[0]
    steady = jnp.logical_and(valid, change == 0)
    step = tm // TAIL_LEVELS
    quarters = jnp.clip((used_ref[i] + step - 1) // step, 1, TAIL_LEVELS)

    def copies(e, jj):
        return [pltpu.make_async_copy(
            w_hbm.at[e, :, pl.ds(pl.multiple_of(part * half_stride + jj * tn, tn), tn)],
            stage.at[part], sems.at[part]) for part in range(2)]

    @pl.when(change > 0)
    def _():
        @pl.when(jnp.logical_and(j == 0, change == 1))
        def _():
            for cp in copies(be_ref[i], j):
                cp.start()

        for cp in copies(be_ref[i], j):
            cp.wait()

        def convert(part, rows):
            w = stage[part, rows, :].astype(BF16)
            w_bf[part, rows, :] = w
            return w

        compute(convert, tm)

        last = nxt_ref[i] < 0
        next_e = jnp.where(last, be_ref[0], nxt_ref[i])
        next_j = jnp.where(last, j + 1, j)

        @pl.when(next_j < nj)
        def _():
            for cp in copies(next_e, next_j):
                cp.start(priority=1)

    def current(part, rows):
        return w_bf[part, rows, :]

    for q in range(1, TAIL_LEVELS + 1):
        n_rows = tm * q // TAIL_LEVELS

        @pl.when(jnp.logical_and(steady, quarters == q))
        def _():
            compute(current, n_rows)
            if n_rows < tm:
                o_ref[n_rows:, :] = jnp.zeros((tm - n_rows, o_ref.shape[1]), o_ref.dtype)

    @pl.when(jnp.logical_not(valid))
    def _():
        o_ref[...] = jnp.zeros_like(o_ref)


def _expert_up_kernel(be_ref, nv_ref, chg_ref, nxt_ref, used_ref, xs_ref, w_hbm, bg_ref, bu_ref,
                      o_ref, stage, w_bf, sems, *, ff, tn, nj):
    def compute(get_w, n_rows):
        x_hi, x_lo = _unpack_pairs_f32(xs_ref[:n_rows, :])
        x_hi, x_lo = x_hi.astype(BF16), x_lo.astype(BF16)
        half = x_hi.shape[-1]
        top, bottom = slice(0, half), slice(half, 2 * half)
        gate = (jnp.dot(x_hi, get_w(0, top), preferred_element_type=F32)
                + jnp.dot(x_lo, get_w(0, bottom), preferred_element_type=F32) + bg_ref[0])
        up = (jnp.dot(x_hi, get_w(1, top), preferred_element_type=F32)
              + jnp.dot(x_lo, get_w(1, bottom), preferred_element_type=F32) + bu_ref[0])
        gate = jnp.minimum(gate, SWIGLU_LIMIT)
        up = jnp.clip(up, -SWIGLU_LIMIT, SWIGLU_LIMIT)
        act = (up + 1.0) * gate * (1.0 / (1.0 + jnp.exp(-SWIGLU_ALPHA * gate)))
        o_ref[:n_rows, :] = act.astype(o_ref.dtype)

    _expert_block(be_ref, nv_ref, chg_ref, nxt_ref, used_ref, w_hbm, stage, sems, w_bf, o_ref,
                  compute, half_stride=ff, tn=tn, nj=nj)


def _expert_scratch(k, tn):
    return [pltpu.VMEM((2, k, tn), F32),
            pltpu.VMEM((2, k, tn), BF16),
            pltpu.SemaphoreType.DMA((2,))]


def _expert_up(xs, meta, w_gu, b_gu, *, tm):
    n_rows, dw = xs.shape
    n_exp, d, ff2 = w_gu.shape
    ff = ff2 // 2
    assert d == 2 * dw
    tn = _tile(ff, 512)
    nj = ff // tn
    n_blocks = n_rows // tm
    b3 = b_gu.reshape(n_exp, 1, ff2)
    grid_spec = pltpu.PrefetchScalarGridSpec(
        num_scalar_prefetch=5,
        grid=(nj, n_blocks),
        in_specs=[
            pl.BlockSpec((tm, dw), lambda j, i, be, nv, *_: (jnp.minimum(i, nv[0] - 1), 0)),
            pl.BlockSpec(memory_space=pl.ANY),
            pl.BlockSpec((1, 1, tn), lambda j, i, be, *_: (be[i], 0, j)),
            pl.BlockSpec((1, 1, tn), lambda j, i, be, *_: (be[i], 0, nj + j)),
        ],
        out_specs=pl.BlockSpec((tm, tn), lambda j, i, *_: (i, j)),
        scratch_shapes=_expert_scratch(d, tn),
    )
    return pl.pallas_call(
        functools.partial(_expert_up_kernel, ff=ff, tn=tn, nj=nj),
        grid_spec=grid_spec,
        out_shape=jax.ShapeDtypeStruct((n_rows, ff), BF16),
        compiler_params=_params("arbitrary", "arbitrary"),
        name="expert_up",
    )(*meta, xs, w_gu, b3, b3)


def _expert_down_kernel(be_ref, nv_ref, chg_ref, nxt_ref, used_ref, h_ref, w_hbm, ba_ref, bb_ref,
                        o_ref, stage, w_bf, sems, *, half, tn, nj):
    def compute(get_w, n_rows):
        k_half = h_ref.shape[-1] // 2
        top, bottom = slice(0, k_half), slice(k_half, 2 * k_half)
        h_top, h_bottom = h_ref[:n_rows, top], h_ref[:n_rows, bottom]
        ya = (jnp.dot(h_top, get_w(0, top), preferred_element_type=F32)
              + jnp.dot(h_bottom, get_w(0, bottom), preferred_element_type=F32) + ba_ref[0])
        yb = (jnp.dot(h_top, get_w(1, top), preferred_element_type=F32)
              + jnp.dot(h_bottom, get_w(1, bottom), preferred_element_type=F32) + bb_ref[0])
        o_ref[:n_rows, :] = _pack_bf16_pairs(ya, yb)

    _expert_block(be_ref, nv_ref, chg_ref, nxt_ref, used_ref, w_hbm, stage, sems, w_bf, o_ref,
                  compute, half_stride=half, tn=tn, nj=nj)


def _expert_down(hs, meta, w_dn, b_dn, *, tm):
    n_rows, ff = hs.shape
    n_exp, _, d = w_dn.shape
    half = d // 2
    tn = _tile(half, 1024)
    nj = half // tn
    n_blocks = n_rows // tm
    b3 = b_dn.reshape(n_exp, 1, d)
    grid_spec = pltpu.PrefetchScalarGridSpec(
        num_scalar_prefetch=5,
        grid=(nj, n_blocks),
        in_specs=[
            pl.BlockSpec((tm, ff), lambda j, i, be, nv, *_: (jnp.minimum(i, nv[0] - 1), 0)),
            pl.BlockSpec(memory_space=pl.ANY),
            pl.BlockSpec((1, 1, tn), lambda j, i, be, *_: (be[i], 0, j)),
            pl.BlockSpec((1, 1, tn), lambda j, i, be, *_: (be[i], 0, nj + j)),
        ],
        out_specs=pl.BlockSpec((tm, tn), lambda j, i, *_: (i, j)),
        scratch_shapes=_expert_scratch(ff, tn),
    )
    return pl.pallas_call(
        functools.partial(_expert_down_kernel, half=half, tn=tn, nj=nj),
        grid_spec=grid_spec,
        out_shape=jax.ShapeDtypeStruct((n_rows, half), jnp.uint32),
        compiler_params=_params("arbitrary", "arbitrary"),
        name="expert_down",
    )(*meta, hs, w_dn, b3, b3)


def _combine_kernel(dest_ref, x_ref, gate_ref, g_ref, ys_hbm, o_ref, buf, sems, *, tb, n_steps):
    s = pl.program_id(0)
    cur = s & 1

    def row_copy(row, buf_slot, k, t):
        return pltpu.make_async_copy(ys_hbm.at[pl.ds(row, 1), :],
                                     buf.at[buf_slot, k, pl.ds(t, 1), :], sems.at[buf_slot])

    def gather(step, buf_slot):
        base = step * (tb * TOP_K)
        for t in range(tb):
            for k in range(TOP_K):
                row_copy(dest_ref[base + t * TOP_K + k], buf_slot, k, t).start(priority=k % 2)

    @pl.when(s == 0)
    def _():
        gather(0, 0)

    @pl.when(s + 1 < n_steps)
    def _():
        gather(s + 1, 1 - cur)

    def drain(t, carry):
        for k in range(TOP_K):
            row_copy(0, cur, 0, 0).wait()
        return carry

    lax.fori_loop(0, tb, drain, 0)

    half = x_ref.shape[-1] // 2
    acc_hi = x_ref[:, :half]
    acc_lo = x_ref[:, half:]
    for k in range(TOP_K):
        y_hi, y_lo = _unpack_pairs_f32(buf[cur, k])
        gate = gate_ref[:, k:k + 1]
        acc_hi = acc_hi + gate * y_hi
        acc_lo = acc_lo + gate * y_lo
    ms = (jnp.sum(acc_hi * acc_hi, axis=-1, keepdims=True)
          + jnp.sum(acc_lo * acc_lo, axis=-1, keepdims=True)) / (2 * half)
    r = lax.rsqrt(ms + NORM_EPS)
    o_ref[:, :half] = acc_hi * r * g_ref[:, :half]
    o_ref[:, half:] = acc_lo * r * g_ref[:, half:]


def _combine(x, gates, dest, ys, g):
    n_tok, d = x.shape
    tb = _tile(n_tok, 128)
    n_steps = n_tok // tb
    grid_spec = pltpu.PrefetchScalarGridSpec(
        num_scalar_prefetch=1,
        grid=(n_steps,),
        in_specs=[pl.BlockSpec((tb, d), lambda i, dest: (i, 0)),
                  pl.BlockSpec((tb, TOP_K), lambda i, dest: (i, 0)),
                  pl.BlockSpec((1, d), lambda i, dest: (0, 0)),
                  pl.BlockSpec(memory_space=pl.ANY)],
        out_specs=pl.BlockSpec((tb, d), lambda i, dest: (i, 0)),
        scratch_shapes=[pltpu.VMEM((2, TOP_K, tb, d // 2), ys.dtype),
                        pltpu.SemaphoreType.DMA((2,))],
    )
    return pl.pallas_call(
        functools.partial(_combine_kernel, tb=tb, n_steps=n_steps),
        grid_spec=grid_spec,
        out_shape=jax.ShapeDtypeStruct((n_tok, d), F32),
        compiler_params=_params("arbitrary"),
        name="combine",
    )(dest, x, gates, g.reshape(1, d), ys)


def _routing_tables(top_idx, rank, counts, tm):
    n_exp = counts.shape[0]
    flat_e = top_idx.reshape(-1)
    n_pairs = flat_e.shape[0]
    pcounts = (counts + tm - 1) // tm * tm
    pends = jnp.cumsum(pcounts)
    pstarts = pends - pcounts
    onehot = flat_e[:, None] == jnp.arange(n_exp, dtype=jnp.int32)[None, :]
    dest = jnp.sum(jnp.where(onehot, pstarts[None, :], 0), axis=1) + rank.reshape(-1)
    assert n_pairs % tm == 0
    n_blocks = n_pairs // tm + n_exp
    block_id = jnp.arange(n_blocks, dtype=jnp.int32)
    block_e = jnp.minimum(jnp.sum(pends[None, :] <= (block_id * tm)[:, None], axis=1),
                          n_exp - 1).astype(jnp.int32)
    n_valid = (pends[-1] // tm).astype(jnp.int32)
    prev_e = jnp.concatenate([jnp.full((1,), -1, jnp.int32), block_e[:-1]])
    is_change = (block_id < n_valid) & (block_e != prev_e)
    change_ord = jnp.cumsum(is_change.astype(jnp.int32))
    change = jnp.where(is_change, change_ord, 0).astype(jnp.int32)
    later = is_change[None, :] & (block_id[None, :] > block_id[:, None])
    next_e = jnp.where(jnp.any(later, axis=1), block_e[jnp.argmax(later, axis=1)], -1)
    real_end = pstarts + counts
    used = jnp.clip(real_end[block_e] - block_id * tm, 0, tm).astype(jnp.int32)
    meta = (block_e, n_valid.reshape(1), change, next_e.astype(jnp.int32), used)
    return (dest.astype(jnp.int32), pends.astype(jnp.int32), pcounts.astype(jnp.int32),
            meta, n_blocks * tm)


def kernel(x, attn_norm_g, w_in, lambda_q1, lambda_k1, lambda_q2, lambda_k2, diff_subln_g,
           sgu_ln_g, sgu_ln_b, sgu_w, sgu_b, rel_bias, w_out, ffn_norm_g, router_w, router_b,
           w_gate_up, b_gate_up, w_down, b_down, final_norm_g):
    batch, seq, d = x.shape
    n_tok = batch * seq
    hd = lambda_q1.shape[-1]
    n_maps = rel_bias.shape[1]
    n_heads = n_maps // 2
    q_cols = n_maps * hd
    attn_width = n_heads * diff_subln_g.shape[-1]
    tq = _tile(seq, 512)
    tk = tq // 2
    tm = 512

    xt = x.reshape(n_tok, d)
    h = _rmsnorm_rows(xt, attn_norm_g[0], BF16)
    proj = _in_projection(h, w_in[0].astype(BF16))

    lam = (jnp.exp(jnp.sum(lambda_q1[0].astype(F32) * lambda_k1[0].astype(F32)))
           - jnp.exp(jnp.sum(lambda_q2[0].astype(F32) * lambda_k2[0].astype(F32)))
           + LAMBDA_INIT).reshape(1)
    bias = _bias_tiles(rel_bias, tq)
    far_bias = rel_bias[-1].astype(F32)
    att = _diff_attention(proj, bias, far_bias, lam, diff_subln_g[0],
                          batch=batch, seq=seq, n_heads=n_heads, hd=hd, tq=tq, tk=tk)
    sgu = _spatial_gating(proj, sgu_ln_g[0], sgu_ln_b[0], sgu_w[0], sgu_b[0],
                          u_col0=2 * q_cols + attn_width)
    x1 = _out_projection(att, sgu, w_out[0].astype(BF16), xt)

    hp, top_idx, gates, rank, counts = _router(x1, ffn_norm_g[0], router_w[0], router_b[0])
    dest, pends, pcounts, meta, n_rows = _routing_tables(
        top_idx, rank, counts[0].astype(jnp.int32), tm)
    xs = _dispatch(hp, dest, pends, pcounts, n_rows=n_rows, tm=tm)
    hs = _expert_up(xs, meta, w_gate_up[0], b_gate_up[0], tm=tm)
    ys = _expert_down(hs, meta, w_down[0], b_down[0], tm=tm)
    out = _combine(x1, gates, dest, ys, final_norm_g)
    return out.reshape(batch, seq, d)
```
